```python
import jax, jax.numpy as jnp
from jax import lax
import numpy as np

D_MODEL = 4096
BATCH = 2
SEQ = 8192
DEPTH = 2

GRID_W = 64
CTX_LEN = 256
N_MIXERS = 2
N_MOD = 6
RMS_EPS = 1e-6
NEG_INF = -1e30

NA_HEAD_DIM = 128
NA_HEADS = D_MODEL // NA_HEAD_DIM
NA_WIN_H = 8
NA_WIN_W = 16

RW_HEAD_DIM = 64
RW_HEADS = D_MODEL // RW_HEAD_DIM
RW_DECAY_LORA = 128
RW_ICLR_LORA = 128
RW_GATE_LORA = 480
RW_GN_EPS = 64e-5
RW_N_LERP = 6
RW_N_DIR = 2

N_EXPERTS = 32
TOP_K = 4
D_EXPERT = 512
SWIGLU_LIMIT = 7.0
SWIGLU_ALPHA = 1.702
MOE_BLOCK = 128

kernel_name = 'hybrid_natten_rwkv7_moe_diffusion_trunk'


def rms_norm(x, gain):
    xf = x.astype(jnp.float32)
    y = xf * lax.rsqrt(jnp.mean(xf * xf, axis=-1, keepdims=True) + RMS_EPS)
    return (y * gain.astype(jnp.float32)).astype(x.dtype)


def ada_modulation(cond, ada_w, ada_b):
    return jnp.split(jax.nn.silu(cond) @ ada_w + ada_b, N_MOD, axis=-1)


def neighbourhood_attention(h, hc, w_qkv, q_gain, k_gain, rpb, w_o, with_ctx_out):
    B, L, D = h.shape
    C = hc.shape[1]
    rows = L // GRID_W
    kh = min(NA_WIN_H, rows)
    scale = NA_HEAD_DIM ** -0.5

    def project(t):
        q, k, v = jnp.split(t @ w_qkv, 3, axis=-1)
        heads = lambda z: z.reshape(t.shape[0], t.shape[1], NA_HEADS, NA_HEAD_DIM)
        return rms_norm(heads(q), q_gain), rms_norm(heads(k), k_gain), heads(v)

    q, k, v = project(h)
    qc, kc, vc = project(hc)
    q_rows = jnp.moveaxis(q.reshape(B, rows, GRID_W, NA_HEADS, NA_HEAD_DIM), 1, 0)
    k_grid = k.reshape(B, rows, GRID_W, NA_HEADS, NA_HEAD_DIM)
    v_grid = v.reshape(B, rows, GRID_W, NA_HEADS, NA_HEAD_DIM)

    row_start = np.clip(np.arange(rows) - kh // 2, 0, rows - kh).astype(np.int32)
    col = np.arange(GRID_W)
    col_start = np.clip(col - NA_WIN_W // 2, 0, GRID_W - NA_WIN_W)
    in_win = (col[None, :] >= col_start[:, None]) & (col[None, :] < col_start[:, None] + NA_WIN_W)
    win_mask = np.broadcast_to(in_win[:, None, :], (GRID_W, kh, GRID_W)).reshape(GRID_W, kh * GRID_W)
    dc_idx = np.clip(col[None, :] - col[:, None] + NA_WIN_W - 1, 0, 2 * NA_WIN_W - 2)
    rpb_cols = rpb.astype(jnp.float32)[:, :, dc_idx]

    def row_block(args):
        q_r, r0, r = args
        k_r = lax.dynamic_slice_in_dim(k_grid, r0, kh, axis=1).reshape(B, kh * GRID_W, NA_HEADS, NA_HEAD_DIM)
        v_r = lax.dynamic_slice_in_dim(v_grid, r0, kh, axis=1).reshape(B, kh * GRID_W, NA_HEADS, NA_HEAD_DIM)
        dr = r0 + jnp.arange(kh, dtype=jnp.int32) - r + NA_WIN_H - 1
        bias = jnp.transpose(rpb_cols[:, dr], (0, 2, 1, 3)).reshape(NA_HEADS, GRID_W, kh * GRID_W)
        bias = jnp.where(win_mask, bias, NEG_INF)
        s_lat = jnp.einsum('bqhd,bkhd->bhqk', q_r, k_r).astype(jnp.float32) * scale + bias
        s_ctx = jnp.einsum('bqhd,bkhd->bhqk', q_r, kc).astype(jnp.float32) * scale
        p = jax.nn.softmax(jnp.concatenate([s_ctx, s_lat], axis=-1), axis=-1).astype(v.dtype)
        return (jnp.einsum('bhqk,bkhd->bqhd', p[..., :C], vc)
                + jnp.einsum('bhqk,bkhd->bqhd', p[..., C:], v_r))

    o = lax.map(row_block, (q_rows, jnp.asarray(row_start), jnp.arange(rows, dtype=jnp.int32)))
    y = jnp.moveaxis(o, 0, 1).reshape(B, L, D) @ w_o
    yc = None
    if with_ctx_out:
        s = jnp.einsum('bqhd,bkhd->bhqk', qc, kc).astype(jnp.float32) * scale
        p = jax.nn.softmax(s, axis=-1).astype(vc.dtype)
        yc = jnp.einsum('bhqk,bkhd->bqhd', p, vc).reshape(B, C, D) @ w_o
    return y, yc


def token_shift(t):
    half = t.shape[-1] // 2
    prev = jnp.pad(t[:, :-1, :half], ((0, 0), (1, 0), (0, 0)))
    nxt = jnp.pad(t[:, 1:, half:], ((0, 0), (0, 1), (0, 0)))
    return jnp.concatenate([prev, nxt], axis=-1)


def seq_flip(z):
    return z[:, ::-1]


def wkv7_scan(state, r, w, k, v, a, b):
    def step(S, inp):
        r_t, w_t, k_t, v_t, a_t, b_t = inp
        sa = jnp.einsum('bhvk,bhk->bhv', S, a_t)
        S = S * w_t[:, :, None, :] + sa[..., None] * b_t[:, :, None, :] + v_t[..., None] * k_t[:, :, None, :]
        return S, jnp.einsum('bhvk,bhk->bhv', S, r_t)
    xs = tuple(jnp.swapaxes(z.astype(jnp.float32), 0, 1) for z in (r, w, k, v, a, b))
    state, y = lax.scan(step, state, xs)
    return state, jnp.swapaxes(y, 0, 1)


def rwkv7_time_mix(h, hc, mix, w_r, w_k, w_v, k_k, k_a, r_k, w0, w1, w2, a0, a1, a2, g1, g2,
                   gn_w, gn_b, w_o, with_ctx_out):
    k_k_h = k_k.reshape(RW_HEADS, RW_HEAD_DIM).astype(jnp.float32)
    k_a_h = k_a.reshape(RW_HEADS, RW_HEAD_DIM).astype(jnp.float32)
    r_k_f = r_k.astype(jnp.float32)

    def branch(t):
        Bt, Lt, _ = t.shape
        heads = lambda z: z.reshape(Bt, Lt, RW_HEADS, RW_HEAD_DIM)
        dx = token_shift(t) - t
        xr, xw, xk, xv, xa, xg = [t + dx * mix[j] for j in range(RW_N_LERP)]
        r = heads(xr @ w_r).astype(jnp.float32)
        k = heads(xk @ w_k).astype(jnp.float32)
        v = heads(xv @ w_v).astype(jnp.float32)
        kk = k * k_k_h
        kk = kk * lax.rsqrt(jnp.sum(kk * kk, axis=-1, keepdims=True) + 1e-12)
        dirs = []
        for d in range(RW_N_DIR):
            w_log = -jax.nn.softplus(-(w0[d] + jnp.tanh(xw @ w1[d]) @ w2[d]).astype(jnp.float32)) - 0.5
            decay = heads(jnp.exp(-jnp.exp(w_log)))
            a = heads(jax.nn.sigmoid((a0[d] + (xa @ a1[d]) @ a2[d]).astype(jnp.float32)))
            g = jax.nn.sigmoid(xg @ g1[d]) @ g2[d]
            k_d = k * (1.0 + (a - 1.0) * k_a_h)
            dirs.append((decay, k_d, -kk, kk * a, g))
        return r, v, dirs

    def direction_out(y, r, k_d, v, g):
        Bt, Lt = y.shape[:2]
        mu = jnp.mean(y, axis=-1, keepdims=True)
        var = jnp.mean(jnp.square(y - mu), axis=-1, keepdims=True)
        yn = ((y - mu) * lax.rsqrt(var + RW_GN_EPS)).reshape(Bt, Lt, D_MODEL) * gn_w + gn_b
        bonus = (jnp.sum(r * k_d * r_k_f, axis=-1, keepdims=True) * v).reshape(Bt, Lt, D_MODEL)
        return (yn + bonus).astype(g.dtype) * g

    r_l, v_l, dirs_l = branch(h)
    r_c, v_c, dirs_c = branch(hc)
    state0 = jnp.zeros((h.shape[0], RW_HEADS, RW_HEAD_DIM, RW_HEAD_DIM), jnp.float32)
    outs_l, outs_c = [], []
    for d in range(RW_N_DIR):
        order = (lambda z: z) if d == 0 else seq_flip
        dw_c, dk_c, da_c, db_c, dg_c = dirs_c[d]
        dw_l, dk_l, da_l, db_l, dg_l = dirs_l[d]
        s_ctx, y_c = wkv7_scan(state0, *[order(z) for z in (r_c, dw_c, dk_c, v_c, da_c, db_c)])
        _, y_l = wkv7_scan(s_ctx, *[order(z) for z in (r_l, dw_l, dk_l, v_l, da_l, db_l)])
        outs_l.append(direction_out(order(y_l), r_l, dk_l, v_l, dg_l))
        if with_ctx_out:
            outs_c.append(direction_out(order(y_c), r_c, dk_c, v_c, dg_c))
    y = (outs_l[0] + outs_l[1]) @ w_o
    yc = (outs_c[0] + outs_c[1]) @ w_o if with_ctx_out else None
    return y, yc


def moe_ffn(t, router_w, router_b, w_gate, b_gate, w_up, b_up, w_down, b_down):
    N, D = t.shape
    logits = (t @ router_w + router_b).astype(jnp.float32)
    top_val, top_idx = lax.top_k(logits, TOP_K)
    top_w = jax.nn.softmax(top_val, axis=-1).astype(t.dtype)
    flat_e = top_idx.reshape(-1)
    flat_tok = jnp.repeat(jnp.arange(N, dtype=jnp.int32), TOP_K)
    flat_w = top_w.reshape(-1)
    order = jnp.argsort(flat_e)
    e_sorted = flat_e[order]
    counts = jax.ops.segment_sum(jnp.ones_like(flat_e), flat_e, num_segments=N_EXPERTS)
    padded = (counts + MOE_BLOCK - 1) // MOE_BLOCK * MOE_BLOCK
    pad_end = jnp.cumsum(padded)
    pad_start = pad_end - padded
    grp_start = jnp.cumsum(counts) - counts
    rank = jnp.arange(N * TOP_K, dtype=jnp.int32) - grp_start[e_sorted]
    dest = pad_start[e_sorted] + rank
    n_blocks = (N * TOP_K + MOE_BLOCK - 1) // MOE_BLOCK + N_EXPERTS
    n_rows = n_blocks * MOE_BLOCK
    row_tok = jnp.full((n_rows,), N, jnp.int32).at[dest].set(flat_tok[order])
    row_w = jnp.zeros((n_rows,), t.dtype).at[dest].set(flat_w[order])
    blk_start = jnp.arange(n_blocks, dtype=jnp.int32) * MOE_BLOCK
    blk_exp = jnp.minimum(jnp.sum(blk_start[:, None] >= pad_end[None, :], axis=1), N_EXPERTS - 1)
    t_pad = jnp.concatenate([t, jnp.zeros((1, D), t.dtype)], axis=0)

    def expert_block(args):
        tok, wt, e = args
        xb = t_pad[tok]
        g = jnp.minimum(xb @ w_gate[e] + b_gate[e], SWIGLU_LIMIT)
        u = jnp.clip(xb @ w_up[e] + b_up[e], -SWIGLU_LIMIT, SWIGLU_LIMIT)
        hdn = g * jax.nn.sigmoid(SWIGLU_ALPHA * g) * (u + 1.0)
        return (hdn @ w_down[e] + b_down[e]) * wt[:, None]

    ys = lax.map(expert_block, (row_tok.reshape(n_blocks, MOE_BLOCK), row_w.reshape(n_blocks, MOE_BLOCK), blk_exp))
    out = jax.ops.segment_sum(ys.reshape(n_rows, D), row_tok, num_segments=N + 1)
    return out[:N]


def setup_inputs(seed: int = 0) -> dict:
    key = jax.random.key(seed)
    keys = jax.random.split(key, 128)
    ctr = [0]

    def nxt():
        ctr[0] += 1
        return keys[ctr[0]]

    def nrm(shape, std):
        return jax.random.normal(nxt(), shape, jnp.float32) * std

    def unif(shape, lo, hi):
        return jax.random.uniform(nxt(), shape, jnp.float32, lo, hi)

    D, H, N = D_MODEL, RW_HEADS, RW_HEAD_DIM
    inp = {}
    inp['x'] = nrm((BATCH, SEQ, D), 1.0)
    inp['c'] = nrm((BATCH, D), 1.0)
    inp['ctx'] = nrm((BATCH, CTX_LEN, D), 1.0)
    inp['c_ctx'] = nrm((D,), 1.0)
    for i in range(DEPTH):
        p = 'l%d_' % i
        inp[p + 'ada_w'] = nrm((D, N_MOD * D), 0.5 * D ** -0.5)
        inp[p + 'ada_b'] = nrm((N_MOD * D,), 0.02)
        inp[p + 'norm_mix'] = 1.0 + nrm((D,), 0.02)
        inp[p + 'norm_ffn'] = 1.0 + nrm((D,), 0.02)
        if i % N_MIXERS == 0:
            inp[p + 'na_w_qkv'] = nrm((D, 3 * D), D ** -0.5)
            inp[p + 'na_q_gain'] = 1.0 + nrm((NA_HEAD_DIM,), 0.02)
            inp[p + 'na_k_gain'] = 1.0 + nrm((NA_HEAD_DIM,), 0.02)
            inp[p + 'na_rpb'] = nrm((NA_HEADS, 2 * NA_WIN_H - 1, 2 * NA_WIN_W - 1), 0.1)
            inp[p + 'na_w_o'] = nrm((D, D), D ** -0.5)
        else:
            inp[p + 'rw_mix'] = unif((RW_N_LERP, D), 0.0, 1.0)
            inp[p + 'rw_w_r'] = nrm((D, D), D ** -0.5)
            inp[p + 'rw_w_k'] = nrm((D, D), D ** -0.5)
            inp[p + 'rw_w_v'] = nrm((D, D), D ** -0.5)
            inp[p + 'rw_k_k'] = 0.85 + nrm((D,), 0.05)
            inp[p + 'rw_k_a'] = 1.0 + nrm((D,), 0.05)
            inp[p + 'rw_r_k'] = nrm((H, N), 0.1)
            inp[p + 'rw_w0'] = unif((RW_N_DIR, D), -6.0, -1.0)
            inp[p + 'rw_w1'] = nrm((RW_N_DIR, D, RW_DECAY_LORA), D ** -0.5)
            inp[p + 'rw_w2'] = nrm((RW_N_DIR, RW_DECAY_LORA, D), 0.1 * RW_DECAY_LORA ** -0.5)
            inp[p + 'rw_a0'] = nrm((RW_N_DIR, D), 0.1)
            inp[p + 'rw_a1'] = nrm((RW_N_DIR, D, RW_ICLR_LORA), D ** -0.5)
            inp[p + 'rw_a2'] = nrm((RW_N_DIR, RW_ICLR_LORA, D), 0.1 * RW_ICLR_LORA ** -0.5)
            inp[p + 'rw_g1'] = nrm((RW_N_DIR, D, RW_GATE_LORA), D ** -0.5)
            inp[p + 'rw_g2'] = nrm((RW_N_DIR, RW_GATE_LORA, D), RW_GATE_LORA ** -0.5)
            inp[p + 'rw_gn_w'] = 1.0 + nrm((D,), 0.02)
            inp[p + 'rw_gn_b'] = nrm((D,), 0.02)
            inp[p + 'rw_w_o'] = nrm((D, D), D ** -0.5)
        inp[p + 'router_w'] = nrm((D, N_EXPERTS), D ** -0.5)
        inp[p + 'router_b'] = nrm((N_EXPERTS,), 0.01)
        inp[p + 'moe_w_gate'] = nrm((N_EXPERTS, D, D_EXPERT), D ** -0.5)
        inp[p + 'moe_b_gate'] = nrm((N_EXPERTS, D_EXPERT), 0.01)
        inp[p + 'moe_w_up'] = nrm((N_EXPERTS, D, D_EXPERT), D ** -0.5)
        inp[p + 'moe_b_up'] = nrm((N_EXPERTS, D_EXPERT), 0.01)
        inp[p + 'moe_w_down'] = nrm((N_EXPERTS, D_EXPERT, D), D_EXPERT ** -0.5)
        inp[p + 'moe_b_down'] = nrm((N_EXPERTS, D), 0.01)
    return inp


def reference(x, c, ctx, c_ctx,
              l0_ada_w, l0_ada_b, l0_norm_mix, l0_norm_ffn,
              l0_na_w_qkv, l0_na_q_gain, l0_na_k_gain, l0_na_rpb, l0_na_w_o,
              l0_router_w, l0_router_b, l0_moe_w_gate, l0_moe_b_gate, l0_moe_w_up, l0_moe_b_up,
              l0_moe_w_down, l0_moe_b_down,
              l1_ada_w, l1_ada_b, l1_norm_mix, l1_norm_ffn,
              l1_rw_mix, l1_rw_w_r, l1_rw_w_k, l1_rw_w_v, l1_rw_k_k, l1_rw_k_a, l1_rw_r_k,
              l1_rw_w0, l1_rw_w1, l1_rw_w2, l1_rw_a0, l1_rw_a1, l1_rw_a2, l1_rw_g1, l1_rw_g2,
              l1_rw_gn_w, l1_rw_gn_b, l1_rw_w_o,
              l1_router_w, l1_router_b, l1_moe_w_gate, l1_moe_b_gate, l1_moe_w_up, l1_moe_b_up,
              l1_moe_w_down, l1_moe_b_down):
    mixer0 = lambda h, hc, out_ctx: neighbourhood_attention(
        h, hc, l0_na_w_qkv, l0_na_q_gain, l0_na_k_gain, l0_na_rpb, l0_na_w_o, out_ctx)
    mixer1 = lambda h, hc, out_ctx: rwkv7_time_mix(
        h, hc, l1_rw_mix, l1_rw_w_r, l1_rw_w_k, l1_rw_w_v, l1_rw_k_k, l1_rw_k_a, l1_rw_r_k,
        l1_rw_w0, l1_rw_w1, l1_rw_w2, l1_rw_a0, l1_rw_a1, l1_rw_a2, l1_rw_g1, l1_rw_g2,
        l1_rw_gn_w, l1_rw_gn_b, l1_rw_w_o, out_ctx)
    layers = [
        (l0_ada_w, l0_ada_b, l0_norm_mix, l0_norm_ffn, mixer0,
         (l0_router_w, l0_router_b, l0_moe_w_gate, l0_moe_b_gate, l0_moe_w_up, l0_moe_b_up, l0_moe_w_down, l0_moe_b_down)),
        (l1_ada_w, l1_ada_b, l1_norm_mix, l1_norm_ffn, mixer1,
         (l1_router_w, l1_router_b, l1_moe_w_gate, l1_moe_b_gate, l1_moe_w_up, l1_moe_b_up, l1_moe_w_down, l1_moe_b_down)),
    ]
    B, L, D = x.shape
    C = ctx.shape[1]
    xc = ctx
    for i in range(DEPTH):
        ada_w, ada_b, norm_mix, norm_ffn, mixer, moe_p = layers[i]
        ctx_out = i < DEPTH - 1
        sh1, sc1, gt1, sh2, sc2, gt2 = ada_modulation(c[:, None, :], ada_w, ada_b)
        csh1, csc1, cgt1, csh2, csc2, cgt2 = ada_modulation(c_ctx, ada_w, ada_b)
        y, yc = mixer(rms_norm(x, norm_mix) * (1.0 + sc1) + sh1,
                      rms_norm(xc, norm_mix) * (1.0 + csc1) + csh1, ctx_out)
        x = x + gt1 * y
        h = rms_norm(x, norm_ffn) * (1.0 + sc2) + sh2
        if ctx_out:
            xc = xc + cgt1 * yc
            hc = rms_norm(xc, norm_ffn) * (1.0 + csc2) + csh2
            f = moe_ffn(jnp.concatenate([hc.reshape(B * C, D), h.reshape(B * L, D)], axis=0), *moe_p)
            xc = xc + cgt2 * f[:B * C].reshape(B, C, D)
            x = x + gt2 * f[B * C:].reshape(B, L, D)
        else:
            x = x + gt2 * moe_ffn(h.reshape(B * L, D), *moe_p).reshape(B, L, D)
    return x
```

```python
import functools

import numpy as np
import jax
import jax.numpy as jnp
from jax import lax
from jax.experimental import pallas as pl
from jax.experimental.pallas import tpu as pltpu

F32 = jnp.float32
BF16 = jnp.bfloat16

GRID_W = 64
N_MOD = 6
RMS_EPS = 1e-6
NEG_INF = -1e30
NA_HEAD_DIM = 128
NA_WIN_H = 8
NA_WIN_W = 16
RW_HEAD_DIM = 64
RW_GN_EPS = 64e-5
TOP_K = 4
SWIGLU_LIMIT = 7.0
SWIGLU_ALPHA = 1.702

V7X_LANES = 128
VMEM_LIMIT_BYTES = 56 * 1024 * 1024

ROW_TILE = 256
NA_QROWS = 4
NA_KROWS = 12
MOE_TILE = 256
WKV_T = 64
WKV_BLOCK = 256
WKV_LANES = 256


def _cparams(*sem):
    return pltpu.CompilerParams(dimension_semantics=sem, vmem_limit_bytes=VMEM_LIMIT_BYTES)


def _mm_body(*refs, n_a, n_pv, n_ev, n_em, n_out, prologue, epilogue, chunk):
    a_refs = refs[:n_a]
    pv_refs = refs[n_a:n_a + n_pv]
    w_ref = refs[n_a + n_pv]
    base = n_a + n_pv + 1
    ev_refs = refs[base:base + n_ev]
    em_refs = refs[base + n_ev:base + n_ev + n_em]
    out_refs = refs[base + n_ev + n_em:base + n_ev + n_em + n_out]
    scratch = refs[base + n_ev + n_em + n_out:]
    if prologue is not None:
        a_scr = scratch[0]

        @pl.when(pl.program_id(1) == 0)
        def _():
            pvs = [r[...] for r in pv_refs]

            def body(c, carry):
                rows = pl.ds(pl.multiple_of(c * chunk, chunk), chunk)
                a_scr[rows, :] = prologue(*[r[rows, :] for r in a_refs], *pvs).astype(BF16)
                return carry

            lax.fori_loop(0, a_scr.shape[0] // chunk, body, 0)

        a = a_scr[...]
    else:
        a = a_refs[0][...]
    acc = jnp.dot(a, w_ref[...].astype(BF16), preferred_element_type=F32)
    evs = [r[...] for r in ev_refs]
    ems = [r[...] for r in em_refs]
    if epilogue is None:
        out_refs[0][...] = acc.astype(out_refs[0].dtype)
    else:
        epilogue(acc, *evs, *ems, *out_refs)


def _matmul(a_list, w, *, out_dtypes=(F32,), prologue=None, prologue_vecs=(), epilogue=None,
            epi_vecs=(), epi_mats=(), a_col=0, tm=None, tn=None, name="mm"):
    M = a_list[0].shape[0]
    K, N = w.shape
    if tm is None:
        tm = 512 if M % 512 == 0 else (256 if M % 256 == 0 else M)
    if tn is None:
        tn = 512 if N % 512 == 0 else (256 if N % 256 == 0 else N)
    assert M % tm == 0 and N % tn == 0
    chunk = 32 if tm % 32 == 0 else tm
    in_specs = [pl.BlockSpec((tm, K), lambda m, n: (m, a_col)) for _ in a_list]
    in_specs += [pl.BlockSpec((1, K), lambda m, n: (0, 0)) for _ in prologue_vecs]
    in_specs += [pl.BlockSpec((K, tn), lambda m, n: (0, n))]
    in_specs += [pl.BlockSpec((1, tn), lambda m, n: (0, n)) for _ in epi_vecs]
    in_specs += [pl.BlockSpec((tm, tn), lambda m, n: (m, n)) for _ in epi_mats]
    out_shape = [jax.ShapeDtypeStruct((M, N), dt) for dt in out_dtypes]
    out_specs = [pl.BlockSpec((tm, tn), lambda m, n: (m, n)) for _ in out_dtypes]
    scratch = [pltpu.VMEM((tm, K), BF16)] if prologue is not None else []
    body = functools.partial(_mm_body, n_a=len(a_list), n_pv=len(prologue_vecs), n_ev=len(epi_vecs),
                             n_em=len(epi_mats), n_out=len(out_dtypes), prologue=prologue, epilogue=epilogue,
                             chunk=chunk)
    outs = pl.pallas_call(
        body, out_shape=out_shape, grid=(M // tm, N // tn), in_specs=in_specs, out_specs=out_specs,
        scratch_shapes=scratch, compiler_params=_cparams("parallel", "arbitrary"), name=name,
    )(*a_list, *prologue_vecs, w, *epi_vecs, *epi_mats)
    return outs[0] if len(out_dtypes) == 1 else outs


def _rows_body(*refs, n_add, gate_rows, do_norm, shift_row, scale_row, has_router, write_x, chunk):
    x_ref = refs[0]
    add_refs = refs[1:1 + n_add]
    mod_ref = refs[1 + n_add]
    pos = 2 + n_add
    if do_norm:
        nw_ref = refs[pos]
        pos += 1
    if has_router:
        rw_ref, rb_ref = refs[pos], refs[pos + 1]
        pos += 2
    outs = refs[pos:]
    oi = 0
    if write_x:
        xo_ref = outs[oi]
        oi += 1
    if do_norm:
        h_ref = outs[oi]
        oi += 1
    if has_router:
        lg_ref = outs[oi]
    gates = [mod_ref[0, g:g + 1, :] for g in gate_rows]
    if do_norm:
        nw = nw_ref[...]
        sh = mod_ref[0, shift_row:shift_row + 1, :]
        sc = 1.0 + mod_ref[0, scale_row:scale_row + 1, :]

    def body(c, carry):
        rows = pl.ds(pl.multiple_of(c * chunk, chunk), chunk)
        x = x_ref[rows, :]
        for a_ref, g in zip(add_refs, gates):
            x = x + g * a_ref[rows, :].astype(F32)
        if write_x:
            xo_ref[rows, :] = x
        if do_norm:
            y = x * lax.rsqrt(jnp.mean(x * x, axis=-1, keepdims=True) + RMS_EPS)
            h_ref[rows, :] = ((y * nw) * sc + sh).astype(h_ref.dtype)
        return carry

    lax.fori_loop(0, x_ref.shape[0] // chunk, body, 0)
    if has_router:
        lg_ref[...] = jnp.dot(h_ref[...], rw_ref[...], preferred_element_type=F32) + rb_ref[...]


def _rows(x, adds, gate_rows, mod, *, tiles_per_batch, ctx_tiles, norm_w=None, shift_row=0, scale_row=1,
          router=None, write_x=True, name="rows"):
    M, D = x.shape
    tr = ROW_TILE
    do_norm = norm_w is not None

    def mod_idx(i):
        return ((i // tiles_per_batch) * 2 + ((i % tiles_per_batch) >= ctx_tiles).astype(jnp.int32), 0, 0)

    row_spec = pl.BlockSpec((tr, D), lambda i: (i, 0))
    args = [x, *adds, mod]
    in_specs = [row_spec] * (1 + len(adds)) + [pl.BlockSpec((1, 8, D), mod_idx)]
    if do_norm:
        args.append(norm_w.reshape(1, D))
        in_specs.append(pl.BlockSpec((1, D), lambda i: (0, 0)))
    if router is not None:
        rw, rb = router
        args += [rw, rb]
        in_specs += [pl.BlockSpec(rw.shape, lambda i: (0, 0)), pl.BlockSpec(rb.shape, lambda i: (0, 0))]
    out_shape, out_specs = [], []
    if write_x:
        out_shape.append(jax.ShapeDtypeStruct((M, D), F32))
        out_specs.append(row_spec)
    if do_norm:
        out_shape.append(jax.ShapeDtypeStruct((M, D), BF16))
        out_specs.append(row_spec)
    if router is not None:
        out_shape.append(jax.ShapeDtypeStruct((M, V7X_LANES), F32))
        out_specs.append(pl.BlockSpec((tr, V7X_LANES), lambda i: (i, 0)))
    body = functools.partial(_rows_body, n_add=len(adds), gate_rows=tuple(gate_rows), do_norm=do_norm,
                             shift_row=shift_row, scale_row=scale_row, has_router=router is not None,
                             write_x=write_x, chunk=32)
    return pl.pallas_call(body, out_shape=out_shape, grid=(M // tr,), in_specs=in_specs, out_specs=out_specs,
                          compiler_params=_cparams("parallel"), name=name)(*args)


def _ada_mod(c, c_ctx, ada_w, ada_b):
    B, D = c.shape
    cond = jnp.concatenate([c, c_ctx[None, :], jnp.zeros((8 - B - 1, D), F32)], axis=0)

    def epi(acc, bias, out_ref):
        out_ref[...] = acc + bias

    out = _matmul([cond], ada_w, prologue=lambda a: a * jax.nn.sigmoid(a), epilogue=epi,
                  epi_vecs=[ada_b.reshape(1, -1)], tm=8, tn=512, name="ada_mod")
    lat = out[:B].reshape(B, N_MOD, D)
    ctx = jnp.broadcast_to(out[B].reshape(1, N_MOD, D), (B, N_MOD, D))
    mod = jnp.stack([ctx, lat], axis=1).reshape(2 * B, N_MOD, D)
    return jnp.pad(mod, ((0, 0), (0, 8 - N_MOD), (0, 0)))


def _qkv_epilogue(acc, gain, out_ref, *, n_norm_tiles):
    tn = acc.shape[1]

    @pl.when(pl.program_id(1) < n_norm_tiles)
    def _():
        for i in range(tn // NA_HEAD_DIM):
            sl = slice(i * NA_HEAD_DIM, (i + 1) * NA_HEAD_DIM)
            z = acc[:, sl]
            z = z * lax.rsqrt(jnp.mean(z * z, axis=-1, keepdims=True) + RMS_EPS)
            out_ref[:, sl] = (z * gain[:, sl]).astype(out_ref.dtype)

    @pl.when(pl.program_id(1) >= n_norm_tiles)
    def _():
        out_ref[...] = acc.astype(out_ref.dtype)


def _na_bias_tables(rpb, rows):
    nblk = rows // NA_QROWS
    kh = min(NA_WIN_H, rows)
    assert kh == NA_WIN_H and rows >= NA_KROWS + NA_QROWS and rows % NA_QROWS == 0
    col = np.arange(GRID_W)
    col_start = np.clip(col - NA_WIN_W // 2, 0, GRID_W - NA_WIN_W)
    col_ok = (col[None, :] >= col_start[:, None]) & (col[None, :] < col_start[:, None] + NA_WIN_W)
    dc = np.clip(col[None, :] - col[:, None] + NA_WIN_W - 1, 0, 2 * NA_WIN_W - 2)
    tabs = []
    for j in (0, 1, nblk - 1):
        s0 = int(np.clip(NA_QROWS * j - NA_WIN_H // 2, 0, rows - NA_KROWS))
        qr = NA_QROWS * j + np.arange(NA_QROWS)
        rs = np.clip(qr - kh // 2, 0, rows - kh)
        kr = s0 + np.arange(NA_KROWS)
        row_ok = (kr[None, :] >= rs[:, None]) & (kr[None, :] < rs[:, None] + kh)
        dr = np.clip(kr[None, :] - qr[:, None] + NA_WIN_H - 1, 0, 2 * NA_WIN_H - 2)
        ok = row_ok[:, None, :, None] & col_ok[None, :, None, :]
        shape = (NA_QROWS, GRID_W, NA_KROWS, GRID_W)
        dr_f = np.broadcast_to(dr[:, None, :, None], shape).reshape(NA_QROWS * GRID_W, NA_KROWS * GRID_W)
        dc_f = np.broadcast_to(dc[None, :, None, :], shape).reshape(NA_QROWS * GRID_W, NA_KROWS * GRID_W)
        ok_f = ok.reshape(NA_QROWS * GRID_W, NA_KROWS * GRID_W)
        tabs.append(jnp.where(ok_f[None], rpb.astype(F32)[:, dr_f, dc_f], NEG_INF))
    return jnp.stack(tabs, axis=0)


def _na_body(q_ref, k_ref, v_ref, bias_ref, o_ref, *, C, rows):
    nq = NA_QROWS * GRID_W
    nk = NA_KROWS * GRID_W
    nblk = rows // NA_QROWS
    dn = (((1,), (1,)), ((), ()))
    kc = k_ref[0:C, :]
    vc = v_ref[0:C, :]
    s = lax.dot_general(q_ref[0:C, :], kc, dn, preferred_element_type=F32)
    p = jnp.exp(s - jnp.max(s, axis=-1, keepdims=True))
    o = jnp.dot(p.astype(BF16), vc, preferred_element_type=F32) / jnp.sum(p, axis=-1, keepdims=True)
    o_ref[0:C, :] = o.astype(o_ref.dtype)

    def body(j, carry):
        s0 = jnp.clip(NA_QROWS * j - NA_WIN_H // 2, 0, rows - NA_KROWS)
        cls = jnp.where(j == 0, 0, jnp.where(j == nblk - 1, 2, 1))
        qrows = pl.ds(pl.multiple_of(C + j * nq, GRID_W), nq)
        krows = pl.ds(pl.multiple_of(C + s0 * GRID_W, GRID_W), nk)
        qb = q_ref[qrows, :]
        kb = k_ref[krows, :]
        vb = v_ref[krows, :]
        s_lat = lax.dot_general(qb, kb, dn, preferred_element_type=F32) + bias_ref[cls, 0]
        s_ctx = lax.dot_general(qb, kc, dn, preferred_element_type=F32)
        m = jnp.maximum(jnp.max(s_lat, axis=-1, keepdims=True), jnp.max(s_ctx, axis=-1, keepdims=True))
        p_lat = jnp.exp(s_lat - m)
        p_ctx = jnp.exp(s_ctx - m)
        den = jnp.sum(p_lat, axis=-1, keepdims=True) + jnp.sum(p_ctx, axis=-1, keepdims=True)
        o = (jnp.dot(p_ctx.astype(BF16), vc, preferred_element_type=F32)
             + jnp.dot(p_lat.astype(BF16), vb, preferred_element_type=F32)) / den
        o_ref[qrows, :] = o.astype(o_ref.dtype)
        return carry

    lax.fori_loop(0, nblk, body, 0)


def _neighbourhood_attention(qkv, bias_tabs, *, B, C, L, D):
    H = D // NA_HEAD_DIM
    Lt = C + L
    rows = L // GRID_W
    blk = (Lt, NA_HEAD_DIM)
    body = functools.partial(_na_body, C=C, rows=rows)
    return pl.pallas_call(
        body, out_shape=jax.ShapeDtypeStruct((B * Lt, D), BF16), grid=(B, H),
        in_specs=[pl.BlockSpec(blk, lambda b, h: (b, h)),
                  pl.BlockSpec(blk, lambda b, h: (b, H + h)),
                  pl.BlockSpec(blk, lambda b, h: (b, 2 * H + h)),
                  pl.BlockSpec((3, 1) + bias_tabs.shape[2:], lambda b, h: (0, h, 0, 0))],
        out_specs=pl.BlockSpec(blk, lambda b, h: (b, h)),
        compiler_params=_cparams("parallel", "parallel"), name="na_attention",
    )(qkv, qkv, qkv, bias_tabs)


def _moe_body(be_ref, x_ref, wg_ref, bg_ref, wu_ref, bu_ref, wd_ref, bd_ref, o_ref):
    del be_ref
    x = x_ref[...]
    g = jnp.minimum(jnp.dot(x, wg_ref[0], preferred_element_type=F32) + bg_ref[0], SWIGLU_LIMIT)
    u = jnp.clip(jnp.dot(x, wu_ref[0], preferred_element_type=F32) + bu_ref[0], -SWIGLU_LIMIT, SWIGLU_LIMIT)
    hdn = g * jax.nn.sigmoid(SWIGLU_ALPHA * g) * (u + 1.0)
    y = jnp.dot(hdn.astype(BF16), wd_ref[0], preferred_element_type=F32) + bd_ref[0]
    o_ref[...] = y.astype(o_ref.dtype)


def _moe_experts(xs, blk_exp, wg, bg, wu, bu, wd, bd):
    n_rows, D = xs.shape
    E, _, De = wg.shape
    tm = MOE_TILE
    grid_spec = pltpu.PrefetchScalarGridSpec(
        num_scalar_prefetch=1, grid=(n_rows // tm,),
        in_specs=[pl.BlockSpec((tm, D), lambda i, be: (i, 0)),
                  pl.BlockSpec((1, D, De), lambda i, be: (be[i], 0, 0)),
                  pl.BlockSpec((1, 1, De), lambda i, be: (be[i], 0, 0)),
                  pl.BlockSpec((1, D, De), lambda i, be: (be[i], 0, 0)),
                  pl.BlockSpec((1, 1, De), lambda i, be: (be[i], 0, 0)),
                  pl.BlockSpec((1, De, D), lambda i, be: (be[i], 0, 0)),
                  pl.BlockSpec((1, 1, D), lambda i, be: (be[i], 0, 0))],
        out_specs=pl.BlockSpec((tm, D), lambda i, be: (i, 0)))
    return pl.pallas_call(
        _moe_body, out_shape=jax.ShapeDtypeStruct((n_rows, D), F32), grid_spec=grid_spec,
        compiler_params=_cparams("arbitrary"), name="moe_experts",
    )(blk_exp, xs, wg, bg.reshape(E, 1, De), wu, bu.reshape(E, 1, De), wd, bd.reshape(E, 1, D))


def _moe_ffn(h, logits, wg, bg, wu, bu, wd, bd):
    N, D = h.shape
    E = wg.shape[0]
    tm = MOE_TILE
    top_val, top_idx = lax.top_k(logits, TOP_K)
    top_w = jax.nn.softmax(top_val, axis=-1)
    flat_e = top_idx.reshape(-1)
    onehot = (flat_e[:, None] == jnp.arange(E, dtype=flat_e.dtype)[None, :]).astype(jnp.int32)
    csum = jnp.cumsum(onehot, axis=0)
    counts = csum[-1]
    rank = jnp.take_along_axis(csum, flat_e[:, None], axis=1)[:, 0] - 1
    padded = (counts + tm - 1) // tm * tm
    pad_end = jnp.cumsum(padded)
    pad_start = pad_end - padded
    dest = pad_start[flat_e] + rank
    n_blocks = (N * TOP_K + tm - 1) // tm + E
    n_rows = n_blocks * tm
    flat_tok = jnp.repeat(jnp.arange(N, dtype=jnp.int32), TOP_K)
    row_tok = jnp.full((n_rows,), N, jnp.int32).at[dest].set(flat_tok)
    blk_start = jnp.arange(n_blocks, dtype=jnp.int32) * tm
    blk_exp = jnp.minimum(jnp.sum(blk_start[:, None] >= pad_end[None, :], axis=1), E - 1).astype(jnp.int32)
    h_pad = jnp.concatenate([h, jnp.zeros((1, D), h.dtype)], axis=0)
    xs = h_pad[row_tok]
    ys = _moe_experts(xs, blk_exp, wg, bg, wu, bu, wd, bd)
    picked = ys[dest].reshape(N, TOP_K, D)
    return jnp.sum(picked * top_w[:, :, None], axis=1)


def _block_id(idx, size):
    return lax.shift_right_logical(idx, int(np.log2(size)))


def _split3(x):
    hi = x.astype(BF16).astype(F32)
    r1 = x - hi
    mid = r1.astype(BF16).astype(F32)
    lo = (r1 - mid).astype(BF16).astype(F32)
    return hi, mid, lo


def _dot_exact_lhs01(m01, x):
    hi, mid, lo = _split3(x)
    mb = m01.astype(BF16)
    return (jnp.dot(mb, hi.astype(BF16), preferred_element_type=F32)
            + jnp.dot(mb, mid.astype(BF16), preferred_element_type=F32)
            + jnp.dot(mb, lo.astype(BF16), preferred_element_type=F32))


def _dot_exact_rhs01(x, m01):
    hi, mid, lo = _split3(x)
    mb = m01.astype(BF16)
    return (jnp.dot(hi.astype(BF16), mb, preferred_element_type=F32)
            + jnp.dot(mid.astype(BF16), mb, preferred_element_type=F32)
            + jnp.dot(lo.astype(BF16), mb, preferred_element_type=F32))


def _wkv_body(r_ref, k_ref, v_ref, e_ref, a_ref, g_ref, kk_ref, ka_ref, rk_ref, gw_ref, gb_ref, o_ref,
              st_ref, rt_s, kt_s, at_s, bt_s, kw_s, bw_s, wt_s, y_s, *, n_ctx_blocks):
    T = WKV_T
    TB, LW = r_ref.shape
    n_pairs = LW // V7X_LANES
    n_chunks = TB // T
    d = pl.program_id(0)
    i = pl.program_id(3)
    rev = d == 1

    @pl.when(i == 0)
    def _():
        st_ref[...] = jnp.zeros_like(st_ref)

    ri = lax.broadcasted_iota(jnp.int32, (TB, TB), 0)
    ci = lax.broadcasted_iota(jnp.int32, (TB, TB), 1)
    same_chunk = _block_id(ri, T) == _block_id(ci, T)
    before = jnp.where(rev, ri, ci) <= jnp.where(rev, ci, ri)
    tri = jnp.where(same_chunk & before, 1.0, 0.0).astype(F32)
    ones_c = jnp.where(same_chunk, 1.0, 0.0).astype(F32)
    li = _block_id(lax.broadcasted_iota(jnp.int32, (LW, LW), 0), RW_HEAD_DIM)
    lj = _block_id(lax.broadcasted_iota(jnp.int32, (LW, LW), 1), RW_HEAD_DIM)
    head_ones = jnp.where(li == lj, 1.0, 0.0).astype(F32)

    r = r_ref[...]
    k = k_ref[...]
    e = e_ref[...]
    a = a_ref[...]
    kkv = k * kk_ref[...]
    kk = kkv * lax.rsqrt(_dot_exact_rhs01(kkv * kkv, head_ones) + 1e-12)
    kd = k * (1.0 + (a - 1.0) * ka_ref[...])
    be = kk * a
    cum = _dot_exact_lhs01(tri, e)
    tot = _dot_exact_lhs01(ones_c, e)
    ecum = jnp.exp(cum)
    encum = jnp.exp(-cum)
    edec = jnp.exp(cum - tot)
    rt_s[...] = r * encum
    kt_s[...] = kd * ecum
    at_s[...] = -kk * jnp.exp(e - cum)
    bt_s[...] = be * ecum
    kw_s[...] = kd * edec
    bw_s[...] = be * edec
    wt_s[...] = jnp.exp(-tot)

    lane = lax.broadcasted_iota(jnp.int32, (1, V7X_LANES), 1)
    m_a = jnp.where(lane < RW_HEAD_DIM, 1.0, 0.0).astype(F32)
    m_b = 1.0 - m_a
    pr = lax.broadcasted_iota(jnp.int32, (2 * T, 2 * T), 0)
    pc = lax.broadcasted_iota(jnp.int32, (2 * T, 2 * T), 1)
    same_head = _block_id(pr, T) == _block_id(pc, T)
    p_lo, p_hi = jnp.where(rev, pr, pc), jnp.where(rev, pc, pr)
    incl = same_head & (p_lo <= p_hi)
    strict = same_head & (p_lo < p_hi)
    hr = _block_id(lax.broadcasted_iota(jnp.int32, (V7X_LANES, V7X_LANES), 0), RW_HEAD_DIM)
    hc = _block_id(lax.broadcasted_iota(jnp.int32, (V7X_LANES, V7X_LANES), 1), RW_HEAD_DIM)
    head_blk = hr == hc
    dn_nt = (((1,), (1,)), ((), ()))
    dn_tn = (((0,), (0,)), ((), ()))
    n_steps = int(np.log2(T))

    def stack(x):
        return jnp.concatenate([x * m_a, x * m_b], axis=0)

    def chunk_body(it, carry):
        c = jnp.where(rev, n_chunks - 1 - it, it)
        rows = pl.ds(pl.multiple_of(c * T, T), T)
        for p in range(n_pairs):
            lanes = slice(p * V7X_LANES, (p + 1) * V7X_LANES)
            st = st_ref[p]
            st_b = st.astype(BF16)
            v = v_ref[rows, lanes]
            r_ms = stack(rt_s[rows, lanes]).astype(BF16)
            a_ms = stack(at_s[rows, lanes]).astype(BF16)
            k_ms = stack(kt_s[rows, lanes]).astype(BF16)
            b_ms = stack(bt_s[rows, lanes]).astype(BF16)
            v_ms = stack(v).astype(BF16)
            mm = lax.dot_general(jnp.concatenate([r_ms, a_ms], axis=0), jnp.concatenate([k_ms, b_ms], axis=0),
                                 dn_nt, preferred_element_type=F32)
            m_rk = jnp.where(incl, mm[:2 * T, :2 * T], 0.0).astype(BF16)
            m_rb = jnp.where(incl, mm[:2 * T, 2 * T:], 0.0).astype(BF16)
            m_ak = jnp.where(strict, mm[2 * T:, :2 * T], 0.0).astype(BF16)
            nmat = jnp.where(strict, mm[2 * T:, 2 * T:], 0.0)
            x = jnp.dot(jnp.concatenate([a_ms, m_ak], axis=1), jnp.concatenate([st_b, v_ms], axis=0),
                        preferred_element_type=F32)
            for s in range(n_steps):
                nb = nmat.astype(BF16)
                if s < n_steps - 1:
                    nx = jnp.dot(nb, jnp.concatenate([nb, x.astype(BF16)], axis=1), preferred_element_type=F32)
                    nmat = nx[:, :2 * T]
                    x = x + nx[:, 2 * T:]
                else:
                    x = x + jnp.dot(nb, x.astype(BF16), preferred_element_type=F32)
            u_ms = x.astype(BF16)
            y_stk = jnp.dot(jnp.concatenate([r_ms, m_rk, m_rb], axis=1),
                            jnp.concatenate([st_b, v_ms, u_ms], axis=0), preferred_element_type=F32)
            y_s[rows, lanes] = y_stk[:T] + y_stk[T:]
            u = x[:T] + x[T:]
            kbw_t = jnp.concatenate([kw_s[rows, lanes], bw_s[rows, lanes]], axis=0).T
            upd = jnp.dot(kbw_t.astype(BF16), jnp.concatenate([v, u], axis=0).astype(BF16),
                          preferred_element_type=F32)
            w_row = wt_s[pl.ds(pl.multiple_of(c * T, T), 1), lanes]
            w_col = jnp.broadcast_to(w_row, (V7X_LANES, V7X_LANES)).T
            st_ref[p] = w_col * st + jnp.where(head_blk, upd, 0.0)
        return carry

    lax.fori_loop(0, n_chunks, chunk_body, 0)

    y = y_s[...]
    inv_n = 1.0 / RW_HEAD_DIM
    mu = _dot_exact_rhs01(y, head_ones) * inv_n
    yc = y - mu
    var = _dot_exact_rhs01(yc * yc, head_ones) * inv_n
    yn = yc * lax.rsqrt(var + RW_GN_EPS) * gw_ref[...] + gb_ref[...]
    bonus = _dot_exact_rhs01(r * kd * rk_ref[...], head_ones) * v_ref[...]
    o_ref[...] = ((yn + bonus) * g_ref[...]).astype(o_ref.dtype)
    del n_ctx_blocks


def _wkv(r, k, v, e2, a2, g2, k_k, k_a, r_k, gn_w, gn_b, *, B, C, L):
    M, D = r.shape
    Lt = C + L
    TB, LW = WKV_BLOCK, WKV_LANES
    assert C % TB == 0 and L % TB == 0 and D % LW == 0
    nb, ncb = Lt // TB, C // TB

    def tblk(d, b, i):
        bwd = jnp.where(i < ncb, ncb - 1 - i, nb - 1 - i + ncb)
        return b * nb + jnp.where(d == 0, i, bwd)

    shared = pl.BlockSpec((TB, LW), lambda d, b, g, i: (tblk(d, b, i), g))
    per_dir = pl.BlockSpec((None, TB, LW), lambda d, b, g, i: (d, tblk(d, b, i), g))
    vec = pl.BlockSpec((1, LW), lambda d, b, g, i: (0, g))
    body = functools.partial(_wkv_body, n_ctx_blocks=ncb)
    tbuf = pltpu.VMEM((TB, LW), F32)
    return pl.pallas_call(
        body, out_shape=jax.ShapeDtypeStruct((2, M, D), BF16), grid=(2, B, D // LW, nb),
        in_specs=[shared, shared, shared, per_dir, per_dir, per_dir, vec, vec, vec, vec, vec],
        out_specs=per_dir,
        scratch_shapes=[pltpu.VMEM((LW // V7X_LANES, V7X_LANES, V7X_LANES), F32)] + [tbuf] * 8,
        compiler_params=_cparams("arbitrary", "arbitrary", "arbitrary", "arbitrary"), name="wkv7",
    )(r, k, v, e2, a2, g2, k_k.reshape(1, D), k_a.reshape(1, D), r_k.reshape(1, D), gn_w.reshape(1, D),
      gn_b.reshape(1, D))


def _token_shift(h, *, B, C, L):
    M, D = h.shape
    half = D // 2
    t = h.reshape(B, C + L, D)

    def one(z):
        prev = jnp.pad(z[:, :-1, :half], ((0, 0), (1, 0), (0, 0)))
        nxt = jnp.pad(z[:, 1:, half:], ((0, 0), (0, 1), (0, 0)))
        return jnp.concatenate([prev, nxt], axis=-1)

    return jnp.concatenate([one(t[:, :C]), one(t[:, C:])], axis=1).reshape(M, D)


def _lerp(t, ts, mix):
    tf = t.astype(F32)
    return tf + (ts.astype(F32) - tf) * mix


def _rwkv7_time_mix(h, p, *, B, C, L):
    M, D = h.shape
    ts = _token_shift(h, B=B, C=C, L=L)
    mix = p["mix"]
    proj = lambda j, w, nm: _matmul([h, ts], w, prologue=_lerp, prologue_vecs=[mix[j].reshape(1, D)], name=nm)
    r = proj(0, p["w_r"].astype(BF16), "rw_r")
    k = proj(2, p["w_k"].astype(BF16), "rw_k")
    v = proj(3, p["w_v"].astype(BF16), "rw_v")
    lw = p["w1"].shape[2]
    la = p["a1"].shape[2]
    lg = p["g1"].shape[2]
    lgp = -(-lg // V7X_LANES) * V7X_LANES
    w1c = jnp.concatenate([p["w1"][0], p["w1"][1]], axis=1).astype(BF16)
    a1c = jnp.concatenate([p["a1"][0], p["a1"][1]], axis=1).astype(BF16)
    g1p = jnp.pad(p["g1"], ((0, 0), (0, 0), (0, lgp - lg)))
    g1c = jnp.concatenate([g1p[0], g1p[1]], axis=1).astype(BF16)
    g2p = jnp.pad(p["g2"], ((0, 0), (0, lgp - lg), (0, 0))).astype(BF16)
    pw = proj(1, w1c, "rw_w1")
    pa = proj(4, a1c, "rw_a1")
    pg = proj(5, g1c, "rw_g1")

    def epi_decay(acc, w0, out_ref):
        out_ref[...] = jax.nn.sigmoid(acc + w0) * float(np.exp(-0.5))

    def epi_sigmoid(acc, a0, out_ref):
        out_ref[...] = jax.nn.sigmoid(acc + a0)

    es, as_, gs = [], [], []
    for d in range(2):
        es.append(_matmul([pw], p["w2"][d].astype(BF16), prologue=jnp.tanh, epilogue=epi_decay,
                          epi_vecs=[p["w0"][d].reshape(1, D)], a_col=d, name="rw_w2") if lw else None)
        as_.append(_matmul([pa], p["a2"][d].astype(BF16), prologue=lambda z: z, epilogue=epi_sigmoid,
                           epi_vecs=[p["a0"][d].reshape(1, D)], a_col=d, name="rw_a2"))
        gs.append(_matmul([pg], g2p[d], prologue=jax.nn.sigmoid, a_col=d, name="rw_g2"))
    o2 = _wkv(r, k, v, jnp.stack(es), jnp.stack(as_), jnp.stack(gs), p["k_k"], p["k_a"], p["r_k"],
              p["gn_w"], p["gn_b"], B=B, C=C, L=L)
    return _matmul([o2[0], o2[1]], p["w_o"].astype(BF16),
                   prologue=lambda a, b: a.astype(F32) + b.astype(F32), name="rw_o")


def _router_params(router_w, router_b):
    D, E = router_w.shape
    rw = jnp.pad(router_w, ((0, 0), (0, V7X_LANES - E))).astype(BF16)
    rb = jnp.pad(router_b, (0, V7X_LANES - E)).reshape(1, V7X_LANES)
    return rw, rb


def kernel(x, c, ctx, c_ctx, l0_ada_w, l0_ada_b, l0_norm_mix, l0_norm_ffn, l0_na_w_qkv, l0_na_q_gain, l0_na_k_gain, l0_na_rpb, l0_na_w_o, l0_router_w, l0_router_b, l0_moe_w_gate, l0_moe_b_gate, l0_moe_w_up, l0_moe_b_up, l0_moe_w_down, l0_moe_b_down, l1_ada_w, l1_ada_b, l1_norm_mix, l1_norm_ffn, l1_rw_mix, l1_rw_w_r, l1_rw_w_k, l1_rw_w_v, l1_rw_k_k, l1_rw_k_a, l1_rw_r_k, l1_rw_w0, l1_rw_w1, l1_rw_w2, l1_rw_a0, l1_rw_a1, l1_rw_a2, l1_rw_g1, l1_rw_g2, l1_rw_gn_w, l1_rw_gn_b, l1_rw_w_o, l1_router_w, l1_router_b, l1_moe_w_gate, l1_moe_b_gate, l1_moe_w_up, l1_moe_b_up, l1_moe_w_down, l1_moe_b_down):
    B, L, D = x.shape
    C = ctx.shape[1]
    Lt = C + L
    M = B * Lt
    H = D // NA_HEAD_DIM
    E = l0_router_w.shape[1]
    assert C % ROW_TILE == 0 and L % ROW_TILE == 0 and L % GRID_W == 0
    tiles = dict(tiles_per_batch=Lt // ROW_TILE, ctx_tiles=C // ROW_TILE)
    xa = jnp.concatenate([ctx, x], axis=1).reshape(M, D)

    mod0 = _ada_mod(c, c_ctx, l0_ada_w, l0_ada_b)
    (h,) = _rows(xa, [], [], mod0, norm_w=l0_norm_mix, shift_row=0, scale_row=1, write_x=False,
                 name="l0_norm_mix", **tiles)
    scale = NA_HEAD_DIM ** -0.5
    gain = jnp.concatenate([jnp.tile(l0_na_q_gain * scale, H), jnp.tile(l0_na_k_gain, H), jnp.ones((D,), F32)])
    tn = 512 if D % 512 == 0 else D
    qkv = _matmul([h], l0_na_w_qkv.astype(BF16), out_dtypes=(BF16,),
                  epilogue=functools.partial(_qkv_epilogue, n_norm_tiles=2 * D // tn),
                  epi_vecs=[gain.reshape(1, 3 * D)], tn=tn, name="na_qkv")
    bias_tabs = _na_bias_tables(l0_na_rpb, L // GRID_W)
    att = _neighbourhood_attention(qkv, bias_tabs, B=B, C=C, L=L, D=D)
    y = _matmul([att], l0_na_w_o.astype(BF16), name="na_o")
    rw0, rb0 = _router_params(l0_router_w, l0_router_b)
    xa, h2, logits = _rows(xa, [y], [2], mod0, norm_w=l0_norm_ffn, shift_row=3, scale_row=4,
                           router=(rw0, rb0), name="l0_norm_ffn", **tiles)
    f = _moe_ffn(h2, logits[:, :E], l0_moe_w_gate.astype(BF16), l0_moe_b_gate, l0_moe_w_up.astype(BF16),
                 l0_moe_b_up, l0_moe_w_down.astype(BF16), l0_moe_b_down)

    mod1 = _ada_mod(c, c_ctx, l1_ada_w, l1_ada_b)
    (xa,) = _rows(xa, [f], [5], mod0, name="l0_moe_residual", **tiles)
    (h,) = _rows(xa, [], [], mod1, norm_w=l1_norm_mix, shift_row=0, scale_row=1, write_x=False,
                 name="l1_norm_mix", **tiles)
    rwp = dict(mix=l1_rw_mix, w_r=l1_rw_w_r, w_k=l1_rw_w_k, w_v=l1_rw_w_v, k_k=l1_rw_k_k, k_a=l1_rw_k_a,
               r_k=l1_rw_r_k, w0=l1_rw_w0, w1=l1_rw_w1, w2=l1_rw_w2, a0=l1_rw_a0, a1=l1_rw_a1, a2=l1_rw_a2,
               g1=l1_rw_g1, g2=l1_rw_g2, gn_w=l1_rw_gn_w, gn_b=l1_rw_gn_b, w_o=l1_rw_w_o)
    y = _rwkv7_time_mix(h, rwp, B=B, C=C, L=L)
    rw1, rb1 = _router_params(l1_router_w, l1_router_b)
    xa, h2, logits = _rows(xa, [y], [2], mod1, norm_w=l1_norm_ffn, shift_row=3, scale_row=4,
                           router=(rw1, rb1), name="l1_norm_ffn", **tiles)
    lat = lambda z: z.reshape(B, Lt, -1)[:, C:].reshape(B * L, -1)
    f = _moe_ffn(lat(h2), lat(logits)[:, :E], l1_moe_w_gate.astype(BF16), l1_moe_b_gate,
                 l1_moe_w_up.astype(BF16), l1_moe_b_up, l1_moe_w_down.astype(BF16), l1_moe_b_down)
    x_lat = lat(xa)
    mod1_lat = mod1[1::2]
    (out,) = _rows(x_lat, [f], [5], mod1_lat_expand(mod1_lat), name="l1_moe_residual",
                   tiles_per_batch=L // ROW_TILE, ctx_tiles=0)
    return out.reshape(B, L, D)


def mod1_lat_expand(mod_lat):
    return jnp.repeat(mod_lat, 2, axis=0)
```

```python
import functools

import numpy as np
import jax
import jax.numpy as jnp
from jax import lax
from jax.experimental import pallas as pl
from jax.experimental.pallas import tpu as pltpu

F32 = jnp.float32
BF16 = jnp.bfloat16

GRID_W = 64
N_MOD = 6
RMS_EPS = 1e-6
NEG_INF = -1e30
NA_HEAD_DIM = 128
NA_WIN_H = 8
NA_WIN_W = 16
RW_HEAD_DIM = 64
RW_GN_EPS = 64e-5
TOP_K = 4
SWIGLU_LIMIT = 7.0
SWIGLU_ALPHA = 1.702

V7X_LANES = 128
VMEM_LIMIT_BYTES = 56 * 1024 * 1024

ROW_TILE = 256
NA_QROWS = 4
NA_KROWS = 12
MOE_TILE = 256
WKV_T = 64
WKV_BLOCK = 256
WKV_LANES = 1024
WKV_SEG = 256


def _cparams(*sem):
    return pltpu.CompilerParams(dimension_semantics=sem, vmem_limit_bytes=VMEM_LIMIT_BYTES)


def _mm_body(*refs, n_a, n_pv, n_ev, n_em, n_out, prologue, epilogue, chunk):
    a_refs = refs[:n_a]
    pv_refs = refs[n_a:n_a + n_pv]
    w_ref = refs[n_a + n_pv]
    base = n_a + n_pv + 1
    ev_refs = refs[base:base + n_ev]
    em_refs = refs[base + n_ev:base + n_ev + n_em]
    out_refs = refs[base + n_ev + n_em:base + n_ev + n_em + n_out]
    scratch = refs[base + n_ev + n_em + n_out:]
    if prologue is not None:
        a_scr = scratch[0]

        @pl.when(pl.program_id(1) == 0)
        def _():
            pvs = [r[...] for r in pv_refs]

            def body(c, carry):
                rows = pl.ds(pl.multiple_of(c * chunk, chunk), chunk)
                a_scr[rows, :] = prologue(*[r[rows, :] for r in a_refs], *pvs).astype(BF16)
                return carry

            lax.fori_loop(0, a_scr.shape[0] // chunk, body, 0)

        a = a_scr[...]
    else:
        a = a_refs[0][...]
    acc = jnp.dot(a, w_ref[...].astype(BF16), preferred_element_type=F32)
    evs = [r[...] for r in ev_refs]
    ems = [r[...] for r in em_refs]
    if epilogue is None:
        out_refs[0][...] = acc.astype(out_refs[0].dtype)
    else:
        epilogue(acc, *evs, *ems, *out_refs)


def _matmul(a_list, w, *, out_dtypes=(F32,), prologue=None, prologue_vecs=(), epilogue=None,
            epi_vecs=(), epi_mats=(), a_col=0, tm=None, tn=None, name="mm"):
    M = a_list[0].shape[0]
    K, N = w.shape
    if tm is None:
        tm = 512 if M % 512 == 0 else (256 if M % 256 == 0 else M)
    if tn is None:
        tn = 512 if N % 512 == 0 else (256 if N % 256 == 0 else N)
    assert M % tm == 0 and N % tn == 0
    chunk = 32 if tm % 32 == 0 else tm
    in_specs = [pl.BlockSpec((tm, K), lambda m, n: (m, a_col)) for _ in a_list]
    in_specs += [pl.BlockSpec((1, K), lambda m, n: (0, 0)) for _ in prologue_vecs]
    in_specs += [pl.BlockSpec((K, tn), lambda m, n: (0, n))]
    in_specs += [pl.BlockSpec((1, tn), lambda m, n: (0, n)) for _ in epi_vecs]
    in_specs += [pl.BlockSpec((tm, tn), lambda m, n: (m, n)) for _ in epi_mats]
    out_shape = [jax.ShapeDtypeStruct((M, N), dt) for dt in out_dtypes]
    out_specs = [pl.BlockSpec((tm, tn), lambda m, n: (m, n)) for _ in out_dtypes]
    scratch = [pltpu.VMEM((tm, K), BF16)] if prologue is not None else []
    body = functools.partial(_mm_body, n_a=len(a_list), n_pv=len(prologue_vecs), n_ev=len(epi_vecs),
                             n_em=len(epi_mats), n_out=len(out_dtypes), prologue=prologue, epilogue=epilogue,
                             chunk=chunk)
    outs = pl.pallas_call(
        body, out_shape=out_shape, grid=(M // tm, N // tn), in_specs=in_specs, out_specs=out_specs,
        scratch_shapes=scratch, compiler_params=_cparams("parallel", "arbitrary"), name=name,
    )(*a_list, *prologue_vecs, w, *epi_vecs, *epi_mats)
    return outs[0] if len(out_dtypes) == 1 else outs


def _rows_body(*refs, n_add, gate_rows, do_norm, shift_row, scale_row, has_router, write_x, chunk):
    x_ref = refs[0]
    add_refs = refs[1:1 + n_add]
    mod_ref = refs[1 + n_add]
    pos = 2 + n_add
    if do_norm:
        nw_ref = refs[pos]
        pos += 1
    if has_router:
        rw_ref, rb_ref = refs[pos], refs[pos + 1]
        pos += 2
    outs = refs[pos:]
    oi = 0
    if write_x:
        xo_ref = outs[oi]
        oi += 1
    if do_norm:
        h_ref = outs[oi]
        oi += 1
    if has_router:
        lg_ref = outs[oi]
    gates = [mod_ref[0, g:g + 1, :] for g in gate_rows]
    if do_norm:
        nw = nw_ref[...]
        sh = mod_ref[0, shift_row:shift_row + 1, :]
        sc = 1.0 + mod_ref[0, scale_row:scale_row + 1, :]

    def body(c, carry):
        rows = pl.ds(pl.multiple_of(c * chunk, chunk), chunk)
        x = x_ref[rows, :]
        for a_ref, g in zip(add_refs, gates):
            x = x + g * a_ref[rows, :].astype(F32)
        if write_x:
            xo_ref[rows, :] = x
        if do_norm:
            y = x * lax.rsqrt(jnp.mean(x * x, axis=-1, keepdims=True) + RMS_EPS)
            h_ref[rows, :] = ((y * nw) * sc + sh).astype(h_ref.dtype)
        return carry

    lax.fori_loop(0, x_ref.shape[0] // chunk, body, 0)
    if has_router:
        lg_ref[...] = jnp.dot(h_ref[...], rw_ref[...], preferred_element_type=F32) + rb_ref[...]


def _rows(x, adds, gate_rows, mod, *, tiles_per_batch, ctx_tiles, norm_w=None, shift_row=0, scale_row=1,
          router=None, write_x=True, name="rows"):
    M, D = x.shape
    tr = ROW_TILE
    do_norm = norm_w is not None

    def mod_idx(i):
        return ((i // tiles_per_batch) * 2 + ((i % tiles_per_batch) >= ctx_tiles).astype(jnp.int32), 0, 0)

    row_spec = pl.BlockSpec((tr, D), lambda i: (i, 0))
    args = [x, *adds, mod]
    in_specs = [row_spec] * (1 + len(adds)) + [pl.BlockSpec((1, 8, D), mod_idx)]
    if do_norm:
        args.append(norm_w.reshape(1, D))
        in_specs.append(pl.BlockSpec((1, D), lambda i: (0, 0)))
    if router is not None:
        rw, rb = router
        args += [rw, rb]
        in_specs += [pl.BlockSpec(rw.shape, lambda i: (0, 0)), pl.BlockSpec(rb.shape, lambda i: (0, 0))]
    out_shape, out_specs = [], []
    if write_x:
        out_shape.append(jax.ShapeDtypeStruct((M, D), F32))
        out_specs.append(row_spec)
    if do_norm:
        out_shape.append(jax.ShapeDtypeStruct((M, D), BF16))
        out_specs.append(row_spec)
    if router is not None:
        out_shape.append(jax.ShapeDtypeStruct((M, V7X_LANES), F32))
        out_specs.append(pl.BlockSpec((tr, V7X_LANES), lambda i: (i, 0)))
    body = functools.partial(_rows_body, n_add=len(adds), gate_rows=tuple(gate_rows), do_norm=do_norm,
                             shift_row=shift_row, scale_row=scale_row, has_router=router is not None,
                             write_x=write_x, chunk=32)
    return pl.pallas_call(body, out_shape=out_shape, grid=(M // tr,), in_specs=in_specs, out_specs=out_specs,
                          compiler_params=_cparams("parallel"), name=name)(*args)


def _ada_mod(c, c_ctx, ada_w, ada_b):
    B, D = c.shape
    cond = jnp.concatenate([c, c_ctx[None, :], jnp.zeros((8 - B - 1, D), F32)], axis=0)

    def epi(acc, bias, out_ref):
        out_ref[...] = acc + bias

    out = _matmul([cond], ada_w, prologue=lambda a: a * jax.nn.sigmoid(a), epilogue=epi,
                  epi_vecs=[ada_b.reshape(1, -1)], tm=8, tn=512, name="ada_mod")
    lat = out[:B].reshape(B, N_MOD, D)
    ctx = jnp.broadcast_to(out[B].reshape(1, N_MOD, D), (B, N_MOD, D))
    mod = jnp.stack([ctx, lat], axis=1).reshape(2 * B, N_MOD, D)
    return jnp.pad(mod, ((0, 0), (0, 8 - N_MOD), (0, 0)))


def _qkv_epilogue(acc, gain, out_ref, *, n_norm_tiles):
    tn = acc.shape[1]

    @pl.when(pl.program_id(1) < n_norm_tiles)
    def _():
        for i in range(tn // NA_HEAD_DIM):
            sl = slice(i * NA_HEAD_DIM, (i + 1) * NA_HEAD_DIM)
            z = acc[:, sl]
            z = z * lax.rsqrt(jnp.mean(z * z, axis=-1, keepdims=True) + RMS_EPS)
            out_ref[:, sl] = (z * gain[:, sl]).astype(out_ref.dtype)

    @pl.when(pl.program_id(1) >= n_norm_tiles)
    def _():
        out_ref[...] = acc.astype(out_ref.dtype)


def _na_bias_tables(rpb, rows):
    nblk = rows // NA_QROWS
    kh = min(NA_WIN_H, rows)
    assert kh == NA_WIN_H and rows >= NA_KROWS + NA_QROWS and rows % NA_QROWS == 0
    col = np.arange(GRID_W)
    col_start = np.clip(col - NA_WIN_W // 2, 0, GRID_W - NA_WIN_W)
    col_ok = (col[None, :] >= col_start[:, None]) & (col[None, :] < col_start[:, None] + NA_WIN_W)
    dc = np.clip(col[None, :] - col[:, None] + NA_WIN_W - 1, 0, 2 * NA_WIN_W - 2)
    tabs = []
    for j in (0, 1, nblk - 1):
        s0 = int(np.clip(NA_QROWS * j - NA_WIN_H // 2, 0, rows - NA_KROWS))
        qr = NA_QROWS * j + np.arange(NA_QROWS)
        rs = np.clip(qr - kh // 2, 0, rows - kh)
        kr = s0 + np.arange(NA_KROWS)
        row_ok = (kr[None, :] >= rs[:, None]) & (kr[None, :] < rs[:, None] + kh)
        dr = np.clip(kr[None, :] - qr[:, None] + NA_WIN_H - 1, 0, 2 * NA_WIN_H - 2)
        ok = row_ok[:, None, :, None] & col_ok[None, :, None, :]
        shape = (NA_QROWS, GRID_W, NA_KROWS, GRID_W)
        dr_f = np.broadcast_to(dr[:, None, :, None], shape).reshape(NA_QROWS * GRID_W, NA_KROWS * GRID_W)
        dc_f = np.broadcast_to(dc[None, :, None, :], shape).reshape(NA_QROWS * GRID_W, NA_KROWS * GRID_W)
        ok_f = ok.reshape(NA_QROWS * GRID_W, NA_KROWS * GRID_W)
        tabs.append(jnp.where(ok_f[None], rpb.astype(F32)[:, dr_f, dc_f], NEG_INF))
    return jnp.stack(tabs, axis=0)


def _na_body(q_ref, k_ref, v_ref, bias_ref, o_ref, *, C, rows):
    nq = NA_QROWS * GRID_W
    nk = NA_KROWS * GRID_W
    nblk = rows // NA_QROWS
    dn = (((1,), (1,)), ((), ()))
    kc = k_ref[0:C, :]
    vc = v_ref[0:C, :]
    s = lax.dot_general(q_ref[0:C, :], kc, dn, preferred_element_type=F32)
    p = jnp.exp(s - jnp.max(s, axis=-1, keepdims=True))
    o = jnp.dot(p.astype(BF16), vc, preferred_element_type=F32) / jnp.sum(p, axis=-1, keepdims=True)
    o_ref[0:C, :] = o.astype(o_ref.dtype)

    def body(j, carry):
        s0 = jnp.clip(NA_QROWS * j - NA_WIN_H // 2, 0, rows - NA_KROWS)
        cls = jnp.where(j == 0, 0, jnp.where(j == nblk - 1, 2, 1))
        qrows = pl.ds(pl.multiple_of(C + j * nq, GRID_W), nq)
        krows = pl.ds(pl.multiple_of(C + s0 * GRID_W, GRID_W), nk)
        qb = q_ref[qrows, :]
        kb = k_ref[krows, :]
        vb = v_ref[krows, :]
        s_lat = lax.dot_general(qb, kb, dn, preferred_element_type=F32) + bias_ref[cls, 0]
        s_ctx = lax.dot_general(qb, kc, dn, preferred_element_type=F32)
        m = jnp.maximum(jnp.max(s_lat, axis=-1, keepdims=True), jnp.max(s_ctx, axis=-1, keepdims=True))
        p_lat = jnp.exp(s_lat - m)
        p_ctx = jnp.exp(s_ctx - m)
        den = jnp.sum(p_lat, axis=-1, keepdims=True) + jnp.sum(p_ctx, axis=-1, keepdims=True)
        o = (jnp.dot(p_ctx.astype(BF16), vc, preferred_element_type=F32)
             + jnp.dot(p_lat.astype(BF16), vb, preferred_element_type=F32)) / den
        o_ref[qrows, :] = o.astype(o_ref.dtype)
        return carry

    lax.fori_loop(0, nblk, body, 0)


def _neighbourhood_attention(qkv, bias_tabs, *, B, C, L, D):
    H = D // NA_HEAD_DIM
    Lt = C + L
    rows = L // GRID_W
    blk = (Lt, NA_HEAD_DIM)
    body = functools.partial(_na_body, C=C, rows=rows)
    return pl.pallas_call(
        body, out_shape=jax.ShapeDtypeStruct((B * Lt, D), BF16), grid=(B, H),
        in_specs=[pl.BlockSpec(blk, lambda b, h: (b, h)),
                  pl.BlockSpec(blk, lambda b, h: (b, H + h)),
                  pl.BlockSpec(blk, lambda b, h: (b, 2 * H + h)),
                  pl.BlockSpec((3, 1) + bias_tabs.shape[2:], lambda b, h: (0, h, 0, 0))],
        out_specs=pl.BlockSpec(blk, lambda b, h: (b, h)),
        compiler_params=_cparams("parallel", "parallel"), name="na_attention",
    )(qkv, qkv, qkv, bias_tabs)


def _moe_body(be_ref, x_ref, wg_ref, bg_ref, wu_ref, bu_ref, wd_ref, bd_ref, o_ref):
    del be_ref
    x = x_ref[...]
    g = jnp.minimum(jnp.dot(x, wg_ref[0], preferred_element_type=F32) + bg_ref[0], SWIGLU_LIMIT)
    u = jnp.clip(jnp.dot(x, wu_ref[0], preferred_element_type=F32) + bu_ref[0], -SWIGLU_LIMIT, SWIGLU_LIMIT)
    hdn = g * jax.nn.sigmoid(SWIGLU_ALPHA * g) * (u + 1.0)
    y = jnp.dot(hdn.astype(BF16), wd_ref[0], preferred_element_type=F32) + bd_ref[0]
    o_ref[...] = y.astype(o_ref.dtype)


def _moe_experts(xs, blk_exp, wg, bg, wu, bu, wd, bd):
    n_rows, D = xs.shape
    E, _, De = wg.shape
    tm = MOE_TILE
    grid_spec = pltpu.PrefetchScalarGridSpec(
        num_scalar_prefetch=1, grid=(n_rows // tm,),
        in_specs=[pl.BlockSpec((tm, D), lambda i, be: (i, 0)),
                  pl.BlockSpec((1, D, De), lambda i, be: (be[i], 0, 0)),
                  pl.BlockSpec((1, 1, De), lambda i, be: (be[i], 0, 0)),
                  pl.BlockSpec((1, D, De), lambda i, be: (be[i], 0, 0)),
                  pl.BlockSpec((1, 1, De), lambda i, be: (be[i], 0, 0)),
                  pl.BlockSpec((1, De, D), lambda i, be: (be[i], 0, 0)),
                  pl.BlockSpec((1, 1, D), lambda i, be: (be[i], 0, 0))],
        out_specs=pl.BlockSpec((tm, D), lambda i, be: (i, 0)))
    return pl.pallas_call(
        _moe_body, out_shape=jax.ShapeDtypeStruct((n_rows, D), F32), grid_spec=grid_spec,
        compiler_params=_cparams("arbitrary"), name="moe_experts",
    )(blk_exp, xs, wg, bg.reshape(E, 1, De), wu, bu.reshape(E, 1, De), wd, bd.reshape(E, 1, D))


def _moe_ffn(h, logits, wg, bg, wu, bu, wd, bd):
    N, D = h.shape
    E = wg.shape[0]
    tm = MOE_TILE
    top_val, top_idx = lax.top_k(logits, TOP_K)
    top_w = jax.nn.softmax(top_val, axis=-1)
    flat_e = top_idx.reshape(-1)
    onehot = (flat_e[:, None] == jnp.arange(E, dtype=flat_e.dtype)[None, :]).astype(jnp.int32)
    csum = jnp.cumsum(onehot, axis=0)
    counts = csum[-1]
    rank = jnp.take_along_axis(csum, flat_e[:, None], axis=1)[:, 0] - 1
    padded = (counts + tm - 1) // tm * tm
    pad_end = jnp.cumsum(padded)
    pad_start = pad_end - padded
    dest = pad_start[flat_e] + rank
    n_blocks = (N * TOP_K + tm - 1) // tm + E
    n_rows = n_blocks * tm
    flat_tok = jnp.repeat(jnp.arange(N, dtype=jnp.int32), TOP_K)
    row_tok = jnp.full((n_rows,), N, jnp.int32).at[dest].set(flat_tok)
    blk_start = jnp.arange(n_blocks, dtype=jnp.int32) * tm
    blk_exp = jnp.minimum(jnp.sum(blk_start[:, None] >= pad_end[None, :], axis=1), E - 1).astype(jnp.int32)
    h_pad = jnp.concatenate([h, jnp.zeros((1, D), h.dtype)], axis=0)
    xs = h_pad[row_tok]
    ys = _moe_experts(xs, blk_exp, wg, bg, wu, bu, wd, bd)
    picked = ys[dest].reshape(N, TOP_K, D)
    return jnp.sum(picked * top_w[:, :, None], axis=1)


def _block_id(idx, size):
    return lax.shift_right_logical(idx, int(np.log2(size)))


def _split2(x):
    hi = x.astype(BF16)
    lo = (x - hi.astype(F32)).astype(BF16)
    return hi, lo


def _dot01_lhs(m01, x):
    hi, lo = _split2(x)
    return jnp.dot(m01, hi, preferred_element_type=F32) + jnp.dot(m01, lo, preferred_element_type=F32)


def _dot01_rhs(x, m01):
    hi, lo = _split2(x)
    return jnp.dot(hi, m01, preferred_element_type=F32) + jnp.dot(lo, m01, preferred_element_type=F32)


def _wkv_body(r_ref, k_ref, v_ref, e_ref, a_ref, g_ref, kk_ref, ka_ref, rk_ref, gw_ref, gb_ref, o_ref,
              st_ref, rt_s, kt_s, at_s, bt_s, kw_s, bw_s, wt_s, bn_s, y_s, *, rev):
    T = WKV_T
    TB, LW = r_ref.shape
    n_pairs = LW // V7X_LANES
    n_chunks = TB // T

    @pl.when(pl.program_id(2) == 0)
    def _():
        st_ref[...] = jnp.zeros_like(st_ref)

    def earlier(p, q, strict):
        lo, hi = (p, q) if rev else (q, p)
        return lo < hi if strict else lo <= hi

    seg = min(LW, WKV_SEG)
    ri = lax.broadcasted_iota(jnp.int32, (TB, TB), 0)
    ci = lax.broadcasted_iota(jnp.int32, (TB, TB), 1)
    same_chunk = _block_id(ri, T) == _block_id(ci, T)
    tri = jnp.where(same_chunk & earlier(ri, ci, False), 1.0, 0.0).astype(BF16)
    ones_c = jnp.where(same_chunk, 1.0, 0.0).astype(BF16)
    li = _block_id(lax.broadcasted_iota(jnp.int32, (seg, seg), 0), RW_HEAD_DIM)
    lj = _block_id(lax.broadcasted_iota(jnp.int32, (seg, seg), 1), RW_HEAD_DIM)
    head_ones = jnp.where(li == lj, 1.0, 0.0).astype(BF16)

    for sg in range(LW // seg):
        ls = slice(sg * seg, (sg + 1) * seg)
        r = r_ref[:, ls]
        k = k_ref[:, ls]
        e = e_ref[:, ls]
        a = a_ref[:, ls]
        kkv = k * kk_ref[:, ls]
        kk = kkv * lax.rsqrt(_dot01_rhs(kkv * kkv, head_ones) + 1e-12)
        kd = k * (1.0 + (a - 1.0) * ka_ref[:, ls])
        be = kk * a
        cum = _dot01_lhs(tri, e)
        tot = _dot01_lhs(ones_c, e)
        ecum = jnp.exp(cum)
        edec = jnp.exp(cum - tot)
        rt_s[:, ls] = r * jnp.exp(-cum)
        kt_s[:, ls] = kd * ecum
        at_s[:, ls] = -kk * jnp.exp(e - cum)
        bt_s[:, ls] = be * ecum
        kw_s[:, ls] = kd * edec
        bw_s[:, ls] = be * edec
        wt_s[:, ls] = jnp.exp(-tot)
        bn_s[:, ls] = _dot01_rhs(r * kd * rk_ref[:, ls], head_ones) * v_ref[:, ls]

    lane = lax.broadcasted_iota(jnp.int32, (1, V7X_LANES), 1)
    m_a = jnp.where(lane < RW_HEAD_DIM, 1.0, 0.0).astype(F32)
    m_b = 1.0 - m_a
    pr = lax.broadcasted_iota(jnp.int32, (2 * T, 2 * T), 0)
    pc = lax.broadcasted_iota(jnp.int32, (2 * T, 2 * T), 1)
    same_head = _block_id(pr, T) == _block_id(pc, T)
    incl = same_head & earlier(pr, pc, False)
    strict = same_head & earlier(pr, pc, True)
    hr = _block_id(lax.broadcasted_iota(jnp.int32, (V7X_LANES, V7X_LANES), 0), RW_HEAD_DIM)
    hc = _block_id(lax.broadcasted_iota(jnp.int32, (V7X_LANES, V7X_LANES), 1), RW_HEAD_DIM)
    head_blk = hr == hc
    dn_nt = (((1,), (1,)), ((), ()))
    dn_tn = (((0,), (0,)), ((), ()))
    n_steps = int(np.log2(T))

    def stack(x):
        return jnp.concatenate([x * m_a, x * m_b], axis=0)

    def chunk_body(it, carry):
        c = (n_chunks - 1 - it) if rev else it
        rows = pl.ds(pl.multiple_of(c * T, T), T)
        row0 = pl.ds(pl.multiple_of(c * T, T), 1)
        lanes = [slice(p * V7X_LANES, (p + 1) * V7X_LANES) for p in range(n_pairs)]
        pairs = range(n_pairs)
        st = [st_ref[p] for p in pairs]
        st_b = [z.astype(BF16) for z in st]
        v = [v_ref[rows, ln] for ln in lanes]
        r_ms = [stack(rt_s[rows, ln]).astype(BF16) for ln in lanes]
        a_ms = [stack(at_s[rows, ln]).astype(BF16) for ln in lanes]
        k_ms = [stack(kt_s[rows, ln]).astype(BF16) for ln in lanes]
        b_ms = [stack(bt_s[rows, ln]).astype(BF16) for ln in lanes]
        v_ms = [stack(z).astype(BF16) for z in v]
        mm = [lax.dot_general(jnp.concatenate([r_ms[p], a_ms[p]], axis=0),
                              jnp.concatenate([k_ms[p], b_ms[p]], axis=0), dn_nt, preferred_element_type=F32)
              for p in pairs]
        m_rk = [jnp.where(incl, z[:2 * T, :2 * T], 0.0).astype(BF16) for z in mm]
        m_rb = [jnp.where(incl, z[:2 * T, 2 * T:], 0.0).astype(BF16) for z in mm]
        m_ak = [jnp.where(strict, z[2 * T:, :2 * T], 0.0).astype(BF16) for z in mm]
        nmat = [jnp.where(strict, z[2 * T:, 2 * T:], 0.0) for z in mm]
        x = [jnp.dot(jnp.concatenate([a_ms[p], m_ak[p]], axis=1), jnp.concatenate([st_b[p], v_ms[p]], axis=0),
                     preferred_element_type=F32) for p in pairs]
        for s in range(n_steps):
            nb = [z.astype(BF16) for z in nmat]
            if s < n_steps - 1:
                nx = [jnp.dot(nb[p], jnp.concatenate([nb[p], x[p].astype(BF16)], axis=1),
                              preferred_element_type=F32) for p in pairs]
                nmat = [z[:, :2 * T] for z in nx]
                x = [x[p] + nx[p][:, 2 * T:] for p in pairs]
            else:
                x = [x[p] + jnp.dot(nb[p], x[p].astype(BF16), preferred_element_type=F32) for p in pairs]
        y_stk = [jnp.dot(jnp.concatenate([r_ms[p], m_rk[p], m_rb[p]], axis=1),
                         jnp.concatenate([st_b[p], v_ms[p], x[p].astype(BF16)], axis=0),
                         preferred_element_type=F32) for p in pairs]
        upd = []
        for p, ln in enumerate(lanes):
            u = x[p][:T] + x[p][T:]
            kbw_t = jnp.concatenate([kw_s[rows, ln], bw_s[rows, ln]], axis=0).T
            upd.append(jnp.dot(kbw_t.astype(BF16), jnp.concatenate([v[p], u], axis=0).astype(BF16),
                               preferred_element_type=F32))
        for p, ln in enumerate(lanes):
            y_s[rows, ln] = y_stk[p][:T] + y_stk[p][T:]
            w_col = jnp.broadcast_to(wt_s[row0, ln], (V7X_LANES, V7X_LANES)).T
            st_ref[p] = w_col * st[p] + jnp.where(head_blk, upd[p], 0.0)
        return carry

    lax.fori_loop(0, n_chunks, chunk_body, 0)

    inv_n = 1.0 / RW_HEAD_DIM
    for sg in range(LW // seg):
        ls = slice(sg * seg, (sg + 1) * seg)
        y = y_s[:, ls]
        yc = y - _dot01_rhs(y, head_ones) * inv_n
        var = _dot01_rhs(yc * yc, head_ones) * inv_n
        yn = yc * lax.rsqrt(var + RW_GN_EPS) * gw_ref[:, ls] + gb_ref[:, ls]
        o_ref[:, ls] = ((yn + bn_s[:, ls]) * g_ref[:, ls]).astype(o_ref.dtype)


def _wkv(r, k, v, e, a, g, k_k, k_a, r_k, gn_w, gn_b, *, B, C, L, rev):
    M, D = r.shape
    Lt = C + L
    TB, LW = WKV_BLOCK, min(WKV_LANES, D)
    assert C % TB == 0 and L % TB == 0 and D % LW == 0
    nb, ncb = Lt // TB, C // TB

    def tblk(b, i):
        if not rev:
            return b * nb + i
        return b * nb + jnp.where(i < ncb, ncb - 1 - i, nb - 1 - i + ncb)

    blk = pl.BlockSpec((TB, LW), lambda b, gi, i: (tblk(b, i), gi))
    vec = pl.BlockSpec((1, LW), lambda b, gi, i: (0, gi))
    tbuf = pltpu.VMEM((TB, LW), F32)
    return pl.pallas_call(
        functools.partial(_wkv_body, rev=rev), out_shape=jax.ShapeDtypeStruct((M, D), BF16),
        grid=(B, D // LW, nb), in_specs=[blk] * 6 + [vec] * 5, out_specs=blk,
        scratch_shapes=[pltpu.VMEM((LW // V7X_LANES, V7X_LANES, V7X_LANES), F32)] + [tbuf] * 9,
        compiler_params=_cparams("parallel", "parallel", "arbitrary"), name="wkv7_bwd" if rev else "wkv7_fwd",
    )(r, k, v, e, a, g, k_k.reshape(1, D), k_a.reshape(1, D), r_k.reshape(1, D), gn_w.reshape(1, D),
      gn_b.reshape(1, D))


def _token_shift(h, *, B, C, L):
    M, D = h.shape
    half = D // 2
    t = h.reshape(B, C + L, D)

    def one(z):
        prev = jnp.pad(z[:, :-1, :half], ((0, 0), (1, 0), (0, 0)))
        nxt = jnp.pad(z[:, 1:, half:], ((0, 0), (0, 1), (0, 0)))
        return jnp.concatenate([prev, nxt], axis=-1)

    return jnp.concatenate([one(t[:, :C]), one(t[:, C:])], axis=1).reshape(M, D)


def _lerp(t, ts, mix):
    tf = t.astype(F32)
    return tf + (ts.astype(F32) - tf) * mix


def _rwkv7_time_mix(h, p, *, B, C, L):
    M, D = h.shape
    ts = _token_shift(h, B=B, C=C, L=L)
    mix = p["mix"]
    proj = lambda j, w, nm: _matmul([h, ts], w, prologue=_lerp, prologue_vecs=[mix[j].reshape(1, D)], name=nm)
    r = proj(0, p["w_r"].astype(BF16), "rw_r")
    k = proj(2, p["w_k"].astype(BF16), "rw_k")
    v = proj(3, p["w_v"].astype(BF16), "rw_v")
    lw = p["w1"].shape[2]
    la = p["a1"].shape[2]
    lg = p["g1"].shape[2]
    lgp = -(-lg // V7X_LANES) * V7X_LANES
    w1c = jnp.concatenate([p["w1"][0], p["w1"][1]], axis=1).astype(BF16)
    a1c = jnp.concatenate([p["a1"][0], p["a1"][1]], axis=1).astype(BF16)
    g1p = jnp.pad(p["g1"], ((0, 0), (0, 0), (0, lgp - lg)))
    g1c = jnp.concatenate([g1p[0], g1p[1]], axis=1).astype(BF16)
    g2p = jnp.pad(p["g2"], ((0, 0), (0, lgp - lg), (0, 0))).astype(BF16)
    pw = proj(1, w1c, "rw_w1")
    pa = proj(4, a1c, "rw_a1")
    pg = proj(5, g1c, "rw_g1")

    def epi_decay(acc, w0, out_ref):
        out_ref[...] = jax.nn.sigmoid(acc + w0) * float(np.exp(-0.5))

    def epi_sigmoid(acc, a0, out_ref):
        out_ref[...] = jax.nn.sigmoid(acc + a0)

    o2 = []
    for d in range(2):
        e_d = _matmul([pw], p["w2"][d].astype(BF16), prologue=jnp.tanh, epilogue=epi_decay,
                      epi_vecs=[p["w0"][d].reshape(1, D)], a_col=d, name="rw_w2")
        a_d = _matmul([pa], p["a2"][d].astype(BF16), prologue=lambda z: z, epilogue=epi_sigmoid,
                      epi_vecs=[p["a0"][d].reshape(1, D)], a_col=d, name="rw_a2")
        g_d = _matmul([pg], g2p[d], prologue=jax.nn.sigmoid, a_col=d, name="rw_g2")
        o2.append(_wkv(r, k, v, e_d, a_d, g_d, p["k_k"], p["k_a"], p["r_k"], p["gn_w"], p["gn_b"],
                       B=B, C=C, L=L, rev=d == 1))
    return _matmul([o2[0], o2[1]], p["w_o"].astype(BF16),
                   prologue=lambda a, b: a.astype(F32) + b.astype(F32), name="rw_o")


def _router_params(router_w, router_b):
    D, E = router_w.shape
    rw = jnp.pad(router_w, ((0, 0), (0, V7X_LANES - E))).astype(BF16)
    rb = jnp.pad(router_b, (0, V7X_LANES - E)).reshape(1, V7X_LANES)
    return rw, rb


def kernel(x, c, ctx, c_ctx, l0_ada_w, l0_ada_b, l0_norm_mix, l0_norm_ffn, l0_na_w_qkv, l0_na_q_gain, l0_na_k_gain, l0_na_rpb, l0_na_w_o, l0_router_w, l0_router_b, l0_moe_w_gate, l0_moe_b_gate, l0_moe_w_up, l0_moe_b_up, l0_moe_w_down, l0_moe_b_down, l1_ada_w, l1_ada_b, l1_norm_mix, l1_norm_ffn, l1_rw_mix, l1_rw_w_r, l1_rw_w_k, l1_rw_w_v, l1_rw_k_k, l1_rw_k_a, l1_rw_r_k, l1_rw_w0, l1_rw_w1, l1_rw_w2, l1_rw_a0, l1_rw_a1, l1_rw_a2, l1_rw_g1, l1_rw_g2, l1_rw_gn_w, l1_rw_gn_b, l1_rw_w_o, l1_router_w, l1_router_b, l1_moe_w_gate, l1_moe_b_gate, l1_moe_w_up, l1_moe_b_up, l1_moe_w_down, l1_moe_b_down):
    B, L, D = x.shape
    C = ctx.shape[1]
    Lt = C + L
    M = B * Lt
    H = D // NA_HEAD_DIM
    E = l0_router_w.shape[1]
    assert C % ROW_TILE == 0 and L % ROW_TILE == 0 and L % GRID_W == 0
    tiles = dict(tiles_per_batch=Lt // ROW_TILE, ctx_tiles=C // ROW_TILE)
    xa = jnp.concatenate([ctx, x], axis=1).reshape(M, D)

    mod0 = _ada_mod(c, c_ctx, l0_ada_w, l0_ada_b)
    (h,) = _rows(xa, [], [], mod0, norm_w=l0_norm_mix, shift_row=0, scale_row=1, write_x=False,
                 name="l0_norm_mix", **tiles)
    scale = NA_HEAD_DIM ** -0.5
    gain = jnp.concatenate([jnp.tile(l0_na_q_gain * scale, H), jnp.tile(l0_na_k_gain, H), jnp.ones((D,), F32)])
    tn = 512 if D % 512 == 0 else D
    qkv = _matmul([h], l0_na_w_qkv.astype(BF16), out_dtypes=(BF16,),
                  epilogue=functools.partial(_qkv_epilogue, n_norm_tiles=2 * D // tn),
                  epi_vecs=[gain.reshape(1, 3 * D)], tn=tn, name="na_qkv")
    bias_tabs = _na_bias_tables(l0_na_rpb, L // GRID_W)
    att = _neighbourhood_attention(qkv, bias_tabs, B=B, C=C, L=L, D=D)
    y = _matmul([att], l0_na_w_o.astype(BF16), name="na_o")
    rw0, rb0 = _router_params(l0_router_w, l0_router_b)
    xa, h2, logits = _rows(xa, [y], [2], mod0, norm_w=l0_norm_ffn, shift_row=3, scale_row=4,
                           router=(rw0, rb0), name="l0_norm_ffn", **tiles)
    f = _moe_ffn(h2, logits[:, :E], l0_moe_w_gate.astype(BF16), l0_moe_b_gate, l0_moe_w_up.astype(BF16),
                 l0_moe_b_up, l0_moe_w_down.astype(BF16), l0_moe_b_down)

    mod1 = _ada_mod(c, c_ctx, l1_ada_w, l1_ada_b)
    (xa,) = _rows(xa, [f], [5], mod0, name="l0_moe_residual", **tiles)
    (h,) = _rows(xa, [], [], mod1, norm_w=l1_norm_mix, shift_row=0, scale_row=1, write_x=False,
                 name="l1_norm_mix", **tiles)
    rwp = dict(mix=l1_rw_mix, w_r=l1_rw_w_r, w_k=l1_rw_w_k, w_v=l1_rw_w_v, k_k=l1_rw_k_k, k_a=l1_rw_k_a,
               r_k=l1_rw_r_k, w0=l1_rw_w0, w1=l1_rw_w1, w2=l1_rw_w2, a0=l1_rw_a0, a1=l1_rw_a1, a2=l1_rw_a2,
               g1=l1_rw_g1, g2=l1_rw_g2, gn_w=l1_rw_gn_w, gn_b=l1_rw_gn_b, w_o=l1_rw_w_o)
    y = _rwkv7_time_mix(h, rwp, B=B, C=C, L=L)
    rw1, rb1 = _router_params(l1_router_w, l1_router_b)
    xa, h2, logits = _rows(xa, [y], [2], mod1, norm_w=l1_norm_ffn, shift_row=3, scale_row=4,
                           router=(rw1, rb1), name="l1_norm_ffn", **tiles)
    lat = lambda z: z.reshape(B, Lt, -1)[:, C:].reshape(B * L, -1)
    f = _moe_ffn(lat(h2), lat(logits)[:, :E], l1_moe_w_gate.astype(BF16), l1_moe_b_gate,
                 l1_moe_w_up.astype(BF16), l1_moe_b_up, l1_moe_w_down.astype(BF16), l1_moe_b_down)
    x_lat = lat(xa)
    mod1_lat = mod1[1::2]
    (out,) = _rows(x_lat, [f], [5], mod1_lat_expand(mod1_lat), name="l1_moe_residual",
                   tiles_per_batch=L // ROW_TILE, ctx_tiles=0)
    return out.reshape(B, L, D)


def mod1_lat_expand(mod_lat):
    return jnp.repeat(mod_lat, 2, axis=0)
```

```python
import functools

import numpy as np
import jax
import jax.numpy as jnp
from jax import lax
from jax.experimental import pallas as pl
from jax.experimental.pallas import tpu as pltpu

F32 = jnp.float32
BF16 = jnp.bfloat16

GRID_W = 64
N_MOD = 6
RMS_EPS = 1e-6
NEG_INF = -1e30
NA_HEAD_DIM = 128
NA_WIN_H = 8
NA_WIN_W = 16
RW_HEAD_DIM = 64
RW_GN_EPS = 64e-5
TOP_K = 4
SWIGLU_LIMIT = 7.0
SWIGLU_ALPHA = 1.702

V7X_LANES = 128
VMEM_LIMIT_BYTES = 56 * 1024 * 1024

ROW_TILE = 256
NA_QROWS = 4
NA_KROWS = 12
MOE_TILE = 256
WKV_T = 64
WKV_BLOCK = 256
WKV_LANES = 1024
WKV_SEG = 256


def _cparams(*sem):
    return pltpu.CompilerParams(dimension_semantics=sem, vmem_limit_bytes=VMEM_LIMIT_BYTES)


def _mm_body(*refs, n_a, n_pv, n_ev, n_em, n_out, prologue, epilogue, chunk):
    a_refs = refs[:n_a]
    pv_refs = refs[n_a:n_a + n_pv]
    w_ref = refs[n_a + n_pv]
    base = n_a + n_pv + 1
    ev_refs = refs[base:base + n_ev]
    em_refs = refs[base + n_ev:base + n_ev + n_em]
    out_refs = refs[base + n_ev + n_em:base + n_ev + n_em + n_out]
    scratch = refs[base + n_ev + n_em + n_out:]
    if prologue is not None:
        a_scr = scratch[0]

        @pl.when(pl.program_id(1) == 0)
        def _():
            pvs = [r[...] for r in pv_refs]

            def body(c, carry):
                rows = pl.ds(pl.multiple_of(c * chunk, chunk), chunk)
                a_scr[rows, :] = prologue(*[r[rows, :] for r in a_refs], *pvs).astype(BF16)
                return carry

            lax.fori_loop(0, a_scr.shape[0] // chunk, body, 0)

        a = a_scr[...]
    else:
        a = a_refs[0][...]
    acc = jnp.dot(a, w_ref[...].astype(BF16), preferred_element_type=F32)
    evs = [r[...] for r in ev_refs]
    ems = [r[...] for r in em_refs]
    if epilogue is None:
        out_refs[0][...] = acc.astype(out_refs[0].dtype)
    else:
        epilogue(acc, *evs, *ems, *out_refs)


def _matmul(a_list, w, *, out_dtypes=(F32,), prologue=None, prologue_vecs=(), epilogue=None,
            epi_vecs=(), epi_mats=(), a_col=0, tm=None, tn=None, name="mm"):
    M = a_list[0].shape[0]
    K, N = w.shape
    if tm is None:
        tm = 512 if M % 512 == 0 else (256 if M % 256 == 0 else M)
    if tn is None:
        tn = 512 if N % 512 == 0 else (256 if N % 256 == 0 else N)
    assert M % tm == 0 and N % tn == 0
    chunk = 32 if tm % 32 == 0 else tm
    in_specs = [pl.BlockSpec((tm, K), lambda m, n: (m, a_col)) for _ in a_list]
    in_specs += [pl.BlockSpec((1, K), lambda m, n: (0, 0)) for _ in prologue_vecs]
    in_specs += [pl.BlockSpec((K, tn), lambda m, n: (0, n))]
    in_specs += [pl.BlockSpec((1, tn), lambda m, n: (0, n)) for _ in epi_vecs]
    in_specs += [pl.BlockSpec((tm, tn), lambda m, n: (m, n)) for _ in epi_mats]
    out_shape = [jax.ShapeDtypeStruct((M, N), dt) for dt in out_dtypes]
    out_specs = [pl.BlockSpec((tm, tn), lambda m, n: (m, n)) for _ in out_dtypes]
    scratch = [pltpu.VMEM((tm, K), BF16)] if prologue is not None else []
    body = functools.partial(_mm_body, n_a=len(a_list), n_pv=len(prologue_vecs), n_ev=len(epi_vecs),
                             n_em=len(epi_mats), n_out=len(out_dtypes), prologue=prologue, epilogue=epilogue,
                             chunk=chunk)
    outs = pl.pallas_call(
        body, out_shape=out_shape, grid=(M // tm, N // tn), in_specs=in_specs, out_specs=out_specs,
        scratch_shapes=scratch, compiler_params=_cparams("parallel", "arbitrary"), name=name,
    )(*a_list, *prologue_vecs, w, *epi_vecs, *epi_mats)
    return outs[0] if len(out_dtypes) == 1 else outs


def _rows_body(*refs, n_add, gate_rows, do_norm, shift_row, scale_row, has_router, write_x, chunk):
    x_ref = refs[0]
    add_refs = refs[1:1 + n_add]
    mod_ref = refs[1 + n_add]
    pos = 2 + n_add
    if do_norm:
        nw_ref = refs[pos]
        pos += 1
    if has_router:
        rw_ref, rb_ref = refs[pos], refs[pos + 1]
        pos += 2
    outs = refs[pos:]
    oi = 0
    if write_x:
        xo_ref = outs[oi]
        oi += 1
    if do_norm:
        h_ref = outs[oi]
        oi += 1
    if has_router:
        lg_ref = outs[oi]
    gates = [mod_ref[0, g:g + 1, :] for g in gate_rows]
    if do_norm:
        nw = nw_ref[...]
        sh = mod_ref[0, shift_row:shift_row + 1, :]
        sc = 1.0 + mod_ref[0, scale_row:scale_row + 1, :]

    def body(c, carry):
        rows = pl.ds(pl.multiple_of(c * chunk, chunk), chunk)
        x = x_ref[rows, :]
        for a_ref, g in zip(add_refs, gates):
            x = x + g * a_ref[rows, :].astype(F32)
        if write_x:
            xo_ref[rows, :] = x
        if do_norm:
            y = x * lax.rsqrt(jnp.mean(x * x, axis=-1, keepdims=True) + RMS_EPS)
            h_ref[rows, :] = ((y * nw) * sc + sh).astype(h_ref.dtype)
        return carry

    lax.fori_loop(0, x_ref.shape[0] // chunk, body, 0)
    if has_router:
        lg_ref[...] = jnp.dot(h_ref[...].astype(BF16), rw_ref[...], preferred_element_type=F32) + rb_ref[...]


def _rows(x, adds, gate_rows, mod, *, rows_per_batch, ctx_rows, x_rows_per_batch=None, x_row_offset=0,
          add_row_offsets=None, norm_w=None, shift_row=0, scale_row=1, h_dtype=None, router=None, write_x=True,
          tr=ROW_TILE, name="rows"):
    D = x.shape[1]
    B = mod.shape[0] // 2
    M = B * rows_per_batch
    x_rows_per_batch = x_rows_per_batch or rows_per_batch
    assert rows_per_batch % tr == 0 and ctx_rows % tr == 0 and x_rows_per_batch % tr == 0 and x_row_offset % tr == 0
    tiles_per_batch, ctx_tiles = rows_per_batch // tr, ctx_rows // tr
    x_tiles_per_batch, x_tile_offset = x_rows_per_batch // tr, x_row_offset // tr
    add_row_offsets = list(add_row_offsets or [0] * len(adds))
    assert all(off % tr == 0 for off in add_row_offsets)
    do_norm = norm_w is not None

    def mod_idx(i):
        return ((i // tiles_per_batch) * 2 + ((i % tiles_per_batch) >= ctx_tiles).astype(jnp.int32), 0, 0)

    row_spec = pl.BlockSpec((tr, D), lambda i: (i, 0))
    x_spec = pl.BlockSpec((tr, D), lambda i: ((i // tiles_per_batch) * x_tiles_per_batch + x_tile_offset
                                               + i % tiles_per_batch, 0))
    add_specs = [pl.BlockSpec((tr, D), lambda i, t=off // tr: (t + i, 0)) for off in add_row_offsets]
    args = [x, *adds, mod]
    in_specs = [x_spec] + add_specs + [pl.BlockSpec((1, 8, D), mod_idx)]
    if do_norm:
        args.append(norm_w.reshape(1, D))
        in_specs.append(pl.BlockSpec((1, D), lambda i: (0, 0)))
    if router is not None:
        rw, rb = router
        args += [rw, rb]
        in_specs += [pl.BlockSpec(rw.shape, lambda i: (0, 0)), pl.BlockSpec(rb.shape, lambda i: (0, 0))]
    out_shape, out_specs = [], []
    if write_x:
        out_shape.append(jax.ShapeDtypeStruct((M, D), F32))
        out_specs.append(row_spec)
    if do_norm:
        out_shape.append(jax.ShapeDtypeStruct((M, D), h_dtype or BF16))
        out_specs.append(row_spec)
    if router is not None:
        out_shape.append(jax.ShapeDtypeStruct((M, V7X_LANES), F32))
        out_specs.append(pl.BlockSpec((tr, V7X_LANES), lambda i: (i, 0)))
    body = functools.partial(_rows_body, n_add=len(adds), gate_rows=tuple(gate_rows),
                             do_norm=do_norm, shift_row=shift_row, scale_row=scale_row,
                             has_router=router is not None, write_x=write_x, chunk=32)
    return pl.pallas_call(body, out_shape=out_shape, grid=(M // tr,), in_specs=in_specs, out_specs=out_specs,
                          compiler_params=_cparams("parallel"), name=name)(*args)


def _ada_mod(c, c_ctx, ada_w, ada_b):
    B, D = c.shape
    cond = jnp.concatenate([c, c_ctx[None, :], jnp.zeros((8 - B - 1, D), F32)], axis=0)

    def epi(acc, bias, out_ref):
        out_ref[...] = acc + bias

    out = _matmul([cond], ada_w, prologue=lambda a: a * jax.nn.sigmoid(a), epilogue=epi,
                  epi_vecs=[ada_b.reshape(1, -1)], tm=8, tn=512, name="ada_mod")
    lat = out[:B].reshape(B, N_MOD, D)
    ctx = jnp.broadcast_to(out[B].reshape(1, N_MOD, D), (B, N_MOD, D))
    mod = jnp.stack([ctx, lat], axis=1).reshape(2 * B, N_MOD, D)
    return jnp.pad(mod, ((0, 0), (0, 8 - N_MOD), (0, 0)))


def _qkv_epilogue(acc, gain, out_ref, *, n_norm_tiles):
    tn = acc.shape[1]

    @pl.when(pl.program_id(1) < n_norm_tiles)
    def _():
        for i in range(tn // NA_HEAD_DIM):
            sl = slice(i * NA_HEAD_DIM, (i + 1) * NA_HEAD_DIM)
            z = acc[:, sl]
            z = z * lax.rsqrt(jnp.mean(z * z, axis=-1, keepdims=True) + RMS_EPS)
            out_ref[:, sl] = (z * gain[:, sl]).astype(out_ref.dtype)

    @pl.when(pl.program_id(1) >= n_norm_tiles)
    def _():
        out_ref[...] = acc.astype(out_ref.dtype)


def _na_bias_tables(rpb, rows):
    nblk = rows // NA_QROWS
    kh = min(NA_WIN_H, rows)
    assert kh == NA_WIN_H and rows >= NA_KROWS + NA_QROWS and rows % NA_QROWS == 0
    col = np.arange(GRID_W)
    col_start = np.clip(col - NA_WIN_W // 2, 0, GRID_W - NA_WIN_W)
    col_ok = (col[None, :] >= col_start[:, None]) & (col[None, :] < col_start[:, None] + NA_WIN_W)
    dc = np.clip(col[None, :] - col[:, None] + NA_WIN_W - 1, 0, 2 * NA_WIN_W - 2)
    tabs = []
    for j in (0, 1, nblk - 1):
        s0 = int(np.clip(NA_QROWS * j - NA_WIN_H // 2, 0, rows - NA_KROWS))
        qr = NA_QROWS * j + np.arange(NA_QROWS)
        rs = np.clip(qr - kh // 2, 0, rows - kh)
        kr = s0 + np.arange(NA_KROWS)
        row_ok = (kr[None, :] >= rs[:, None]) & (kr[None, :] < rs[:, None] + kh)
        dr = np.clip(kr[None, :] - qr[:, None] + NA_WIN_H - 1, 0, 2 * NA_WIN_H - 2)
        ok = row_ok[:, None, :, None] & col_ok[None, :, None, :]
        shape = (NA_QROWS, GRID_W, NA_KROWS, GRID_W)
        dr_f = np.broadcast_to(dr[:, None, :, None], shape).reshape(NA_QROWS * GRID_W, NA_KROWS * GRID_W)
        dc_f = np.broadcast_to(dc[None, :, None, :], shape).reshape(NA_QROWS * GRID_W, NA_KROWS * GRID_W)
        ok_f = ok.reshape(NA_QROWS * GRID_W, NA_KROWS * GRID_W)
        tabs.append(jnp.where(ok_f[None], rpb.astype(F32)[:, dr_f, dc_f], NEG_INF))
    return jnp.stack(tabs, axis=0)


def _na_body(q_ref, k_ref, v_ref, bias_ref, o_ref, *, C, rows):
    nq = NA_QROWS * GRID_W
    nk = NA_KROWS * GRID_W
    nblk = rows // NA_QROWS
    dn = (((1,), (1,)), ((), ()))
    kc = k_ref[0:C, :]
    vc = v_ref[0:C, :]
    s = lax.dot_general(q_ref[0:C, :], kc, dn, preferred_element_type=F32)
    p = jnp.exp(s - jnp.max(s, axis=-1, keepdims=True))
    o = jnp.dot(p.astype(BF16), vc, preferred_element_type=F32) / jnp.sum(p, axis=-1, keepdims=True)
    o_ref[0:C, :] = o.astype(o_ref.dtype)

    def body(j, carry):
        s0 = jnp.clip(NA_QROWS * j - NA_WIN_H // 2, 0, rows - NA_KROWS)
        cls = jnp.where(j == 0, 0, jnp.where(j == nblk - 1, 2, 1))
        qrows = pl.ds(pl.multiple_of(C + j * nq, GRID_W), nq)
        krows = pl.ds(pl.multiple_of(C + s0 * GRID_W, GRID_W), nk)
        qb = q_ref[qrows, :]
        kb = k_ref[krows, :]
        vb = v_ref[krows, :]
        s_lat = lax.dot_general(qb, kb, dn, preferred_element_type=F32) + bias_ref[cls, 0]
        s_ctx = lax.dot_general(qb, kc, dn, preferred_element_type=F32)
        m = jnp.maximum(jnp.max(s_lat, axis=-1, keepdims=True), jnp.max(s_ctx, axis=-1, keepdims=True))
        p_lat = jnp.exp(s_lat - m)
        p_ctx = jnp.exp(s_ctx - m)
        den = jnp.sum(p_lat, axis=-1, keepdims=True) + jnp.sum(p_ctx, axis=-1, keepdims=True)
        o = (jnp.dot(p_ctx.astype(BF16), vc, preferred_element_type=F32)
             + jnp.dot(p_lat.astype(BF16), vb, preferred_element_type=F32)) / den
        o_ref[qrows, :] = o.astype(o_ref.dtype)
        return carry

    lax.fori_loop(0, nblk, body, 0)


def _neighbourhood_attention(qkv, bias_tabs, *, B, C, L, D):
    H = D // NA_HEAD_DIM
    Lt = C + L
    rows = L // GRID_W
    blk = (Lt, NA_HEAD_DIM)
    body = functools.partial(_na_body, C=C, rows=rows)
    return pl.pallas_call(
        body, out_shape=jax.ShapeDtypeStruct((B * Lt, D), BF16), grid=(B, H),
        in_specs=[pl.BlockSpec(blk, lambda b, h: (b, h)),
                  pl.BlockSpec(blk, lambda b, h: (b, H + h)),
                  pl.BlockSpec(blk, lambda b, h: (b, 2 * H + h)),
                  pl.BlockSpec((3, 1) + bias_tabs.shape[2:], lambda b, h: (0, h, 0, 0))],
        out_specs=pl.BlockSpec(blk, lambda b, h: (b, h)),
        compiler_params=_cparams("parallel", "parallel"), name="na_attention",
    )(qkv, qkv, qkv, bias_tabs)


def _moe_body(be_ref, tok0_ref, tokn_ref, slot_ref, h_hbm, w_ref, wg_ref, bg_ref, wu_ref, bu_ref, wd_ref, bd_ref,
              ys_hbm, xbuf, ybuf, gsem, ssem):
    del be_ref
    tm = xbuf.shape[1]
    i = pl.program_id(0)
    n_blocks = pl.num_programs(0)
    cur = lax.rem(i, 2)

    def start_gather(tok_ref, buf):
        def body(r, carry):
            pltpu.make_async_copy(h_hbm.at[pl.ds(tok_ref[0, 0, r], 1)], xbuf.at[buf, pl.ds(r, 1)],
                                  gsem.at[buf]).start()
            return carry

        lax.fori_loop(0, tm, body, 0, unroll=8)

    def wait_gather(buf):
        pltpu.make_async_copy(h_hbm.at[pl.ds(0, tm)], xbuf.at[buf], gsem.at[buf]).wait()

    def wait_scatter(buf):
        pltpu.make_async_copy(ybuf.at[buf], ys_hbm.at[pl.ds(0, tm)], ssem.at[buf]).wait()

    @pl.when(i == 0)
    def _():
        start_gather(tok0_ref, 0)

    @pl.when(i + 1 < n_blocks)
    def _():
        start_gather(tokn_ref, 1 - cur)

    wait_gather(cur)
    x = xbuf[cur].astype(BF16)
    g = jnp.minimum(jnp.dot(x, wg_ref[0], preferred_element_type=F32) + bg_ref[0], SWIGLU_LIMIT)
    u = jnp.clip(jnp.dot(x, wu_ref[0], preferred_element_type=F32) + bu_ref[0], -SWIGLU_LIMIT, SWIGLU_LIMIT)
    hdn = g * jax.nn.sigmoid(SWIGLU_ALPHA * g) * (u + 1.0)
    y = jnp.dot(hdn.astype(BF16), wd_ref[0], preferred_element_type=F32) + bd_ref[0]

    @pl.when(i >= 2)
    def _():
        wait_scatter(cur)

    ybuf[cur] = y * w_ref[...]

    def scatter_row(r, carry):
        pltpu.make_async_copy(ybuf.at[cur, pl.ds(r, 1)], ys_hbm.at[pl.ds(slot_ref[0, 0, r], 1)],
                              ssem.at[cur]).start()
        return carry

    lax.fori_loop(0, tm, scatter_row, 0, unroll=8)

    @pl.when(i == n_blocks - 1)
    def _():
        wait_scatter(1 - cur)
        wait_scatter(cur)


def _moe_experts(h_all, row_src, row_slot, row_w, blk_exp, wg, bg, wu, bu, wd, bd):
    M, D = h_all.shape
    E, _, De = wg.shape
    tm = MOE_TILE
    n_rows = row_src.shape[0]
    n_blocks = n_rows // tm
    assert n_blocks >= 2
    smem_blk = lambda idx: pl.BlockSpec((1, 1, tm), idx, memory_space=pltpu.SMEM)
    grid_spec = pltpu.PrefetchScalarGridSpec(
        num_scalar_prefetch=1, grid=(n_blocks,),
        in_specs=[smem_blk(lambda i, be: (0, 0, 0)),
                  smem_blk(lambda i, be: (jnp.minimum(i + 1, n_blocks - 1), 0, 0)),
                  smem_blk(lambda i, be: (i, 0, 0)),
                  pl.BlockSpec(memory_space=pl.ANY),
                  pl.BlockSpec((tm, 1), lambda i, be: (i, 0)),
                  pl.BlockSpec((1, D, De), lambda i, be: (be[i], 0, 0)),
                  pl.BlockSpec((1, 1, De), lambda i, be: (be[i], 0, 0)),
                  pl.BlockSpec((1, D, De), lambda i, be: (be[i], 0, 0)),
                  pl.BlockSpec((1, 1, De), lambda i, be: (be[i], 0, 0)),
                  pl.BlockSpec((1, De, D), lambda i, be: (be[i], 0, 0)),
                  pl.BlockSpec((1, 1, D), lambda i, be: (be[i], 0, 0))],
        out_specs=pl.BlockSpec(memory_space=pl.ANY),
        scratch_shapes=[pltpu.VMEM((2, tm, D), F32), pltpu.VMEM((2, tm, D), F32),
                        pltpu.SemaphoreType.DMA((2,)), pltpu.SemaphoreType.DMA((2,))])
    src3 = row_src.reshape(n_blocks, 1, tm)
    return pl.pallas_call(
        _moe_body, out_shape=jax.ShapeDtypeStruct((n_rows, D), F32), grid_spec=grid_spec,
        compiler_params=_cparams("arbitrary"), name="moe_experts",
    )(blk_exp, src3, src3, row_slot.reshape(n_blocks, 1, tm), h_all, row_w.reshape(n_rows, 1), wg,
      bg.reshape(E, 1, De), wu, bu.reshape(E, 1, De), wd, bd.reshape(E, 1, D))


def _moe_ffn(h_all, tok_rows, logits, wg, bg, wu, bu, wd, bd):
    N = tok_rows.shape[0]
    E = wg.shape[0]
    tm = MOE_TILE
    top_val, top_idx = lax.top_k(logits, TOP_K)
    top_w = jax.nn.softmax(top_val, axis=-1)
    flat_e = top_idx.reshape(-1)
    onehot = (flat_e[:, None] == jnp.arange(E, dtype=flat_e.dtype)[None, :]).astype(jnp.int32)
    csum = jnp.cumsum(onehot, axis=0)
    counts = csum[-1]
    rank = jnp.take_along_axis(csum, flat_e[:, None], axis=1)[:, 0] - 1
    padded = (counts + tm - 1) // tm * tm
    pad_end = jnp.cumsum(padded)
    pad_start = pad_end - padded
    dest = pad_start[flat_e] + rank
    n_blocks = (N * TOP_K + tm - 1) // tm + E
    n_rows = n_blocks * tm
    row_src = jnp.zeros((n_rows,), jnp.int32).at[dest].set(jnp.repeat(tok_rows.astype(jnp.int32), TOP_K))
    row_w = jnp.zeros((n_rows,), F32).at[dest].set(top_w.reshape(-1))
    asg = jnp.arange(N * TOP_K, dtype=jnp.int32)
    row_slot = jnp.full((n_rows,), -1, jnp.int32).at[dest].set((asg % TOP_K) * N + asg // TOP_K)
    is_pad = row_slot < 0
    row_slot = jnp.where(is_pad, N * TOP_K - 1 + jnp.cumsum(is_pad.astype(jnp.int32)), row_slot)
    blk_start = jnp.arange(n_blocks, dtype=jnp.int32) * tm
    blk_exp = jnp.minimum(jnp.sum(blk_start[:, None] >= pad_end[None, :], axis=1), E - 1).astype(jnp.int32)
    return _moe_experts(h_all, row_src, row_slot, row_w, blk_exp, wg, bg, wu, bu, wd, bd)


def _block_id(idx, size):
    return lax.shift_right_logical(idx, int(np.log2(size)))


def _split2(x):
    hi = x.astype(BF16)
    lo = (x - hi.astype(F32)).astype(BF16)
    return hi, lo


def _dot01_lhs(m01, x):
    hi, lo = _split2(x)
    return jnp.dot(m01, hi, preferred_element_type=F32) + jnp.dot(m01, lo, preferred_element_type=F32)


def _dot01_rhs(x, m01):
    hi, lo = _split2(x)
    return jnp.dot(hi, m01, preferred_element_type=F32) + jnp.dot(lo, m01, preferred_element_type=F32)


def _wkv_body(r_ref, k_ref, v_ref, e_ref, a_ref, g_ref, kk_ref, ka_ref, rk_ref, gw_ref, gb_ref, o_ref,
              st_ref, rt_s, kt_s, at_s, bt_s, kw_s, bw_s, wt_s, bn_s, y_s, *, rev):
    T = WKV_T
    TB, LW = r_ref.shape
    n_pairs = LW // V7X_LANES
    n_chunks = TB // T

    @pl.when(pl.program_id(2) == 0)
    def _():
        st_ref[...] = jnp.zeros_like(st_ref)

    def earlier(p, q, strict):
        lo, hi = (p, q) if rev else (q, p)
        return lo < hi if strict else lo <= hi

    seg = min(LW, WKV_SEG)
    ri = lax.broadcasted_iota(jnp.int32, (TB, TB), 0)
    ci = lax.broadcasted_iota(jnp.int32, (TB, TB), 1)
    same_chunk = _block_id(ri, T) == _block_id(ci, T)
    tri = jnp.where(same_chunk & earlier(ri, ci, False), 1.0, 0.0).astype(BF16)
    ones_c = jnp.where(same_chunk, 1.0, 0.0).astype(BF16)
    li = _block_id(lax.broadcasted_iota(jnp.int32, (seg, seg), 0), RW_HEAD_DIM)
    lj = _block_id(lax.broadcasted_iota(jnp.int32, (seg, seg), 1), RW_HEAD_DIM)
    head_ones = jnp.where(li == lj, 1.0, 0.0).astype(BF16)

    for sg in range(LW // seg):
        ls = slice(sg * seg, (sg + 1) * seg)
        r = r_ref[:, ls]
        k = k_ref[:, ls]
        e = e_ref[:, ls]
        a = a_ref[:, ls]
        kkv = k * kk_ref[:, ls]
        kk = kkv * lax.rsqrt(_dot01_rhs(kkv * kkv, head_ones) + 1e-12)
        kd = k * (1.0 + (a - 1.0) * ka_ref[:, ls])
        be = kk * a
        cum = _dot01_lhs(tri, e)
        tot = _dot01_lhs(ones_c, e)
        ecum = jnp.exp(cum)
        edec = jnp.exp(cum - tot)
        rt_s[:, ls] = r * jnp.exp(-cum)
        kt_s[:, ls] = kd * ecum
        at_s[:, ls] = -kk * jnp.exp(e - cum)
        bt_s[:, ls] = be * ecum
        kw_s[:, ls] = kd * edec
        bw_s[:, ls] = be * edec
        wt_s[:, ls] = jnp.exp(-tot)
        bn_s[:, ls] = _dot01_rhs(r * kd * rk_ref[:, ls], head_ones) * v_ref[:, ls]

    lane = lax.broadcasted_iota(jnp.int32, (1, V7X_LANES), 1)
    m_a = jnp.where(lane < RW_HEAD_DIM, 1.0, 0.0).astype(F32)
    m_b = 1.0 - m_a
    pr = lax.broadcasted_iota(jnp.int32, (2 * T, 2 * T), 0)
    pc = lax.broadcasted_iota(jnp.int32, (2 * T, 2 * T), 1)
    same_head = _block_id(pr, T) == _block_id(pc, T)
    incl = same_head & earlier(pr, pc, False)
    strict = same_head & earlier(pr, pc, True)
    hr = _block_id(lax.broadcasted_iota(jnp.int32, (V7X_LANES, V7X_LANES), 0), RW_HEAD_DIM)
    hc = _block_id(lax.broadcasted_iota(jnp.int32, (V7X_LANES, V7X_LANES), 1), RW_HEAD_DIM)
    head_blk = hr == hc
    dn_nt = (((1,), (1,)), ((), ()))
    dn_tn = (((0,), (0,)), ((), ()))
    n_steps = int(np.log2(T))

    def stack(x):
        return jnp.concatenate([x * m_a, x * m_b], axis=0)

    def chunk_body(it, carry):
        c = (n_chunks - 1 - it) if rev else it
        rows = pl.ds(pl.multiple_of(c * T, T), T)
        row0 = pl.ds(pl.multiple_of(c * T, T), 1)
        lanes = [slice(p * V7X_LANES, (p + 1) * V7X_LANES) for p in range(n_pairs)]
        pairs = range(n_pairs)
        st = [st_ref[p] for p in pairs]
        st_b = [z.astype(BF16) for z in st]
        v = [v_ref[rows, ln] for ln in lanes]
        r_ms = [stack(rt_s[rows, ln]).astype(BF16) for ln in lanes]
        a_ms = [stack(at_s[rows, ln]).astype(BF16) for ln in lanes]
        k_ms = [stack(kt_s[rows, ln]).astype(BF16) for ln in lanes]
        b_ms = [stack(bt_s[rows, ln]).astype(BF16) for ln in lanes]
        v_ms = [stack(z).astype(BF16) for z in v]
        mm = [lax.dot_general(jnp.concatenate([r_ms[p], a_ms[p]], axis=0),
                              jnp.concatenate([k_ms[p], b_ms[p]], axis=0), dn_nt, preferred_element_type=F32)
              for p in pairs]
        m_rk = [jnp.where(incl, z[:2 * T, :2 * T], 0.0).astype(BF16) for z in mm]
        m_rb = [jnp.where(incl, z[:2 * T, 2 * T:], 0.0).astype(BF16) for z in mm]
        m_ak = [jnp.where(strict, z[2 * T:, :2 * T], 0.0).astype(BF16) for z in mm]
        nmat = [jnp.where(strict, z[2 * T:, 2 * T:], 0.0) for z in mm]
        x = [jnp.dot(jnp.concatenate([a_ms[p], m_ak[p]], axis=1), jnp.concatenate([st_b[p], v_ms[p]], axis=0),
                     preferred_element_type=F32) for p in pairs]
        for s in range(n_steps):
            nb = [z.astype(BF16) for z in nmat]
            if s < n_steps - 1:
                nx = [jnp.dot(nb[p], jnp.concatenate([nb[p], x[p].astype(BF16)], axis=1),
                              preferred_element_type=F32) for p in pairs]
                nmat = [z[:, :2 * T] for z in nx]
                x = [x[p] + nx[p][:, 2 * T:] for p in pairs]
            else:
                x = [x[p] + jnp.dot(nb[p], x[p].astype(BF16), preferred_element_type=F32) for p in pairs]
        y_stk = [jnp.dot(jnp.concatenate([r_ms[p], m_rk[p], m_rb[p]], axis=1),
                         jnp.concatenate([st_b[p], v_ms[p], x[p].astype(BF16)], axis=0),
                         preferred_element_type=F32) for p in pairs]
        upd = []
        for p, ln in enumerate(lanes):
            u = x[p][:T] + x[p][T:]
            kbw_t = jnp.concatenate([kw_s[rows, ln], bw_s[rows, ln]], axis=0).T
            upd.append(jnp.dot(kbw_t.astype(BF16), jnp.concatenate([v[p], u], axis=0).astype(BF16),
                               preferred_element_type=F32))
        for p, ln in enumerate(lanes):
            y_s[rows, ln] = y_stk[p][:T] + y_stk[p][T:]
            w_col = jnp.broadcast_to(wt_s[row0, ln], (V7X_LANES, V7X_LANES)).T
            st_ref[p] = w_col * st[p] + jnp.where(head_blk, upd[p], 0.0)
        return carry

    lax.fori_loop(0, n_chunks, chunk_body, 0)

    inv_n = 1.0 / RW_HEAD_DIM
    for sg in range(LW // seg):
        ls = slice(sg * seg, (sg + 1) * seg)
        y = y_s[:, ls]
        yc = y - _dot01_rhs(y, head_ones) * inv_n
        var = _dot01_rhs(yc * yc, head_ones) * inv_n
        yn = yc * lax.rsqrt(var + RW_GN_EPS) * gw_ref[:, ls] + gb_ref[:, ls]
        o_ref[:, ls] = ((yn + bn_s[:, ls]) * g_ref[:, ls]).astype(o_ref.dtype)


def _wkv(r, k, v, e, a, g, k_k, k_a, r_k, gn_w, gn_b, *, B, C, L, rev):
    M, D = r.shape
    Lt = C + L
    TB, LW = WKV_BLOCK, min(WKV_LANES, D)
    assert C % TB == 0 and L % TB == 0 and D % LW == 0
    nb, ncb = Lt // TB, C // TB

    def tblk(b, i):
        if not rev:
            return b * nb + i
        return b * nb + jnp.where(i < ncb, ncb - 1 - i, nb - 1 - i + ncb)

    blk = pl.BlockSpec((TB, LW), lambda b, gi, i: (tblk(b, i), gi))
    vec = pl.BlockSpec((1, LW), lambda b, gi, i: (0, gi))
    tbuf = pltpu.VMEM((TB, LW), F32)
    return pl.pallas_call(
        functools.partial(_wkv_body, rev=rev), out_shape=jax.ShapeDtypeStruct((M, D), BF16),
        grid=(B, D // LW, nb), in_specs=[blk] * 6 + [vec] * 5, out_specs=blk,
        scratch_shapes=[pltpu.VMEM((LW // V7X_LANES, V7X_LANES, V7X_LANES), F32)] + [tbuf] * 9,
        compiler_params=_cparams("parallel", "parallel", "arbitrary"), name="wkv7_bwd" if rev else "wkv7_fwd",
    )(r, k, v, e, a, g, k_k.reshape(1, D), k_a.reshape(1, D), r_k.reshape(1, D), gn_w.reshape(1, D),
      gn_b.reshape(1, D))


def _token_shift(h, *, B, C, L):
    M, D = h.shape
    half = D // 2
    t = h.reshape(B, C + L, D)

    def one(z):
        prev = jnp.pad(z[:, :-1, :half], ((0, 0), (1, 0), (0, 0)))
        nxt = jnp.pad(z[:, 1:, half:], ((0, 0), (0, 1), (0, 0)))
        return jnp.concatenate([prev, nxt], axis=-1)

    return jnp.concatenate([one(t[:, :C]), one(t[:, C:])], axis=1).reshape(M, D)


def _lerp(t, ts, mix):
    tf = t.astype(F32)
    return tf + (ts.astype(F32) - tf) * mix


def _rwkv7_time_mix(h, p, *, B, C, L):
    M, D = h.shape
    ts = _token_shift(h, B=B, C=C, L=L)
    mix = p["mix"]
    proj = lambda j, w, nm: _matmul([h, ts], w, prologue=_lerp, prologue_vecs=[mix[j].reshape(1, D)], name=nm)
    r = proj(0, p["w_r"].astype(BF16), "rw_r")
    k = proj(2, p["w_k"].astype(BF16), "rw_k")
    v = proj(3, p["w_v"].astype(BF16), "rw_v")
    lw = p["w1"].shape[2]
    la = p["a1"].shape[2]
    lg = p["g1"].shape[2]
    lgp = -(-lg // V7X_LANES) * V7X_LANES
    w1c = jnp.concatenate([p["w1"][0], p["w1"][1]], axis=1).astype(BF16)
    a1c = jnp.concatenate([p["a1"][0], p["a1"][1]], axis=1).astype(BF16)
    g1p = jnp.pad(p["g1"], ((0, 0), (0, 0), (0, lgp - lg)))
    g1c = jnp.concatenate([g1p[0], g1p[1]], axis=1).astype(BF16)
    g2p = jnp.pad(p["g2"], ((0, 0), (0, lgp - lg), (0, 0))).astype(BF16)
    pw = proj(1, w1c, "rw_w1")
    pa = proj(4, a1c, "rw_a1")
    pg = proj(5, g1c, "rw_g1")

    def epi_decay(acc, w0, out_ref):
        out_ref[...] = jax.nn.sigmoid(acc + w0) * float(np.exp(-0.5))

    def epi_sigmoid(acc, a0, out_ref):
        out_ref[...] = jax.nn.sigmoid(acc + a0)

    o2 = []
    for d in range(2):
        e_d = _matmul([pw], p["w2"][d].astype(BF16), prologue=jnp.tanh, epilogue=epi_decay,
                      epi_vecs=[p["w0"][d].reshape(1, D)], a_col=d, name="rw_w2")
        a_d = _matmul([pa], p["a2"][d].astype(BF16), prologue=lambda z: z, epilogue=epi_sigmoid,
                      epi_vecs=[p["a0"][d].reshape(1, D)], a_col=d, name="rw_a2")
        g_d = _matmul([pg], g2p[d], prologue=jax.nn.sigmoid, a_col=d, name="rw_g2")
        o2.append(_wkv(r, k, v, e_d, a_d, g_d, p["k_k"], p["k_a"], p["r_k"], p["gn_w"], p["gn_b"],
                       B=B, C=C, L=L, rev=d == 1))
    return _matmul([o2[0], o2[1]], p["w_o"].astype(BF16),
                   prologue=lambda a, b: a.astype(F32) + b.astype(F32), name="rw_o")


def _router_params(router_w, router_b):
    D, E = router_w.shape
    rw = jnp.pad(router_w, ((0, 0), (0, V7X_LANES - E))).astype(BF16)
    rb = jnp.pad(router_b, (0, V7X_LANES - E)).reshape(1, V7X_LANES)
    return rw, rb


def kernel(x, c, ctx, c_ctx, l0_ada_w, l0_ada_b, l0_norm_mix, l0_norm_ffn, l0_na_w_qkv, l0_na_q_gain, l0_na_k_gain, l0_na_rpb, l0_na_w_o, l0_router_w, l0_router_b, l0_moe_w_gate, l0_moe_b_gate, l0_moe_w_up, l0_moe_b_up, l0_moe_w_down, l0_moe_b_down, l1_ada_w, l1_ada_b, l1_norm_mix, l1_norm_ffn, l1_rw_mix, l1_rw_w_r, l1_rw_w_k, l1_rw_w_v, l1_rw_k_k, l1_rw_k_a, l1_rw_r_k, l1_rw_w0, l1_rw_w1, l1_rw_w2, l1_rw_a0, l1_rw_a1, l1_rw_a2, l1_rw_g1, l1_rw_g2, l1_rw_gn_w, l1_rw_gn_b, l1_rw_w_o, l1_router_w, l1_router_b, l1_moe_w_gate, l1_moe_b_gate, l1_moe_w_up, l1_moe_b_up, l1_moe_w_down, l1_moe_b_down):
    B, L, D = x.shape
    C = ctx.shape[1]
    Lt = C + L
    M = B * Lt
    H = D // NA_HEAD_DIM
    E = l0_router_w.shape[1]
    assert C % ROW_TILE == 0 and L % ROW_TILE == 0 and L % GRID_W == 0
    tiles = dict(rows_per_batch=Lt, ctx_rows=C)
    xa = jnp.concatenate([ctx, x], axis=1).reshape(M, D)

    mod0 = _ada_mod(c, c_ctx, l0_ada_w, l0_ada_b)
    (h,) = _rows(xa, [], [], mod0, norm_w=l0_norm_mix, shift_row=0, scale_row=1, write_x=False,
                 name="l0_norm_mix", **tiles)
    scale = NA_HEAD_DIM ** -0.5
    gain = jnp.concatenate([jnp.tile(l0_na_q_gain * scale, H), jnp.tile(l0_na_k_gain, H), jnp.ones((D,), F32)])
    tn = 512 if D % 512 == 0 else D
    qkv = _matmul([h], l0_na_w_qkv.astype(BF16), out_dtypes=(BF16,),
                  epilogue=functools.partial(_qkv_epilogue, n_norm_tiles=2 * D // tn),
                  epi_vecs=[gain.reshape(1, 3 * D)], tn=tn, name="na_qkv")
    bias_tabs = _na_bias_tables(l0_na_rpb, L // GRID_W)
    att = _neighbourhood_attention(qkv, bias_tabs, B=B, C=C, L=L, D=D)
    y = _matmul([att], l0_na_w_o.astype(BF16), name="na_o")
    rw0, rb0 = _router_params(l0_router_w, l0_router_b)
    xa, h2, logits = _rows(xa, [y], [2], mod0, norm_w=l0_norm_ffn, shift_row=3, scale_row=4, h_dtype=F32,
                           router=(rw0, rb0), name="l0_norm_ffn", **tiles)
    ys = _moe_ffn(h2, jnp.arange(M, dtype=jnp.int32), logits[:, :E], l0_moe_w_gate.astype(BF16), l0_moe_b_gate,
                  l0_moe_w_up.astype(BF16), l0_moe_b_up, l0_moe_w_down.astype(BF16), l0_moe_b_down)
    (xa,) = _rows(xa, [ys] * TOP_K, [5] * TOP_K, mod0, add_row_offsets=[k * M for k in range(TOP_K)],
                  tr=ROW_TILE // 2, name="l0_moe_residual", **tiles)

    mod1 = _ada_mod(c, c_ctx, l1_ada_w, l1_ada_b)
    (h,) = _rows(xa, [], [], mod1, norm_w=l1_norm_mix, shift_row=0, scale_row=1, write_x=False,
                 name="l1_norm_mix", **tiles)
    rwp = dict(mix=l1_rw_mix, w_r=l1_rw_w_r, w_k=l1_rw_w_k, w_v=l1_rw_w_v, k_k=l1_rw_k_k, k_a=l1_rw_k_a,
               r_k=l1_rw_r_k, w0=l1_rw_w0, w1=l1_rw_w1, w2=l1_rw_w2, a0=l1_rw_a0, a1=l1_rw_a1, a2=l1_rw_a2,
               g1=l1_rw_g1, g2=l1_rw_g2, gn_w=l1_rw_gn_w, gn_b=l1_rw_gn_b, w_o=l1_rw_w_o)
    y = _rwkv7_time_mix(h, rwp, B=B, C=C, L=L)
    rw1, rb1 = _router_params(l1_router_w, l1_router_b)
    xa, h2, logits = _rows(xa, [y], [2], mod1, norm_w=l1_norm_ffn, shift_row=3, scale_row=4, h_dtype=F32,
                           router=(rw1, rb1), name="l1_norm_ffn", **tiles)
    lat_rows = (jnp.arange(B, dtype=jnp.int32)[:, None] * Lt + C + jnp.arange(L, dtype=jnp.int32)[None, :])
    lat_logits = logits.reshape(B, Lt, -1)[:, C:, :E].reshape(B * L, E)
    ys = _moe_ffn(h2, lat_rows.reshape(-1), lat_logits, l1_moe_w_gate.astype(BF16), l1_moe_b_gate,
                  l1_moe_w_up.astype(BF16), l1_moe_b_up, l1_moe_w_down.astype(BF16), l1_moe_b_down)
    (out,) = _rows(xa, [ys] * TOP_K, [5] * TOP_K, mod1, add_row_offsets=[k * B * L for k in range(TOP_K)],
                   rows_per_batch=L, ctx_rows=0, x_rows_per_batch=Lt, x_row_offset=C, tr=ROW_TILE // 2,
                   name="l1_moe_residual")
    return out.reshape(B, L, D)
```

```python
import functools

import numpy as np
import jax
import jax.numpy as jnp
from jax import lax
from jax.experimental import pallas as pl
from jax.experimental.pallas import tpu as pltpu

F32 = jnp.float32
BF16 = jnp.bfloat16

GRID_W = 64
N_MOD = 6
RMS_EPS = 1e-6
NEG_INF = -1e30
NA_HEAD_DIM = 128
NA_WIN_H = 8
NA_WIN_W = 16
RW_HEAD_DIM = 64
RW_GN_EPS = 64e-5
TOP_K = 4
SWIGLU_LIMIT = 7.0
SWIGLU_ALPHA = 1.702

V7X_LANES = 128
VMEM_LIMIT_BYTES = 56 * 1024 * 1024

ROW_TILE = 256
NA_QROWS = 4
NA_KROWS = 12
MOE_TILE = 256
WKV_T = 64
WKV_BLOCK = 256
WKV_LANES = 1024
WKV_SEG = 256


def _cparams(*sem, **kw):
    return pltpu.CompilerParams(dimension_semantics=sem, vmem_limit_bytes=VMEM_LIMIT_BYTES, **kw)


def _mm_body(*refs, n_a, n_pv, n_ev, n_em, n_out, prologue, epilogue, chunk):
    a_refs = refs[:n_a]
    pv_refs = refs[n_a:n_a + n_pv]
    w_ref = refs[n_a + n_pv]
    base = n_a + n_pv + 1
    ev_refs = refs[base:base + n_ev]
    em_refs = refs[base + n_ev:base + n_ev + n_em]
    out_refs = refs[base + n_ev + n_em:base + n_ev + n_em + n_out]
    scratch = refs[base + n_ev + n_em + n_out:]
    if prologue is not None:
        a_scr = scratch[0]

        @pl.when(pl.program_id(1) == 0)
        def _():
            pvs = [r[...] for r in pv_refs]

            def body(c, carry):
                rows = pl.ds(pl.multiple_of(c * chunk, chunk), chunk)
                a_scr[rows, :] = prologue(*[r[rows, :] for r in a_refs], *pvs).astype(BF16)
                return carry

            lax.fori_loop(0, a_scr.shape[0] // chunk, body, 0)

        a = a_scr[...]
    else:
        a = a_refs[0][...]
    acc = jnp.dot(a, w_ref[...].astype(BF16), preferred_element_type=F32)
    evs = [r[...] for r in ev_refs]
    ems = [r[...] for r in em_refs]
    if epilogue is None:
        out_refs[0][...] = acc.astype(out_refs[0].dtype)
    else:
        epilogue(acc, *evs, *ems, *out_refs)


def _matmul(a_list, w, *, out_dtypes=(F32,), prologue=None, prologue_vecs=(), epilogue=None,
            epi_vecs=(), epi_mats=(), a_col=0, tm=None, tn=None, name="mm"):
    M = a_list[0].shape[0]
    K, N = w.shape
    if tm is None:
        tm = 512 if M % 512 == 0 else (256 if M % 256 == 0 else M)
    if tn is None:
        tn = 512 if N % 512 == 0 else (256 if N % 256 == 0 else N)
    assert M % tm == 0 and N % tn == 0
    chunk = 32 if tm % 32 == 0 else tm
    in_specs = [pl.BlockSpec((tm, K), lambda m, n: (m, a_col)) for _ in a_list]
    in_specs += [pl.BlockSpec((1, K), lambda m, n: (0, 0)) for _ in prologue_vecs]
    in_specs += [pl.BlockSpec((K, tn), lambda m, n: (0, n))]
    in_specs += [pl.BlockSpec((1, tn), lambda m, n: (0, n)) for _ in epi_vecs]
    in_specs += [pl.BlockSpec((tm, tn), lambda m, n: (m, n)) for _ in epi_mats]
    out_shape = [jax.ShapeDtypeStruct((M, N), dt) for dt in out_dtypes]
    out_specs = [pl.BlockSpec((tm, tn), lambda m, n: (m, n)) for _ in out_dtypes]
    scratch = [pltpu.VMEM((tm, K), BF16)] if prologue is not None else []
    body = functools.partial(_mm_body, n_a=len(a_list), n_pv=len(prologue_vecs), n_ev=len(epi_vecs),
                             n_em=len(epi_mats), n_out=len(out_dtypes), prologue=prologue, epilogue=epilogue,
                             chunk=chunk)
    outs = pl.pallas_call(
        body, out_shape=out_shape, grid=(M // tm, N // tn), in_specs=in_specs, out_specs=out_specs,
        scratch_shapes=scratch, compiler_params=_cparams("parallel", "arbitrary"), name=name,
    )(*a_list, *prologue_vecs, w, *epi_vecs, *epi_mats)
    return outs[0] if len(out_dtypes) == 1 else outs


def _rows_body(*refs, n_add, gate_rows, do_norm, shift_row, scale_row, has_router, write_x, chunk):
    x_ref = refs[0]
    add_refs = refs[1:1 + n_add]
    mod_ref = refs[1 + n_add]
    pos = 2 + n_add
    if do_norm:
        nw_ref = refs[pos]
        pos += 1
    if has_router:
        rw_ref, rb_ref = refs[pos], refs[pos + 1]
        pos += 2
    outs = refs[pos:]
    oi = 0
    if write_x:
        xo_ref = outs[oi]
        oi += 1
    if do_norm:
        h_ref = outs[oi]
        oi += 1
    if has_router:
        lg_ref = outs[oi]
    gates = [mod_ref[0, g:g + 1, :] for g in gate_rows]
    if do_norm:
        nw = nw_ref[...]
        sh = mod_ref[0, shift_row:shift_row + 1, :]
        sc = 1.0 + mod_ref[0, scale_row:scale_row + 1, :]

    def body(c, carry):
        rows = pl.ds(pl.multiple_of(c * chunk, chunk), chunk)
        x = x_ref[rows, :]
        for a_ref, g in zip(add_refs, gates):
            x = x + g * a_ref[rows, :].astype(F32)
        if write_x:
            xo_ref[rows, :] = x
        if do_norm:
            y = x * lax.rsqrt(jnp.mean(x * x, axis=-1, keepdims=True) + RMS_EPS)
            h_ref[rows, :] = ((y * nw) * sc + sh).astype(h_ref.dtype)
        return carry

    lax.fori_loop(0, x_ref.shape[0] // chunk, body, 0)
    if has_router:
        lg_ref[...] = jnp.dot(h_ref[...].astype(BF16), rw_ref[...], preferred_element_type=F32) + rb_ref[...]


def _rows(x, adds, gate_rows, mod, *, rows_per_batch, ctx_rows, x_rows_per_batch=None, x_row_offset=0,
          add_row_offsets=None, norm_w=None, shift_row=0, scale_row=1, h_dtype=None, router=None, write_x=True,
          tr=ROW_TILE, name="rows"):
    D = x.shape[1]
    B = mod.shape[0] // 2
    M = B * rows_per_batch
    x_rows_per_batch = x_rows_per_batch or rows_per_batch
    assert rows_per_batch % tr == 0 and ctx_rows % tr == 0 and x_rows_per_batch % tr == 0 and x_row_offset % tr == 0
    tiles_per_batch, ctx_tiles = rows_per_batch // tr, ctx_rows // tr
    x_tiles_per_batch, x_tile_offset = x_rows_per_batch // tr, x_row_offset // tr
    add_row_offsets = list(add_row_offsets or [0] * len(adds))
    assert all(off % tr == 0 for off in add_row_offsets)
    do_norm = norm_w is not None

    def mod_idx(i):
        return ((i // tiles_per_batch) * 2 + ((i % tiles_per_batch) >= ctx_tiles).astype(jnp.int32), 0, 0)

    row_spec = pl.BlockSpec((tr, D), lambda i: (i, 0))
    x_spec = pl.BlockSpec((tr, D), lambda i: ((i // tiles_per_batch) * x_tiles_per_batch + x_tile_offset
                                               + i % tiles_per_batch, 0))
    add_specs = [pl.BlockSpec((tr, D), lambda i, t=off // tr: (t + i, 0)) for off in add_row_offsets]
    args = [x, *adds, mod]
    in_specs = [x_spec] + add_specs + [pl.BlockSpec((1, 8, D), mod_idx)]
    if do_norm:
        args.append(norm_w.reshape(1, D))
        in_specs.append(pl.BlockSpec((1, D), lambda i: (0, 0)))
    if router is not None:
        rw, rb = router
        args += [rw, rb]
        in_specs += [pl.BlockSpec(rw.shape, lambda i: (0, 0)), pl.BlockSpec(rb.shape, lambda i: (0, 0))]
    out_shape, out_specs = [], []
    if write_x:
        out_shape.append(jax.ShapeDtypeStruct((M, D), F32))
        out_specs.append(row_spec)
    if do_norm:
        out_shape.append(jax.ShapeDtypeStruct((M, D), h_dtype or BF16))
        out_specs.append(row_spec)
    if router is not None:
        out_shape.append(jax.ShapeDtypeStruct((M, V7X_LANES), F32))
        out_specs.append(pl.BlockSpec((tr, V7X_LANES), lambda i: (i, 0)))
    body = functools.partial(_rows_body, n_add=len(adds), gate_rows=tuple(gate_rows),
                             do_norm=do_norm, shift_row=shift_row, scale_row=scale_row,
                             has_router=router is not None, write_x=write_x, chunk=32)
    return pl.pallas_call(body, out_shape=out_shape, grid=(M // tr,), in_specs=in_specs, out_specs=out_specs,
                          compiler_params=_cparams("parallel"), name=name)(*args)


def _ada_mod(c, c_ctx, ada_w, ada_b):
    B, D = c.shape
    cond = jnp.concatenate([c, c_ctx[None, :], jnp.zeros((8 - B - 1, D), F32)], axis=0)

    def epi(acc, bias, out_ref):
        out_ref[...] = acc + bias

    out = _matmul([cond], ada_w, prologue=lambda a: a * jax.nn.sigmoid(a), epilogue=epi,
                  epi_vecs=[ada_b.reshape(1, -1)], tm=8, tn=512, name="ada_mod")
    lat = out[:B].reshape(B, N_MOD, D)
    ctx = jnp.broadcast_to(out[B].reshape(1, N_MOD, D), (B, N_MOD, D))
    mod = jnp.stack([ctx, lat], axis=1).reshape(2 * B, N_MOD, D)
    return jnp.pad(mod, ((0, 0), (0, 8 - N_MOD), (0, 0)))


def _qkv_epilogue(acc, gain, out_ref, *, n_norm_tiles):
    tn = acc.shape[1]

    @pl.when(pl.program_id(1) < n_norm_tiles)
    def _():
        for i in range(tn // NA_HEAD_DIM):
            sl = slice(i * NA_HEAD_DIM, (i + 1) * NA_HEAD_DIM)
            z = acc[:, sl]
            z = z * lax.rsqrt(jnp.mean(z * z, axis=-1, keepdims=True) + RMS_EPS)
            out_ref[:, sl] = (z * gain[:, sl]).astype(out_ref.dtype)

    @pl.when(pl.program_id(1) >= n_norm_tiles)
    def _():
        out_ref[...] = acc.astype(out_ref.dtype)


def _na_bias_tables(rpb, rows):
    nblk = rows // NA_QROWS
    kh = min(NA_WIN_H, rows)
    assert kh == NA_WIN_H and rows >= NA_KROWS + NA_QROWS and rows % NA_QROWS == 0
    col = np.arange(GRID_W)
    col_start = np.clip(col - NA_WIN_W // 2, 0, GRID_W - NA_WIN_W)
    col_ok = (col[None, :] >= col_start[:, None]) & (col[None, :] < col_start[:, None] + NA_WIN_W)
    dc = np.clip(col[None, :] - col[:, None] + NA_WIN_W - 1, 0, 2 * NA_WIN_W - 2)
    tabs = []
    for j in (0, 1, nblk - 1):
        s0 = int(np.clip(NA_QROWS * j - NA_WIN_H // 2, 0, rows - NA_KROWS))
        qr = NA_QROWS * j + np.arange(NA_QROWS)
        rs = np.clip(qr - kh // 2, 0, rows - kh)
        kr = s0 + np.arange(NA_KROWS)
        row_ok = (kr[None, :] >= rs[:, None]) & (kr[None, :] < rs[:, None] + kh)
        dr = np.clip(kr[None, :] - qr[:, None] + NA_WIN_H - 1, 0, 2 * NA_WIN_H - 2)
        ok = row_ok[:, None, :, None] & col_ok[None, :, None, :]
        ok_f = ok.reshape(NA_QROWS * GRID_W, NA_KROWS * GRID_W)
        by_row = rpb.astype(F32)[:, dr.reshape(-1), :]
        spread = np.zeros((2 * NA_WIN_W - 1, GRID_W * GRID_W), np.float32)
        spread[dc.reshape(-1), np.arange(GRID_W * GRID_W)] = 1.0
        full = jnp.einsum("hpc,cq->hpq", by_row, jnp.asarray(spread), precision=lax.Precision.HIGHEST)
        full = full.reshape(-1, NA_QROWS, NA_KROWS, GRID_W, GRID_W).transpose(0, 1, 3, 2, 4)
        tabs.append(jnp.where(ok_f[None], full.reshape(-1, NA_QROWS * GRID_W, NA_KROWS * GRID_W), NEG_INF))
    return jnp.stack(tabs, axis=0)


def _na_body(q_ref, k_ref, v_ref, bias_ref, o_ref, *, C, rows):
    nq = NA_QROWS * GRID_W
    nk = NA_KROWS * GRID_W
    nblk = rows // NA_QROWS
    dn = (((1,), (1,)), ((), ()))
    kc = k_ref[0:C, :]
    vc = v_ref[0:C, :]
    s = lax.dot_general(q_ref[0:C, :], kc, dn, preferred_element_type=F32)
    p = jnp.exp(s - jnp.max(s, axis=-1, keepdims=True))
    o = jnp.dot(p.astype(BF16), vc, preferred_element_type=F32) / jnp.sum(p, axis=-1, keepdims=True)
    o_ref[0:C, :] = o.astype(o_ref.dtype)

    def body(j, carry):
        s0 = jnp.clip(NA_QROWS * j - NA_WIN_H // 2, 0, rows - NA_KROWS)
        cls = jnp.where(j == 0, 0, jnp.where(j == nblk - 1, 2, 1))
        qrows = pl.ds(pl.multiple_of(C + j * nq, GRID_W), nq)
        krows = pl.ds(pl.multiple_of(C + s0 * GRID_W, GRID_W), nk)
        qb = q_ref[qrows, :]
        kb = k_ref[krows, :]
        vb = v_ref[krows, :]
        s_lat = lax.dot_general(qb, kb, dn, preferred_element_type=F32) + bias_ref[cls, 0]
        s_ctx = lax.dot_general(qb, kc, dn, preferred_element_type=F32)
        m = jnp.maximum(jnp.max(s_lat, axis=-1, keepdims=True), jnp.max(s_ctx, axis=-1, keepdims=True))
        p_lat = jnp.exp(s_lat - m)
        p_ctx = jnp.exp(s_ctx - m)
        den = jnp.sum(p_lat, axis=-1, keepdims=True) + jnp.sum(p_ctx, axis=-1, keepdims=True)
        o = (jnp.dot(p_ctx.astype(BF16), vc, preferred_element_type=F32)
             + jnp.dot(p_lat.astype(BF16), vb, preferred_element_type=F32)) / den
        o_ref[qrows, :] = o.astype(o_ref.dtype)
        return carry

    lax.fori_loop(0, nblk, body, 0)


def _neighbourhood_attention(qkv, bias_tabs, *, B, C, L, D):
    H = D // NA_HEAD_DIM
    Lt = C + L
    rows = L // GRID_W
    blk = (Lt, NA_HEAD_DIM)
    body = functools.partial(_na_body, C=C, rows=rows)
    return pl.pallas_call(
        body, out_shape=jax.ShapeDtypeStruct((B * Lt, D), BF16), grid=(B, H),
        in_specs=[pl.BlockSpec(blk, lambda b, h: (b, h)),
                  pl.BlockSpec(blk, lambda b, h: (b, H + h)),
                  pl.BlockSpec(blk, lambda b, h: (b, 2 * H + h)),
                  pl.BlockSpec((3, 1) + bias_tabs.shape[2:], lambda b, h: (0, h, 0, 0))],
        out_specs=pl.BlockSpec(blk, lambda b, h: (b, h)),
        compiler_params=_cparams("parallel", "parallel"), name="na_attention",
    )(qkv, qkv, qkv, bias_tabs)


def _moe_body(be_ref, tok0_ref, tokn_ref, slot_ref, h_hbm, w_ref, wg_ref, bg_ref, wu_ref, bu_ref, wd_ref, bd_ref,
              ys_hbm, xbuf, ybuf, gsem, ssem):
    del be_ref
    tm = xbuf.shape[1]
    i = pl.program_id(0)
    n_blocks = pl.num_programs(0)
    cur = lax.rem(i, 2)

    def start_gather(tok_ref, buf):
        for r in range(tm):
            pltpu.make_async_copy(h_hbm.at[pl.ds(tok_ref[0, 0, r], 1)], xbuf.at[buf, pl.ds(r, 1)],
                                  gsem.at[buf]).start()

    def wait_gather(buf):
        pltpu.make_async_copy(h_hbm.at[pl.ds(0, tm)], xbuf.at[buf], gsem.at[buf]).wait()

    def wait_scatter(buf):
        pltpu.make_async_copy(ybuf.at[buf], ys_hbm.at[pl.ds(0, tm)], ssem.at[buf]).wait()

    @pl.when(i == 0)
    def _():
        start_gather(tok0_ref, 0)

    start_gather(tokn_ref, 1 - cur)
    wait_gather(cur)
    x = xbuf[cur].astype(BF16)
    g = jnp.minimum(jnp.dot(x, wg_ref[0], preferred_element_type=F32) + bg_ref[0], SWIGLU_LIMIT)
    u = jnp.clip(jnp.dot(x, wu_ref[0], preferred_element_type=F32) + bu_ref[0], -SWIGLU_LIMIT, SWIGLU_LIMIT)
    hdn = g * jax.nn.sigmoid(SWIGLU_ALPHA * g) * (u + 1.0)
    y = jnp.dot(hdn.astype(BF16), wd_ref[0], preferred_element_type=F32) + bd_ref[0]

    @pl.when(i >= 2)
    def _():
        wait_scatter(cur)

    ybuf[cur] = y * w_ref[...]

    for r in range(tm):
        pltpu.make_async_copy(ybuf.at[cur, pl.ds(r, 1)], ys_hbm.at[pl.ds(slot_ref[0, 0, r], 1)],
                              ssem.at[cur]).start()

    @pl.when(i == n_blocks - 1)
    def _():
        wait_scatter(1 - cur)
        wait_scatter(cur)
        wait_gather(1 - cur)


def _moe_experts(h_all, row_src, row_slot, row_w, blk_exp, wg, bg, wu, bu, wd, bd):
    M, D = h_all.shape
    E, _, De = wg.shape
    tm = MOE_TILE
    n_rows = row_src.shape[0]
    n_blocks = n_rows // tm
    assert n_blocks >= 2
    smem_blk = lambda idx: pl.BlockSpec((1, 1, tm), idx, memory_space=pltpu.SMEM)
    grid_spec = pltpu.PrefetchScalarGridSpec(
        num_scalar_prefetch=1, grid=(n_blocks,),
        in_specs=[smem_blk(lambda i, be: (0, 0, 0)),
                  smem_blk(lambda i, be: (jnp.minimum(i + 1, n_blocks - 1), 0, 0)),
                  smem_blk(lambda i, be: (i, 0, 0)),
                  pl.BlockSpec(memory_space=pl.ANY),
                  pl.BlockSpec((tm, 1), lambda i, be: (i, 0)),
                  pl.BlockSpec((1, D, De), lambda i, be: (be[i], 0, 0)),
                  pl.BlockSpec((1, 1, De), lambda i, be: (be[i], 0, 0)),
                  pl.BlockSpec((1, D, De), lambda i, be: (be[i], 0, 0)),
                  pl.BlockSpec((1, 1, De), lambda i, be: (be[i], 0, 0)),
                  pl.BlockSpec((1, De, D), lambda i, be: (be[i], 0, 0)),
                  pl.BlockSpec((1, 1, D), lambda i, be: (be[i], 0, 0))],
        out_specs=pl.BlockSpec(memory_space=pl.ANY),
        scratch_shapes=[pltpu.VMEM((2, tm, D), F32), pltpu.VMEM((2, tm, D), F32),
                        pltpu.SemaphoreType.DMA((2,)), pltpu.SemaphoreType.DMA((2,))])
    src3 = row_src.reshape(n_blocks, 1, tm)
    return pl.pallas_call(
        _moe_body, out_shape=jax.ShapeDtypeStruct((n_rows, D), F32), grid_spec=grid_spec,
        compiler_params=_cparams("arbitrary", disable_bounds_checks=True), name="moe_experts",
    )(blk_exp, src3, src3, row_slot.reshape(n_blocks, 1, tm), h_all, row_w.reshape(n_rows, 1), wg,
      bg.reshape(E, 1, De), wu, bu.reshape(E, 1, De), wd, bd.reshape(E, 1, D))


def _moe_ffn(h_all, tok_rows, logits, wg, bg, wu, bu, wd, bd):
    N = tok_rows.shape[0]
    E = wg.shape[0]
    tm = MOE_TILE
    top_val, top_idx = lax.top_k(logits, TOP_K)
    top_w = jax.nn.softmax(top_val, axis=-1)
    flat_e = top_idx.reshape(-1)
    onehot = (flat_e[:, None] == jnp.arange(E, dtype=flat_e.dtype)[None, :]).astype(jnp.int32)
    csum = jnp.cumsum(onehot, axis=0)
    counts = csum[-1]
    rank = jnp.take_along_axis(csum, flat_e[:, None], axis=1)[:, 0] - 1
    padded = (counts + tm - 1) // tm * tm
    pad_end = jnp.cumsum(padded)
    pad_start = pad_end - padded
    dest = pad_start[flat_e] + rank
    n_blocks = (N * TOP_K + tm - 1) // tm + E
    n_rows = n_blocks * tm
    row_src = jnp.zeros((n_rows,), jnp.int32).at[dest].set(jnp.repeat(tok_rows.astype(jnp.int32), TOP_K))
    row_w = jnp.zeros((n_rows,), F32).at[dest].set(top_w.reshape(-1))
    asg = jnp.arange(N * TOP_K, dtype=jnp.int32)
    row_slot = jnp.full((n_rows,), -1, jnp.int32).at[dest].set((asg % TOP_K) * N + asg // TOP_K)
    is_pad = row_slot < 0
    row_slot = jnp.where(is_pad, N * TOP_K - 1 + jnp.cumsum(is_pad.astype(jnp.int32)), row_slot)
    blk_start = jnp.arange(n_blocks, dtype=jnp.int32) * tm
    blk_exp = jnp.minimum(jnp.sum(blk_start[:, None] >= pad_end[None, :], axis=1), E - 1).astype(jnp.int32)
    return _moe_experts(h_all, row_src, row_slot, row_w, blk_exp, wg, bg, wu, bu, wd, bd)


def _block_id(idx, size):
    return lax.shift_right_logical(idx, int(np.log2(size)))


def _split2(x):
    hi = x.astype(BF16)
    lo = (x - hi.astype(F32)).astype(BF16)
    return hi, lo


def _dot01_lhs(m01, x):
    hi, lo = _split2(x)
    return jnp.dot(m01, hi, preferred_element_type=F32) + jnp.dot(m01, lo, preferred_element_type=F32)


def _wkv_body(r_ref, k_ref, v_ref, e_ref, a_ref, g_ref, kk_ref, ka_ref, rk_ref, gw_ref, gb_ref, o_ref,
              st_ref, rt_s, kt_s, at_s, bt_s, cm_s, bn_s, y_s, *, rev):
    T = WKV_T
    TB, LW = r_ref.shape
    n_pairs = LW // V7X_LANES
    n_chunks = TB // T

    @pl.when(pl.program_id(2) == 0)
    def _():
        st_ref[...] = jnp.zeros_like(st_ref)

    def earlier(p, q, strict):
        lo, hi = (p, q) if rev else (q, p)
        return lo < hi if strict else lo <= hi

    seg = min(LW, WKV_SEG)
    ri = lax.broadcasted_iota(jnp.int32, (TB, TB), 0)
    ci = lax.broadcasted_iota(jnp.int32, (TB, TB), 1)
    same_chunk = _block_id(ri, T) == _block_id(ci, T)
    tri = jnp.where(same_chunk & earlier(ri, ci, False), 1.0, 0.0).astype(BF16)
    li = _block_id(lax.broadcasted_iota(jnp.int32, (seg, seg), 0), RW_HEAD_DIM)
    lj = _block_id(lax.broadcasted_iota(jnp.int32, (seg, seg), 1), RW_HEAD_DIM)
    head_ones = jnp.where(li == lj, 1.0, 0.0).astype(BF16)

    def head_sum(z):
        return jnp.dot(z.astype(BF16), head_ones, preferred_element_type=F32)

    for sg in range(LW // seg):
        ls = slice(sg * seg, (sg + 1) * seg)
        r = r_ref[:, ls]
        k = k_ref[:, ls]
        e = e_ref[:, ls]
        a = a_ref[:, ls]
        kkv = k * kk_ref[:, ls]
        kk = kkv * lax.rsqrt(head_sum(kkv * kkv) + 1e-12)
        kd = k * (1.0 + (a - 1.0) * ka_ref[:, ls])
        be = kk * a
        cum = _dot01_lhs(tri, e)
        ecum = jnp.exp(cum)
        cm_s[:, ls] = cum
        rt_s[:, ls] = r * jnp.exp(-cum)
        kt_s[:, ls] = kd * ecum
        at_s[:, ls] = -kk * jnp.exp(e - cum)
        bt_s[:, ls] = be * ecum
        bn_s[:, ls] = head_sum(r * kd * rk_ref[:, ls]) * v_ref[:, ls]

    lane = lax.broadcasted_iota(jnp.int32, (1, V7X_LANES), 1)
    m_a = jnp.where(lane < RW_HEAD_DIM, 1.0, 0.0).astype(F32)
    m_b = 1.0 - m_a
    pr = lax.broadcasted_iota(jnp.int32, (2 * T, 2 * T), 0)
    pc = lax.broadcasted_iota(jnp.int32, (2 * T, 2 * T), 1)
    same_head = _block_id(pr, T) == _block_id(pc, T)
    incl = same_head & earlier(pr, pc, False)
    strict = same_head & earlier(pr, pc, True)
    hr = _block_id(lax.broadcasted_iota(jnp.int32, (V7X_LANES, V7X_LANES), 0), RW_HEAD_DIM)
    hc = _block_id(lax.broadcasted_iota(jnp.int32, (V7X_LANES, V7X_LANES), 1), RW_HEAD_DIM)
    head_blk = hr == hc
    dn_nt = (((1,), (1,)), ((), ()))
    dn_tn = (((0,), (0,)), ((), ()))
    n_steps = int(np.log2(T))

    def stack(x):
        return jnp.concatenate([x * m_a, x * m_b], axis=0)

    def chunk_body(it, carry):
        c = (n_chunks - 1 - it) if rev else it
        rows = pl.ds(pl.multiple_of(c * T, T), T)
        rows_last = pl.ds(pl.multiple_of(c * T + (0 if rev else T - 8), 8), 8)
        last = slice(0, 1) if rev else slice(7, 8)
        lanes = [slice(p * V7X_LANES, (p + 1) * V7X_LANES) for p in range(n_pairs)]
        pairs = range(n_pairs)
        st = [st_ref[p] for p in pairs]
        st_b = [z.astype(BF16) for z in st]
        v = [v_ref[rows, ln] for ln in lanes]
        r_ms = [stack(rt_s[rows, ln]).astype(BF16) for ln in lanes]
        a_ms = [stack(at_s[rows, ln]).astype(BF16) for ln in lanes]
        k_ms = [stack(kt_s[rows, ln]).astype(BF16) for ln in lanes]
        b_ms = [stack(bt_s[rows, ln]).astype(BF16) for ln in lanes]
        v_ms = [stack(z).astype(BF16) for z in v]
        mm = [lax.dot_general(jnp.concatenate([r_ms[p], a_ms[p]], axis=0),
                              jnp.concatenate([k_ms[p], b_ms[p]], axis=0), dn_nt, preferred_element_type=F32)
              for p in pairs]
        m_rk = [jnp.where(incl, z[:2 * T, :2 * T], 0.0).astype(BF16) for z in mm]
        m_rb = [jnp.where(incl, z[:2 * T, 2 * T:], 0.0).astype(BF16) for z in mm]
        m_ak = [jnp.where(strict, z[2 * T:, :2 * T], 0.0).astype(BF16) for z in mm]
        nmat = [jnp.where(strict, z[2 * T:, 2 * T:], 0.0) for z in mm]
        x = [jnp.dot(jnp.concatenate([a_ms[p], m_ak[p]], axis=1), jnp.concatenate([st_b[p], v_ms[p]], axis=0),
                     preferred_element_type=F32) for p in pairs]
        for s in range(n_steps):
            nb = [z.astype(BF16) for z in nmat]
            if s < n_steps - 1:
                nx = [jnp.dot(nb[p], jnp.concatenate([nb[p], x[p].astype(BF16)], axis=1),
                              preferred_element_type=F32) for p in pairs]
                nmat = [z[:, :2 * T] for z in nx]
                x = [x[p] + nx[p][:, 2 * T:] for p in pairs]
            else:
                x = [x[p] + jnp.dot(nb[p], x[p].astype(BF16), preferred_element_type=F32) for p in pairs]
        y_stk = [jnp.dot(jnp.concatenate([r_ms[p], m_rk[p], m_rb[p]], axis=1),
                         jnp.concatenate([st_b[p], v_ms[p], x[p].astype(BF16)], axis=0),
                         preferred_element_type=F32) for p in pairs]
        w_end = [jnp.exp(-cm_s[rows_last, ln][last]) for ln in lanes]
        upd = []
        for p, ln in enumerate(lanes):
            u = x[p][:T] + x[p][T:]
            kbw_t = (jnp.concatenate([kt_s[rows, ln], bt_s[rows, ln]], axis=0) * w_end[p]).T
            upd.append(jnp.dot(kbw_t.astype(BF16), jnp.concatenate([v[p], u], axis=0).astype(BF16),
                               preferred_element_type=F32))
        for p, ln in enumerate(lanes):
            y_s[rows, ln] = y_stk[p][:T] + y_stk[p][T:]
            w_col = jnp.broadcast_to(w_end[p], (V7X_LANES, V7X_LANES)).T
            st_ref[p] = w_col * st[p] + jnp.where(head_blk, upd[p], 0.0)
        return carry

    lax.fori_loop(0, n_chunks, chunk_body, 0)

    inv_n = 1.0 / RW_HEAD_DIM
    for sg in range(LW // seg):
        ls = slice(sg * seg, (sg + 1) * seg)
        y = y_s[:, ls]
        yc = y - head_sum(y) * inv_n
        var = head_sum(yc * yc) * inv_n
        yn = yc * lax.rsqrt(var + RW_GN_EPS) * gw_ref[:, ls] + gb_ref[:, ls]
        o_ref[:, ls] = ((yn + bn_s[:, ls]) * g_ref[:, ls]).astype(o_ref.dtype)


def _wkv(r, k, v, e, a, g, k_k, k_a, r_k, gn_w, gn_b, *, B, C, L, rev):
    M, D = r.shape
    Lt = C + L
    TB, LW = WKV_BLOCK, min(WKV_LANES, D)
    assert C % TB == 0 and L % TB == 0 and D % LW == 0
    nb, ncb = Lt // TB, C // TB

    def tblk(b, i):
        if not rev:
            return b * nb + i
        return b * nb + jnp.where(i < ncb, ncb - 1 - i, nb - 1 - i + ncb)

    blk = pl.BlockSpec((TB, LW), lambda b, gi, i: (tblk(b, i), gi))
    vec = pl.BlockSpec((1, LW), lambda b, gi, i: (0, gi))
    tbuf = pltpu.VMEM((TB, LW), F32)
    return pl.pallas_call(
        functools.partial(_wkv_body, rev=rev), out_shape=jax.ShapeDtypeStruct((M, D), BF16),
        grid=(B, D // LW, nb), in_specs=[blk] * 6 + [vec] * 5, out_specs=blk,
        scratch_shapes=[pltpu.VMEM((LW // V7X_LANES, V7X_LANES, V7X_LANES), F32)] + [tbuf] * 7,
        compiler_params=_cparams("parallel", "parallel", "arbitrary"), name="wkv7_bwd" if rev else "wkv7_fwd",
    )(r, k, v, e, a, g, k_k.reshape(1, D), k_a.reshape(1, D), r_k.reshape(1, D), gn_w.reshape(1, D),
      gn_b.reshape(1, D))


def _token_shift(h, *, B, C, L):
    M, D = h.shape
    half = D // 2
    t = h.reshape(B, C + L, D)

    def one(z):
        prev = jnp.pad(z[:, :-1, :half], ((0, 0), (1, 0), (0, 0)))
        nxt = jnp.pad(z[:, 1:, half:], ((0, 0), (0, 1), (0, 0)))
        return jnp.concatenate([prev, nxt], axis=-1)

    return jnp.concatenate([one(t[:, :C]), one(t[:, C:])], axis=1).reshape(M, D)


def _lerp(t, ts, mix):
    tf = t.astype(F32)
    return tf + (ts.astype(F32) - tf) * mix


def _rwkv7_time_mix(h, p, *, B, C, L):
    M, D = h.shape
    ts = _token_shift(h, B=B, C=C, L=L)
    mix = p["mix"]
    proj = lambda j, w, nm: _matmul([h, ts], w, prologue=_lerp, prologue_vecs=[mix[j].reshape(1, D)], name=nm)
    r = proj(0, p["w_r"].astype(BF16), "rw_r")
    k = proj(2, p["w_k"].astype(BF16), "rw_k")
    v = proj(3, p["w_v"].astype(BF16), "rw_v")
    lw = p["w1"].shape[2]
    la = p["a1"].shape[2]
    lg = p["g1"].shape[2]
    lgp = -(-lg // V7X_LANES) * V7X_LANES
    w1c = jnp.concatenate([p["w1"][0], p["w1"][1]], axis=1).astype(BF16)
    a1c = jnp.concatenate([p["a1"][0], p["a1"][1]], axis=1).astype(BF16)
    g1p = jnp.pad(p["g1"], ((0, 0), (0, 0), (0, lgp - lg)))
    g1c = jnp.concatenate([g1p[0], g1p[1]], axis=1).astype(BF16)
    g2p = jnp.pad(p["g2"], ((0, 0), (0, lgp - lg), (0, 0))).astype(BF16)
    pw = proj(1, w1c, "rw_w1")
    pa = proj(4, a1c, "rw_a1")
    pg = proj(5, g1c, "rw_g1")

    def epi_decay(acc, w0, out_ref):
        out_ref[...] = jax.nn.sigmoid(acc + w0) * float(np.exp(-0.5))

    def epi_sigmoid(acc, a0, out_ref):
        out_ref[...] = jax.nn.sigmoid(acc + a0)

    o2 = []
    for d in range(2):
        e_d = _matmul([pw], p["w2"][d].astype(BF16), prologue=jnp.tanh, epilogue=epi_decay,
                      epi_vecs=[p["w0"][d].reshape(1, D)], a_col=d, name="rw_w2")
        a_d = _matmul([pa], p["a2"][d].astype(BF16), prologue=lambda z: z, epilogue=epi_sigmoid,
                      epi_vecs=[p["a0"][d].reshape(1, D)], a_col=d, name="rw_a2")
        g_d = _matmul([pg], g2p[d], prologue=jax.nn.sigmoid, a_col=d, name="rw_g2")
        o2.append(_wkv(r, k, v, e_d, a_d, g_d, p["k_k"], p["k_a"], p["r_k"], p["gn_w"], p["gn_b"],
                       B=B, C=C, L=L, rev=d == 1))
    return _matmul([o2[0], o2[1]], p["w_o"].astype(BF16),
                   prologue=lambda a, b: a.astype(F32) + b.astype(F32), name="rw_o")


def _router_params(router_w, router_b):
    D, E = router_w.shape
    rw = jnp.pad(router_w, ((0, 0), (0, V7X_LANES - E))).astype(BF16)
    rb = jnp.pad(router_b, (0, V7X_LANES - E)).reshape(1, V7X_LANES)
    return rw, rb


def kernel(x, c, ctx, c_ctx, l0_ada_w, l0_ada_b, l0_norm_mix, l0_norm_ffn, l0_na_w_qkv, l0_na_q_gain, l0_na_k_gain, l0_na_rpb, l0_na_w_o, l0_router_w, l0_router_b, l0_moe_w_gate, l0_moe_b_gate, l0_moe_w_up, l0_moe_b_up, l0_moe_w_down, l0_moe_b_down, l1_ada_w, l1_ada_b, l1_norm_mix, l1_norm_ffn, l1_rw_mix, l1_rw_w_r, l1_rw_w_k, l1_rw_w_v, l1_rw_k_k, l1_rw_k_a, l1_rw_r_k, l1_rw_w0, l1_rw_w1, l1_rw_w2, l1_rw_a0, l1_rw_a1, l1_rw_a2, l1_rw_g1, l1_rw_g2, l1_rw_gn_w, l1_rw_gn_b, l1_rw_w_o, l1_router_w, l1_router_b, l1_moe_w_gate, l1_moe_b_gate, l1_moe_w_up, l1_moe_b_up, l1_moe_w_down, l1_moe_b_down):
    B, L, D = x.shape
    C = ctx.shape[1]
    Lt = C + L
    M = B * Lt
    H = D // NA_HEAD_DIM
    E = l0_router_w.shape[1]
    assert C % ROW_TILE == 0 and L % ROW_TILE == 0 and L % GRID_W == 0
    tiles = dict(rows_per_batch=Lt, ctx_rows=C)
    xa = jnp.concatenate([ctx, x], axis=1).reshape(M, D)

    mod0 = _ada_mod(c, c_ctx, l0_ada_w, l0_ada_b)
    (h,) = _rows(xa, [], [], mod0, norm_w=l0_norm_mix, shift_row=0, scale_row=1, write_x=False,
                 name="l0_norm_mix", **tiles)
    scale = NA_HEAD_DIM ** -0.5
    gain = jnp.concatenate([jnp.tile(l0_na_q_gain * scale, H), jnp.tile(l0_na_k_gain, H), jnp.ones((D,), F32)])
    tn = 512 if D % 512 == 0 else D
    qkv = _matmul([h], l0_na_w_qkv.astype(BF16), out_dtypes=(BF16,),
                  epilogue=functools.partial(_qkv_epilogue, n_norm_tiles=2 * D // tn),
                  epi_vecs=[gain.reshape(1, 3 * D)], tn=tn, name="na_qkv")
    bias_tabs = _na_bias_tables(l0_na_rpb, L // GRID_W)
    att = _neighbourhood_attention(qkv, bias_tabs, B=B, C=C, L=L, D=D)
    y = _matmul([att], l0_na_w_o.astype(BF16), name="na_o")
    rw0, rb0 = _router_params(l0_router_w, l0_router_b)
    xa, h2, logits = _rows(xa, [y], [2], mod0, norm_w=l0_norm_ffn, shift_row=3, scale_row=4, h_dtype=F32,
                           router=(rw0, rb0), name="l0_norm_ffn", **tiles)
    ys = _moe_ffn(h2, jnp.arange(M, dtype=jnp.int32), logits[:, :E], l0_moe_w_gate.astype(BF16), l0_moe_b_gate,
                  l0_moe_w_up.astype(BF16), l0_moe_b_up, l0_moe_w_down.astype(BF16), l0_moe_b_down)
    (xa,) = _rows(xa, [ys] * TOP_K, [5] * TOP_K, mod0, add_row_offsets=[k * M for k in range(TOP_K)],
                  tr=ROW_TILE // 2, name="l0_moe_residual", **tiles)

    mod1 = _ada_mod(c, c_ctx, l1_ada_w, l1_ada_b)
    (h,) = _rows(xa, [], [], mod1, norm_w=l1_norm_mix, shift_row=0, scale_row=1, write_x=False,
                 name="l1_norm_mix", **tiles)
    rwp = dict(mix=l1_rw_mix, w_r=l1_rw_w_r, w_k=l1_rw_w_k, w_v=l1_rw_w_v, k_k=l1_rw_k_k, k_a=l1_rw_k_a,
               r_k=l1_rw_r_k, w0=l1_rw_w0, w1=l1_rw_w1, w2=l1_rw_w2, a0=l1_rw_a0, a1=l1_rw_a1, a2=l1_rw_a2,
               g1=l1_rw_g1, g2=l1_rw_g2, gn_w=l1_rw_gn_w, gn_b=l1_rw_gn_b, w_o=l1_rw_w_o)
    y = _rwkv7_time_mix(h, rwp, B=B, C=C, L=L)
    rw1, rb1 = _router_params(l1_router_w, l1_router_b)
    xa, h2, logits = _rows(xa, [y], [2], mod1, norm_w=l1_norm_ffn, shift_row=3, scale_row=4, h_dtype=F32,
                           router=(rw1, rb1), name="l1_norm_ffn", **tiles)
    lat_rows = (jnp.arange(B, dtype=jnp.int32)[:, None] * Lt + C + jnp.arange(L, dtype=jnp.int32)[None, :])
    lat_logits = logits.reshape(B, Lt, -1)[:, C:, :E].reshape(B * L, E)
    ys = _moe_ffn(h2, lat_rows.reshape(-1), lat_logits, l1_moe_w_gate.astype(BF16), l1_moe_b_gate,
                  l1_moe_w_up.astype(BF16), l1_moe_b_up, l1_moe_w_down.astype(BF16), l1_moe_b_down)
    (out,) = _rows(xa, [ys] * TOP_K, [5] * TOP_K, mod1, add_row_offsets=[k * B * L for k in range(TOP_K)],
                   rows_per_batch=L, ctx_rows=0, x_rows_per_batch=Lt, x_row_offset=C, tr=ROW_TILE // 2,
                   name="l1_moe_residual")
    return out.reshape(B, L, D)
```

```python
import functools

import numpy as np
import jax
import jax.numpy as jnp
from jax import lax
from jax.experimental import pallas as pl
from jax.experimental.pallas import tpu as pltpu

F32 = jnp.float32
BF16 = jnp.bfloat16

GRID_W = 64
N_MOD = 6
RMS_EPS = 1e-6
NEG_INF = -1e30
NA_HEAD_DIM = 128
NA_WIN_H = 8
NA_WIN_W = 16
RW_HEAD_DIM = 64
RW_GN_EPS = 64e-5
TOP_K = 4
SWIGLU_LIMIT = 7.0
SWIGLU_ALPHA = 1.702

V7X_LANES = 128
VMEM_LIMIT_BYTES = 56 * 1024 * 1024

ROW_TILE = 256
NA_QROWS = 4
NA_KROWS = 12
MOE_TILE = 256
MOE_DMA_PIECES = 4
WKV_T = 64
WKV_BLOCK = 256
WKV_LANES = 1024
WKV_SEG = 256


def _cparams(*sem, **kw):
    return pltpu.CompilerParams(dimension_semantics=sem, vmem_limit_bytes=VMEM_LIMIT_BYTES, **kw)


def _mm_body(*refs, n_a, n_pv, n_ev, n_em, n_out, prologue, epilogue, chunk):
    a_refs = refs[:n_a]
    pv_refs = refs[n_a:n_a + n_pv]
    w_ref = refs[n_a + n_pv]
    base = n_a + n_pv + 1
    ev_refs = refs[base:base + n_ev]
    em_refs = refs[base + n_ev:base + n_ev + n_em]
    out_refs = refs[base + n_ev + n_em:base + n_ev + n_em + n_out]
    scratch = refs[base + n_ev + n_em + n_out:]
    if prologue is not None:
        a_scr = scratch[0]

        @pl.when(pl.program_id(1) == 0)
        def _():
            pvs = [r[...] for r in pv_refs]

            def body(c, carry):
                rows = pl.ds(pl.multiple_of(c * chunk, chunk), chunk)
                a_scr[rows, :] = prologue(*[r[rows, :] for r in a_refs], *pvs).astype(BF16)
                return carry

            lax.fori_loop(0, a_scr.shape[0] // chunk, body, 0)

        a = a_scr[...]
    else:
        a = a_refs[0][...]
    acc = jnp.dot(a, w_ref[...].astype(BF16), preferred_element_type=F32)
    evs = [r[...] for r in ev_refs]
    ems = [r[...] for r in em_refs]
    if epilogue is None:
        out_refs[0][...] = acc.astype(out_refs[0].dtype)
    else:
        epilogue(acc, *evs, *ems, *out_refs)


def _matmul(a_list, w, *, out_dtypes=(F32,), prologue=None, prologue_vecs=(), epilogue=None,
            epi_vecs=(), epi_mats=(), a_col=0, tm=None, tn=None, name="mm"):
    M = a_list[0].shape[0]
    K, N = w.shape
    if tm is None:
        tm = 512 if M % 512 == 0 else (256 if M % 256 == 0 else M)
    if tn is None:
        tn = 512 if N % 512 == 0 else (256 if N % 256 == 0 else N)
    assert M % tm == 0 and N % tn == 0
    chunk = 32 if tm % 32 == 0 else tm
    in_specs = [pl.BlockSpec((tm, K), lambda m, n: (m, a_col)) for _ in a_list]
    in_specs += [pl.BlockSpec((1, K), lambda m, n: (0, 0)) for _ in prologue_vecs]
    in_specs += [pl.BlockSpec((K, tn), lambda m, n: (0, n))]
    in_specs += [pl.BlockSpec((1, tn), lambda m, n: (0, n)) for _ in epi_vecs]
    in_specs += [pl.BlockSpec((tm, tn), lambda m, n: (m, n)) for _ in epi_mats]
    out_shape = [jax.ShapeDtypeStruct((M, N), dt) for dt in out_dtypes]
    out_specs = [pl.BlockSpec((tm, tn), lambda m, n: (m, n)) for _ in out_dtypes]
    scratch = [pltpu.VMEM((tm, K), BF16)] if prologue is not None else []
    body = functools.partial(_mm_body, n_a=len(a_list), n_pv=len(prologue_vecs), n_ev=len(epi_vecs),
                             n_em=len(epi_mats), n_out=len(out_dtypes), prologue=prologue, epilogue=epilogue,
                             chunk=chunk)
    outs = pl.pallas_call(
        body, out_shape=out_shape, grid=(M // tm, N // tn), in_specs=in_specs, out_specs=out_specs,
        scratch_shapes=scratch, compiler_params=_cparams("parallel", "arbitrary"), name=name,
    )(*a_list, *prologue_vecs, w, *epi_vecs, *epi_mats)
    return outs[0] if len(out_dtypes) == 1 else outs


def _rows_body(*refs, n_add, gate_rows, ctx_src_tiles, do_norm, shift_row, scale_row, has_router, write_x, chunk):
    x_ref = refs[0]
    n_src = 1 if ctx_src_tiles is None else 2
    refs = refs[n_src - 1:]
    ctx_ref = refs[0]
    add_refs = refs[1:1 + n_add]
    mod_ref = refs[1 + n_add]
    pos = 2 + n_add
    if do_norm:
        nw_ref = refs[pos]
        pos += 1
    if has_router:
        rw_ref, rb_ref = refs[pos], refs[pos + 1]
        pos += 2
    outs = refs[pos:]
    oi = 0
    if write_x:
        xo_ref = outs[oi]
        oi += 1
    if do_norm:
        h_ref = outs[oi]
        oi += 1
    if has_router:
        lg_ref = outs[oi]
    gates = [mod_ref[0, g:g + 1, :] for g in gate_rows]
    if do_norm:
        nw = nw_ref[...]
        sh = mod_ref[0, shift_row:shift_row + 1, :]
        sc = 1.0 + mod_ref[0, scale_row:scale_row + 1, :]

    def run(src_ref):
        def body(c, carry):
            rows = pl.ds(pl.multiple_of(c * chunk, chunk), chunk)
            x = src_ref[rows, :]
            for a_ref, g in zip(add_refs, gates):
                x = x + g * a_ref[rows, :].astype(F32)
            if write_x:
                xo_ref[rows, :] = x
            if do_norm:
                y = x * lax.rsqrt(jnp.mean(x * x, axis=-1, keepdims=True) + RMS_EPS)
                h_ref[rows, :] = ((y * nw) * sc + sh).astype(h_ref.dtype)
            return carry

        lax.fori_loop(0, src_ref.shape[0] // chunk, body, 0)

    if ctx_src_tiles is None:
        run(x_ref)
    else:
        tiles_per_batch, ctx_tiles = ctx_src_tiles
        is_ctx = lax.rem(pl.program_id(0), tiles_per_batch) < ctx_tiles
        pl.when(is_ctx)(functools.partial(run, ctx_ref))
        pl.when(jnp.logical_not(is_ctx))(functools.partial(run, x_ref))
    if has_router:
        lg_ref[...] = jnp.dot(h_ref[...].astype(BF16), rw_ref[...], preferred_element_type=F32) + rb_ref[...]


def _rows(x, adds, gate_rows, mod, *, rows_per_batch, ctx_rows, x_rows_per_batch=None, x_row_offset=0, x_ctx=None,
          add_row_offsets=None, norm_w=None, shift_row=0, scale_row=1, h_dtype=None, router=None, write_x=True,
          tr=ROW_TILE, name="rows"):
    D = x.shape[1]
    B = mod.shape[0] // 2
    M = B * rows_per_batch
    x_rows_per_batch = x_rows_per_batch or rows_per_batch
    assert rows_per_batch % tr == 0 and ctx_rows % tr == 0 and x_rows_per_batch % tr == 0 and x_row_offset % tr == 0
    tiles_per_batch, ctx_tiles = rows_per_batch // tr, ctx_rows // tr
    x_tiles_per_batch, x_tile_offset = x_rows_per_batch // tr, x_row_offset // tr
    add_row_offsets = list(add_row_offsets or [0] * len(adds))
    assert all(off % tr == 0 for off in add_row_offsets)
    do_norm = norm_w is not None

    def mod_idx(i):
        return ((i // tiles_per_batch) * 2 + ((i % tiles_per_batch) >= ctx_tiles).astype(jnp.int32), 0, 0)

    row_spec = pl.BlockSpec((tr, D), lambda i: (i, 0))
    x_spec = pl.BlockSpec((tr, D), lambda i: ((i // tiles_per_batch) * x_tiles_per_batch + x_tile_offset
                                               + i % tiles_per_batch, 0))
    add_specs = [pl.BlockSpec((tr, D), lambda i, t=off // tr: (t + i, 0)) for off in add_row_offsets]
    src = [x]
    if x_ctx is not None:
        lat_tiles = tiles_per_batch - ctx_tiles
        x_spec = pl.BlockSpec((tr, D), lambda i: ((i // tiles_per_batch) * lat_tiles
                                                   + jnp.maximum(i % tiles_per_batch - ctx_tiles, 0), 0))
        ctx_spec = pl.BlockSpec((tr, D), lambda i: ((i // tiles_per_batch) * ctx_tiles
                                                     + jnp.minimum(i % tiles_per_batch, ctx_tiles - 1), 0))
        src = [x, x_ctx]
    args = [*src, *adds, mod]
    in_specs = [x_spec] + ([ctx_spec] if x_ctx is not None else []) + add_specs + [pl.BlockSpec((1, 8, D), mod_idx)]
    if do_norm:
        args.append(norm_w.reshape(1, D))
        in_specs.append(pl.BlockSpec((1, D), lambda i: (0, 0)))
    if router is not None:
        rw, rb = router
        args += [rw, rb]
        in_specs += [pl.BlockSpec(rw.shape, lambda i: (0, 0)), pl.BlockSpec(rb.shape, lambda i: (0, 0))]
    out_shape, out_specs = [], []
    if write_x:
        out_shape.append(jax.ShapeDtypeStruct((M, D), F32))
        out_specs.append(row_spec)
    if do_norm:
        out_shape.append(jax.ShapeDtypeStruct((M, D), h_dtype or BF16))
        out_specs.append(row_spec)
    if router is not None:
        out_shape.append(jax.ShapeDtypeStruct((M, V7X_LANES), F32))
        out_specs.append(pl.BlockSpec((tr, V7X_LANES), lambda i: (i, 0)))
    body = functools.partial(_rows_body, n_add=len(adds), gate_rows=tuple(gate_rows),
                             ctx_src_tiles=(tiles_per_batch, ctx_tiles) if x_ctx is not None else None,
                             do_norm=do_norm, shift_row=shift_row, scale_row=scale_row,
                             has_router=router is not None, write_x=write_x, chunk=32)
    return pl.pallas_call(body, out_shape=out_shape, grid=(M // tr,), in_specs=in_specs, out_specs=out_specs,
                          compiler_params=_cparams("parallel"), name=name)(*args)


def _ada_mod(c, c_ctx, ada_w, ada_b):
    B, D = c.shape
    cond = jnp.concatenate([c, c_ctx[None, :], jnp.zeros((8 - B - 1, D), F32)], axis=0)

    def epi(acc, bias, out_ref):
        out_ref[...] = acc + bias

    out = _matmul([cond], ada_w, prologue=lambda a: a * jax.nn.sigmoid(a), epilogue=epi,
                  epi_vecs=[ada_b.reshape(1, -1)], tm=8, tn=512, name="ada_mod")
    lat = out[:B].reshape(B, N_MOD, D)
    ctx = jnp.broadcast_to(out[B].reshape(1, N_MOD, D), (B, N_MOD, D))
    mod = jnp.stack([ctx, lat], axis=1).reshape(2 * B, N_MOD, D)
    return jnp.pad(mod, ((0, 0), (0, 8 - N_MOD), (0, 0)))


def _qkv_epilogue(acc, gain, out_ref, *, n_norm_tiles):
    tn = acc.shape[1]

    @pl.when(pl.program_id(1) < n_norm_tiles)
    def _():
        for i in range(tn // NA_HEAD_DIM):
            sl = slice(i * NA_HEAD_DIM, (i + 1) * NA_HEAD_DIM)
            z = acc[:, sl]
            z = z * lax.rsqrt(jnp.mean(z * z, axis=-1, keepdims=True) + RMS_EPS)
            out_ref[:, sl] = (z * gain[:, sl]).astype(out_ref.dtype)

    @pl.when(pl.program_id(1) >= n_norm_tiles)
    def _():
        out_ref[...] = acc.astype(out_ref.dtype)


def _na_bias_tables(rpb, rows):
    nblk = rows // NA_QROWS
    kh = min(NA_WIN_H, rows)
    assert kh == NA_WIN_H and rows >= NA_KROWS + NA_QROWS and rows % NA_QROWS == 0
    col = np.arange(GRID_W)
    col_start = np.clip(col - NA_WIN_W // 2, 0, GRID_W - NA_WIN_W)
    col_ok = (col[None, :] >= col_start[:, None]) & (col[None, :] < col_start[:, None] + NA_WIN_W)
    dc = np.clip(col[None, :] - col[:, None] + NA_WIN_W - 1, 0, 2 * NA_WIN_W - 2)
    tabs = []
    for j in (0, 1, nblk - 1):
        s0 = int(np.clip(NA_QROWS * j - NA_WIN_H // 2, 0, rows - NA_KROWS))
        qr = NA_QROWS * j + np.arange(NA_QROWS)
        rs = np.clip(qr - kh // 2, 0, rows - kh)
        kr = s0 + np.arange(NA_KROWS)
        row_ok = (kr[None, :] >= rs[:, None]) & (kr[None, :] < rs[:, None] + kh)
        dr = np.clip(kr[None, :] - qr[:, None] + NA_WIN_H - 1, 0, 2 * NA_WIN_H - 2)
        ok = row_ok[:, None, :, None] & col_ok[None, :, None, :]
        ok_f = ok.reshape(NA_QROWS * GRID_W, NA_KROWS * GRID_W)
        by_row = rpb.astype(F32)[:, dr.reshape(-1), :]
        spread = np.zeros((2 * NA_WIN_W - 1, GRID_W * GRID_W), np.float32)
        spread[dc.reshape(-1), np.arange(GRID_W * GRID_W)] = 1.0
        full = jnp.einsum("hpc,cq->hpq", by_row, jnp.asarray(spread), precision=lax.Precision.HIGHEST)
        full = full.reshape(-1, NA_QROWS, NA_KROWS, GRID_W, GRID_W).transpose(0, 1, 3, 2, 4)
        tabs.append(jnp.where(ok_f[None], full.reshape(-1, NA_QROWS * GRID_W, NA_KROWS * GRID_W), NEG_INF))
    return jnp.stack(tabs, axis=0)


def _na_body(q_ref, k_ref, v_ref, bias_ref, o_ref, *, C, rows):
    nq = NA_QROWS * GRID_W
    nk = NA_KROWS * GRID_W
    nblk = rows // NA_QROWS
    dn = (((1,), (1,)), ((), ()))
    kc = k_ref[0:C, :]
    vc = v_ref[0:C, :]
    s = lax.dot_general(q_ref[0:C, :], kc, dn, preferred_element_type=F32)
    p = jnp.exp(s - jnp.max(s, axis=-1, keepdims=True))
    o = jnp.dot(p.astype(BF16), vc, preferred_element_type=F32) / jnp.sum(p, axis=-1, keepdims=True)
    o_ref[0:C, :] = o.astype(o_ref.dtype)

    def body(j, carry):
        s0 = jnp.clip(NA_QROWS * j - NA_WIN_H // 2, 0, rows - NA_KROWS)
        cls = jnp.where(j == 0, 0, jnp.where(j == nblk - 1, 2, 1))
        qrows = pl.ds(pl.multiple_of(C + j * nq, GRID_W), nq)
        krows = pl.ds(pl.multiple_of(C + s0 * GRID_W, GRID_W), nk)
        qb = q_ref[qrows, :]
        kb = k_ref[krows, :]
        vb = v_ref[krows, :]
        s_lat = lax.dot_general(qb, kb, dn, preferred_element_type=F32) + bias_ref[cls, 0]
        s_ctx = lax.dot_general(qb, kc, dn, preferred_element_type=F32)
        m = jnp.maximum(jnp.max(s_lat, axis=-1, keepdims=True), jnp.max(s_ctx, axis=-1, keepdims=True))
        p_lat = jnp.exp(s_lat - m)
        p_ctx = jnp.exp(s_ctx - m)
        den = jnp.sum(p_lat, axis=-1, keepdims=True) + jnp.sum(p_ctx, axis=-1, keepdims=True)
        o = (jnp.dot(p_ctx.astype(BF16), vc, preferred_element_type=F32)
             + jnp.dot(p_lat.astype(BF16), vb, preferred_element_type=F32)) / den
        o_ref[qrows, :] = o.astype(o_ref.dtype)
        return carry

    lax.fori_loop(0, nblk, body, 0)


def _neighbourhood_attention(qkv, bias_tabs, *, B, C, L, D):
    H = D // NA_HEAD_DIM
    Lt = C + L
    rows = L // GRID_W
    blk = (Lt, NA_HEAD_DIM)
    body = functools.partial(_na_body, C=C, rows=rows)
    return pl.pallas_call(
        body, out_shape=jax.ShapeDtypeStruct((B * Lt, D), BF16), grid=(B, H),
        in_specs=[pl.BlockSpec(blk, lambda b, h: (b, h)),
                  pl.BlockSpec(blk, lambda b, h: (b, H + h)),
                  pl.BlockSpec(blk, lambda b, h: (b, 2 * H + h)),
                  pl.BlockSpec((3, 1) + bias_tabs.shape[2:], lambda b, h: (0, h, 0, 0))],
        out_specs=pl.BlockSpec(blk, lambda b, h: (b, h)),
        compiler_params=_cparams("parallel", "parallel"), name="na_attention",
    )(qkv, qkv, qkv, bias_tabs)


def _moe_body(be_ref, tok0_ref, tokn_ref, slotp_ref, slotc_ref, h_hbm, w_ref, wg_ref, bg_ref, wu_ref, bu_ref,
              wd_ref, bd_ref, ys_hbm, xbuf0, xbuf1, ybuf0, ybuf1, gsem, ssem):
    del be_ref
    tm = xbuf0.shape[0]
    i = pl.program_id(0)
    n_blocks = pl.num_programs(0)
    xbufs, ybufs = (xbuf0, xbuf1), (ybuf0, ybuf1)

    def start_gather(tok_ref, b, rows=None):
        for r in rows or range(tm):
            pltpu.make_async_copy(h_hbm.at[pl.ds(tok_ref[0, 0, r], 1)], xbufs[b].at[pl.ds(r, 1)],
                                  gsem.at[b]).start()

    def start_scatter(slot_ref, b, rows=None):
        for r in rows or range(tm):
            pltpu.make_async_copy(ybufs[b].at[pl.ds(r, 1)], ys_hbm.at[pl.ds(slot_ref[0, 0, r], 1)],
                                  ssem.at[b]).start()

    def wait_gather(b):
        pltpu.make_async_copy(h_hbm.at[pl.ds(0, tm)], xbufs[b], gsem.at[b]).wait()

    def wait_scatter(b):
        pltpu.make_async_copy(ybufs[b], ys_hbm.at[pl.ds(0, tm)], ssem.at[b]).wait()

    @pl.when(i == 0)
    def _():
        ybuf1[...] = jnp.zeros(ybuf1.shape, ybuf1.dtype)
        start_gather(tok0_ref, 0)

    def step(cur):
        wait_gather(cur)
        n_pc = MOE_DMA_PIECES
        D, De = wg_ref.shape[1], wg_ref.shape[2]
        kc, nc, rc = D // n_pc, D // n_pc, tm // (2 * n_pc)

        def issue(piece):
            rows = range(piece * rc, (piece + 1) * rc)
            start_gather(tokn_ref, 1 - cur, rows)
            start_scatter(slotp_ref, 1 - cur, rows)

        g = u = None
        for pc in range(n_pc):
            issue(pc)
            xk = xbufs[cur][:, pc * kc:(pc + 1) * kc].astype(BF16)
            gk = jnp.dot(xk, wg_ref[0, pc * kc:(pc + 1) * kc, :], preferred_element_type=F32)
            uk = jnp.dot(xk, wu_ref[0, pc * kc:(pc + 1) * kc, :], preferred_element_type=F32)
            g, u = (gk, uk) if g is None else (g + gk, u + uk)
        g = jnp.minimum(g + bg_ref[0], SWIGLU_LIMIT)
        u = jnp.clip(u + bu_ref[0], -SWIGLU_LIMIT, SWIGLU_LIMIT)
        hdn = (g * jax.nn.sigmoid(SWIGLU_ALPHA * g) * (u + 1.0)).astype(BF16)
        del De

        @pl.when(i >= 1)
        def _():
            wait_scatter(cur)

        w = w_ref[...]
        for pc in range(n_pc):
            issue(n_pc + pc)
            cols = slice(pc * nc, (pc + 1) * nc)
            y = jnp.dot(hdn, wd_ref[0, :, cols], preferred_element_type=F32) + bd_ref[0, :, cols]
            ybufs[cur][:, cols] = y * w

        @pl.when(i == n_blocks - 1)
        def _():
            start_scatter(slotc_ref, cur)
            wait_scatter(1 - cur)
            wait_scatter(cur)
            wait_gather(1 - cur)

    for parity in (0, 1):
        pl.when(lax.rem(i, 2) == parity)(functools.partial(step, parity))


def _moe_experts(h_all, row_src, row_slot, row_w, blk_exp, wg, bg, wu, bu, wd, bd):
    M, D = h_all.shape
    E, _, De = wg.shape
    tm = MOE_TILE
    n_rows = row_src.shape[0]
    n_blocks = n_rows // tm
    assert n_blocks >= 2
    smem_blk = lambda idx: pl.BlockSpec((1, 1, tm), idx, memory_space=pltpu.SMEM)
    grid_spec = pltpu.PrefetchScalarGridSpec(
        num_scalar_prefetch=1, grid=(n_blocks,),
        in_specs=[smem_blk(lambda i, be: (0, 0, 0)),
                  smem_blk(lambda i, be: (jnp.minimum(i + 1, n_blocks - 1), 0, 0)),
                  smem_blk(lambda i, be: (i, 0, 0)),
                  smem_blk(lambda i, be: (i + 1, 0, 0)),
                  pl.BlockSpec(memory_space=pl.ANY),
                  pl.BlockSpec((tm, 1), lambda i, be: (i, 0)),
                  pl.BlockSpec((1, D, De), lambda i, be: (be[i], 0, 0)),
                  pl.BlockSpec((1, 1, De), lambda i, be: (be[i], 0, 0)),
                  pl.BlockSpec((1, D, De), lambda i, be: (be[i], 0, 0)),
                  pl.BlockSpec((1, 1, De), lambda i, be: (be[i], 0, 0)),
                  pl.BlockSpec((1, De, D), lambda i, be: (be[i], 0, 0)),
                  pl.BlockSpec((1, 1, D), lambda i, be: (be[i], 0, 0))],
        out_specs=pl.BlockSpec(memory_space=pl.ANY),
        scratch_shapes=[pltpu.VMEM((tm, D), F32)] * 4 + [pltpu.SemaphoreType.DMA((2,))] * 2)
    src3 = row_src.reshape(n_blocks, 1, tm)
    spare = n_rows + jnp.arange(tm, dtype=jnp.int32)
    slot3 = jnp.concatenate([spare, row_slot]).reshape(n_blocks + 1, 1, tm)
    return pl.pallas_call(
        _moe_body, out_shape=jax.ShapeDtypeStruct((n_rows + tm, D), F32), grid_spec=grid_spec,
        compiler_params=_cparams("arbitrary", disable_bounds_checks=True), name="moe_experts",
    )(blk_exp, src3, src3, slot3, slot3, h_all, row_w.reshape(n_rows, 1), wg, bg.reshape(E, 1, De), wu,
      bu.reshape(E, 1, De), wd, bd.reshape(E, 1, D))


def _moe_ffn(h_all, tok_rows, logits, wg, bg, wu, bu, wd, bd):
    N = tok_rows.shape[0]
    E = wg.shape[0]
    tm = MOE_TILE
    top_val, top_idx = lax.top_k(logits, TOP_K)
    top_w = jax.nn.softmax(top_val, axis=-1)
    flat_e = top_idx.reshape(-1)
    onehot = (flat_e[:, None] == jnp.arange(E, dtype=flat_e.dtype)[None, :]).astype(jnp.int32)
    csum = jnp.cumsum(onehot, axis=0)
    counts = csum[-1]
    rank = jnp.take_along_axis(csum, flat_e[:, None], axis=1)[:, 0] - 1
    padded = (counts + tm - 1) // tm * tm
    pad_end = jnp.cumsum(padded)
    pad_start = pad_end - padded
    dest = pad_start[flat_e] + rank
    n_blocks = (N * TOP_K + tm - 1) // tm + E
    n_rows = n_blocks * tm
    row_src = jnp.zeros((n_rows,), jnp.int32).at[dest].set(jnp.repeat(tok_rows.astype(jnp.int32), TOP_K))
    row_w = jnp.zeros((n_rows,), F32).at[dest].set(top_w.reshape(-1))
    asg = jnp.arange(N * TOP_K, dtype=jnp.int32)
    row_slot = jnp.full((n_rows,), -1, jnp.int32).at[dest].set((asg % TOP_K) * N + asg // TOP_K)
    is_pad = row_slot < 0
    row_slot = jnp.where(is_pad, N * TOP_K - 1 + jnp.cumsum(is_pad.astype(jnp.int32)), row_slot)
    blk_start = jnp.arange(n_blocks, dtype=jnp.int32) * tm
    blk_exp = jnp.minimum(jnp.sum(blk_start[:, None] >= pad_end[None, :], axis=1), E - 1).astype(jnp.int32)
    return _moe_experts(h_all, row_src, row_slot, row_w, blk_exp, wg, bg, wu, bu, wd, bd)


def _block_id(idx, size):
    return lax.shift_right_logical(idx, int(np.log2(size)))


def _split2(x):
    hi = x.astype(BF16)
    lo = (x - hi.astype(F32)).astype(BF16)
    return hi, lo


def _dot01_lhs(m01, x):
    hi, lo = _split2(x)
    return jnp.dot(m01, hi, preferred_element_type=F32) + jnp.dot(m01, lo, preferred_element_type=F32)


def _wkv_body(r_ref, k_ref, v_ref, pw_ref, pa_ref, pg_ref, w2_ref, a2_ref, g2_ref, w0_ref, a0_ref, kk_ref, ka_ref,
              rk_ref, gw_ref, gb_ref, o_ref, st_ref, rt_s, kt_s, at_s, bt_s, cm_s, bn_s, y_s, *, rev):
    T = WKV_T
    TB, LW = r_ref.shape
    n_pairs = LW // V7X_LANES
    n_chunks = TB // T
    pw_act = jnp.tanh(pw_ref[...]).astype(BF16)
    pa_act = pa_ref[...].astype(BF16)
    pg_act = jax.nn.sigmoid(pg_ref[...]).astype(BF16)

    @pl.when(pl.program_id(2) == 0)
    def _():
        st_ref[...] = jnp.zeros_like(st_ref)

    def earlier(p, q, strict):
        lo, hi = (p, q) if rev else (q, p)
        return lo < hi if strict else lo <= hi

    seg = min(LW, WKV_SEG)
    ri = lax.broadcasted_iota(jnp.int32, (TB, TB), 0)
    ci = lax.broadcasted_iota(jnp.int32, (TB, TB), 1)
    same_chunk = _block_id(ri, T) == _block_id(ci, T)
    tri = jnp.where(same_chunk & earlier(ri, ci, False), 1.0, 0.0).astype(BF16)
    li = _block_id(lax.broadcasted_iota(jnp.int32, (seg, seg), 0), RW_HEAD_DIM)
    lj = _block_id(lax.broadcasted_iota(jnp.int32, (seg, seg), 1), RW_HEAD_DIM)
    head_ones = jnp.where(li == lj, 1.0, 0.0).astype(BF16)

    def head_sum(z):
        return jnp.dot(z.astype(BF16), head_ones, preferred_element_type=F32)

    for sg in range(LW // seg):
        ls = slice(sg * seg, (sg + 1) * seg)
        r = r_ref[:, ls]
        k = k_ref[:, ls]
        e = jax.nn.sigmoid(jnp.dot(pw_act, w2_ref[:, ls], preferred_element_type=F32) + w0_ref[:, ls]) \
            * float(np.exp(-0.5))
        a = jax.nn.sigmoid(jnp.dot(pa_act, a2_ref[:, ls], preferred_element_type=F32) + a0_ref[:, ls])
        kkv = k * kk_ref[:, ls]
        kk = kkv * lax.rsqrt(head_sum(kkv * kkv) + 1e-12)
        kd = k * (1.0 + (a - 1.0) * ka_ref[:, ls])
        be = kk * a
        cum = _dot01_lhs(tri, e)
        ecum = jnp.exp(cum)
        cm_s[:, ls] = cum
        rt_s[:, ls] = r * jnp.exp(-cum)
        kt_s[:, ls] = kd * ecum
        at_s[:, ls] = -kk * jnp.exp(e - cum)
        bt_s[:, ls] = be * ecum
        bn_s[:, ls] = head_sum(r * kd * rk_ref[:, ls]) * v_ref[:, ls]

    lane = lax.broadcasted_iota(jnp.int32, (1, V7X_LANES), 1)
    m_a = jnp.where(lane < RW_HEAD_DIM, 1.0, 0.0).astype(F32)
    m_b = 1.0 - m_a
    pr = lax.broadcasted_iota(jnp.int32, (2 * T, 2 * T), 0)
    pc = lax.broadcasted_iota(jnp.int32, (2 * T, 2 * T), 1)
    same_head = _block_id(pr, T) == _block_id(pc, T)
    incl = same_head & earlier(pr, pc, False)
    strict = same_head & earlier(pr, pc, True)
    hr = _block_id(lax.broadcasted_iota(jnp.int32, (V7X_LANES, V7X_LANES), 0), RW_HEAD_DIM)
    hc = _block_id(lax.broadcasted_iota(jnp.int32, (V7X_LANES, V7X_LANES), 1), RW_HEAD_DIM)
    head_blk = hr == hc
    dn_nt = (((1,), (1,)), ((), ()))
    dn_tn = (((0,), (0,)), ((), ()))
    n_steps = int(np.log2(T))

    def stack(x):
        return jnp.concatenate([x * m_a, x * m_b], axis=0)

    def chunk_body(it, carry):
        c = (n_chunks - 1 - it) if rev else it
        rows = pl.ds(pl.multiple_of(c * T, T), T)
        rows_last = pl.ds(pl.multiple_of(c * T + (0 if rev else T - 8), 8), 8)
        last = slice(0, 1) if rev else slice(7, 8)
        lanes = [slice(p * V7X_LANES, (p + 1) * V7X_LANES) for p in range(n_pairs)]
        pairs = range(n_pairs)
        st = [st_ref[p] for p in pairs]
        st_b = [z.astype(BF16) for z in st]
        v = [v_ref[rows, ln] for ln in lanes]
        r_ms = [stack(rt_s[rows, ln]).astype(BF16) for ln in lanes]
        a_ms = [stack(at_s[rows, ln]).astype(BF16) for ln in lanes]
        k_ms = [stack(kt_s[rows, ln]).astype(BF16) for ln in lanes]
        b_ms = [stack(bt_s[rows, ln]).astype(BF16) for ln in lanes]
        v_ms = [stack(z).astype(BF16) for z in v]
        mm = [lax.dot_general(jnp.concatenate([r_ms[p], a_ms[p]], axis=0),
                              jnp.concatenate([k_ms[p], b_ms[p]], axis=0), dn_nt, preferred_element_type=F32)
              for p in pairs]
        m_rk = [jnp.where(incl, z[:2 * T, :2 * T], 0.0).astype(BF16) for z in mm]
        m_rb = [jnp.where(incl, z[:2 * T, 2 * T:], 0.0).astype(BF16) for z in mm]
        m_ak = [jnp.where(strict, z[2 * T:, :2 * T], 0.0).astype(BF16) for z in mm]
        nmat = [jnp.where(strict, z[2 * T:, 2 * T:], 0.0) for z in mm]
        x = [jnp.dot(jnp.concatenate([a_ms[p], m_ak[p]], axis=1), jnp.concatenate([st_b[p], v_ms[p]], axis=0),
                     preferred_element_type=F32) for p in pairs]
        for s in range(n_steps):
            nb = [z.astype(BF16) for z in nmat]
            if s < n_steps - 1:
                nx = [jnp.dot(nb[p], jnp.concatenate([nb[p], x[p].astype(BF16)], axis=1),
                              preferred_element_type=F32) for p in pairs]
                nmat = [z[:, :2 * T] for z in nx]
                x = [x[p] + nx[p][:, 2 * T:] for p in pairs]
            else:
                x = [x[p] + jnp.dot(nb[p], x[p].astype(BF16), preferred_element_type=F32) for p in pairs]
        y_stk = [jnp.dot(jnp.concatenate([r_ms[p], m_rk[p], m_rb[p]], axis=1),
                         jnp.concatenate([st_b[p], v_ms[p], x[p].astype(BF16)], axis=0),
                         preferred_element_type=F32) for p in pairs]
        w_end = [jnp.exp(-cm_s[rows_last, ln][last]) for ln in lanes]
        upd = []
        for p, ln in enumerate(lanes):
            u = x[p][:T] + x[p][T:]
            kbw_t = (jnp.concatenate([kt_s[rows, ln], bt_s[rows, ln]], axis=0) * w_end[p]).T
            upd.append(jnp.dot(kbw_t.astype(BF16), jnp.concatenate([v[p], u], axis=0).astype(BF16),
                               preferred_element_type=F32))
        for p, ln in enumerate(lanes):
            y_s[rows, ln] = y_stk[p][:T] + y_stk[p][T:]
            w_col = jnp.broadcast_to(w_end[p], (V7X_LANES, V7X_LANES)).T
            st_ref[p] = w_col * st[p] + jnp.where(head_blk, upd[p], 0.0)
        return carry

    lax.fori_loop(0, n_chunks, chunk_body, 0)

    inv_n = 1.0 / RW_HEAD_DIM
    for sg in range(LW // seg):
        ls = slice(sg * seg, (sg + 1) * seg)
        y = y_s[:, ls]
        yc = y - head_sum(y) * inv_n
        var = head_sum(yc * yc) * inv_n
        yn = yc * lax.rsqrt(var + RW_GN_EPS) * gw_ref[:, ls] + gb_ref[:, ls]
        gate = jnp.dot(pg_act, g2_ref[:, ls], preferred_element_type=F32)
        o_ref[:, ls] = ((yn + bn_s[:, ls]) * gate).astype(o_ref.dtype)


def _wkv(r, k, v, pw, pa, pg, w2, a2, g2, w0, a0, k_k, k_a, r_k, gn_w, gn_b, *, B, C, L, d):
    M, D = r.shape
    Lt = C + L
    rev = d == 1
    TB, LW = WKV_BLOCK, min(WKV_LANES, D)
    assert C % TB == 0 and L % TB == 0 and D % LW == 0
    nb, ncb = Lt // TB, C // TB

    def tblk(b, i):
        if not rev:
            return b * nb + i
        return b * nb + jnp.where(i < ncb, ncb - 1 - i, nb - 1 - i + ncb)

    blk = pl.BlockSpec((TB, LW), lambda b, gi, i: (tblk(b, i), gi))
    vec = pl.BlockSpec((1, LW), lambda b, gi, i: (0, gi))
    low = lambda arr, w: pl.BlockSpec((TB, w.shape[0]), lambda b, gi, i: (tblk(b, i), d))
    up = lambda w: pl.BlockSpec((w.shape[0], LW), lambda b, gi, i: (0, gi))
    tbuf = pltpu.VMEM((TB, LW), F32)
    row = lambda z: z.reshape(1, D)
    return pl.pallas_call(
        functools.partial(_wkv_body, rev=rev), out_shape=jax.ShapeDtypeStruct((M, D), BF16),
        grid=(B, D // LW, nb),
        in_specs=[blk] * 3 + [low(pw, w2), low(pa, a2), low(pg, g2), up(w2), up(a2), up(g2)] + [vec] * 7,
        out_specs=blk,
        scratch_shapes=[pltpu.VMEM((LW // V7X_LANES, V7X_LANES, V7X_LANES), F32)] + [tbuf] * 7,
        compiler_params=_cparams("parallel", "parallel", "arbitrary"), name="wkv7_bwd" if rev else "wkv7_fwd",
    )(r, k, v, pw, pa, pg, w2, a2, g2, row(w0), row(a0), row(k_k), row(k_a), row(r_k), row(gn_w), row(gn_b))


def _token_shift(h, *, B, C, L):
    M, D = h.shape
    half = D // 2
    t = h.reshape(B, C + L, D)

    def one(z):
        prev = jnp.pad(z[:, :-1, :half], ((0, 0), (1, 0), (0, 0)))
        nxt = jnp.pad(z[:, 1:, half:], ((0, 0), (0, 1), (0, 0)))
        return jnp.concatenate([prev, nxt], axis=-1)

    return jnp.concatenate([one(t[:, :C]), one(t[:, C:])], axis=1).reshape(M, D)


def _lerp(t, ts, mix):
    tf = t.astype(F32)
    return tf + (ts.astype(F32) - tf) * mix


def _rwkv7_time_mix(h, p, *, B, C, L):
    M, D = h.shape
    ts = _token_shift(h, B=B, C=C, L=L)
    mix = p["mix"]
    proj = lambda j, w, nm: _matmul([h, ts], w, prologue=_lerp, prologue_vecs=[mix[j].reshape(1, D)], name=nm)
    r = proj(0, p["w_r"].astype(BF16), "rw_r")
    k = proj(2, p["w_k"].astype(BF16), "rw_k")
    v = proj(3, p["w_v"].astype(BF16), "rw_v")
    lg = p["g1"].shape[2]
    lgp = -(-lg // V7X_LANES) * V7X_LANES
    w1c = jnp.concatenate([p["w1"][0], p["w1"][1]], axis=1).astype(BF16)
    a1c = jnp.concatenate([p["a1"][0], p["a1"][1]], axis=1).astype(BF16)
    g1p = jnp.pad(p["g1"], ((0, 0), (0, 0), (0, lgp - lg)))
    g1c = jnp.concatenate([g1p[0], g1p[1]], axis=1).astype(BF16)
    g2p = jnp.pad(p["g2"], ((0, 0), (0, lgp - lg), (0, 0))).astype(BF16)
    pw = proj(1, w1c, "rw_w1")
    pa = proj(4, a1c, "rw_a1")
    pg = proj(5, g1c, "rw_g1")

    o2 = [_wkv(r, k, v, pw, pa, pg, p["w2"][d].astype(BF16), p["a2"][d].astype(BF16), g2p[d], p["w0"][d],
               p["a0"][d], p["k_k"], p["k_a"], p["r_k"], p["gn_w"], p["gn_b"], B=B, C=C, L=L, d=d)
          for d in range(2)]
    return _matmul([o2[0], o2[1]], p["w_o"].astype(BF16),
                   prologue=lambda a, b: a.astype(F32) + b.astype(F32), name="rw_o")


def _router_params(router_w, router_b):
    D, E = router_w.shape
    rw = jnp.pad(router_w, ((0, 0), (0, V7X_LANES - E))).astype(BF16)
    rb = jnp.pad(router_b, (0, V7X_LANES - E)).reshape(1, V7X_LANES)
    return rw, rb


def kernel(x, c, ctx, c_ctx, l0_ada_w, l0_ada_b, l0_norm_mix, l0_norm_ffn, l0_na_w_qkv, l0_na_q_gain, l0_na_k_gain, l0_na_rpb, l0_na_w_o, l0_router_w, l0_router_b, l0_moe_w_gate, l0_moe_b_gate, l0_moe_w_up, l0_moe_b_up, l0_moe_w_down, l0_moe_b_down, l1_ada_w, l1_ada_b, l1_norm_mix, l1_norm_ffn, l1_rw_mix, l1_rw_w_r, l1_rw_w_k, l1_rw_w_v, l1_rw_k_k, l1_rw_k_a, l1_rw_r_k, l1_rw_w0, l1_rw_w1, l1_rw_w2, l1_rw_a0, l1_rw_a1, l1_rw_a2, l1_rw_g1, l1_rw_g2, l1_rw_gn_w, l1_rw_gn_b, l1_rw_w_o, l1_router_w, l1_router_b, l1_moe_w_gate, l1_moe_b_gate, l1_moe_w_up, l1_moe_b_up, l1_moe_w_down, l1_moe_b_down):
    B, L, D = x.shape
    C = ctx.shape[1]
    Lt = C + L
    M = B * Lt
    H = D // NA_HEAD_DIM
    E = l0_router_w.shape[1]
    assert C % ROW_TILE == 0 and L % ROW_TILE == 0 and L % GRID_W == 0
    tiles = dict(rows_per_batch=Lt, ctx_rows=C)

    mod0 = _ada_mod(c, c_ctx, l0_ada_w, l0_ada_b)
    xa, h = _rows(x.reshape(B * L, D), [], [], mod0, x_ctx=ctx.reshape(B * C, D), norm_w=l0_norm_mix, shift_row=0,
                  scale_row=1, name="l0_norm_mix", **tiles)
    scale = NA_HEAD_DIM ** -0.5
    gain = jnp.concatenate([jnp.tile(l0_na_q_gain * scale, H), jnp.tile(l0_na_k_gain, H), jnp.ones((D,), F32)])
    tn = 512 if D % 512 == 0 else D
    qkv = _matmul([h], l0_na_w_qkv.astype(BF16), out_dtypes=(BF16,),
                  epilogue=functools.partial(_qkv_epilogue, n_norm_tiles=2 * D // tn),
                  epi_vecs=[gain.reshape(1, 3 * D)], tn=tn, name="na_qkv")
    bias_tabs = _na_bias_tables(l0_na_rpb, L // GRID_W)
    att = _neighbourhood_attention(qkv, bias_tabs, B=B, C=C, L=L, D=D)
    y = _matmul([att], l0_na_w_o.astype(BF16), name="na_o")
    rw0, rb0 = _router_params(l0_router_w, l0_router_b)
    xa, h2, logits = _rows(xa, [y], [2], mod0, norm_w=l0_norm_ffn, shift_row=3, scale_row=4, h_dtype=F32,
                           router=(rw0, rb0), name="l0_norm_ffn", **tiles)
    ys = _moe_ffn(h2, jnp.arange(M, dtype=jnp.int32), logits[:, :E], l0_moe_w_gate.astype(BF16), l0_moe_b_gate,
                  l0_moe_w_up.astype(BF16), l0_moe_b_up, l0_moe_w_down.astype(BF16), l0_moe_b_down)
    (xa,) = _rows(xa, [ys] * TOP_K, [5] * TOP_K, mod0, add_row_offsets=[k * M for k in range(TOP_K)],
                  tr=ROW_TILE // 2, name="l0_moe_residual", **tiles)

    mod1 = _ada_mod(c, c_ctx, l1_ada_w, l1_ada_b)
    (h,) = _rows(xa, [], [], mod1, norm_w=l1_norm_mix, shift_row=0, scale_row=1, write_x=False,
                 name="l1_norm_mix", **tiles)
    rwp = dict(mix=l1_rw_mix, w_r=l1_rw_w_r, w_k=l1_rw_w_k, w_v=l1_rw_w_v, k_k=l1_rw_k_k, k_a=l1_rw_k_a,
               r_k=l1_rw_r_k, w0=l1_rw_w0, w1=l1_rw_w1, w2=l1_rw_w2, a0=l1_rw_a0, a1=l1_rw_a1, a2=l1_rw_a2,
               g1=l1_rw_g1, g2=l1_rw_g2, gn_w=l1_rw_gn_w, gn_b=l1_rw_gn_b, w_o=l1_rw_w_o)
    y = _rwkv7_time_mix(h, rwp, B=B, C=C, L=L)
    rw1, rb1 = _router_params(l1_router_w, l1_router_b)
    xa, h2, logits = _rows(xa, [y], [2], mod1, norm_w=l1_norm_ffn, shift_row=3, scale_row=4, h_dtype=F32,
                           router=(rw1, rb1), name="l1_norm_ffn", **tiles)
    lat_rows = (jnp.arange(B, dtype=jnp.int32)[:, None] * Lt + C + jnp.arange(L, dtype=jnp.int32)[None, :])
    lat_logits = logits.reshape(B, Lt, -1)[:, C:, :E].reshape(B * L, E)
    ys = _moe_ffn(h2, lat_rows.reshape(-1), lat_logits, l1_moe_w_gate.astype(BF16), l1_moe_b_gate,
                  l1_moe_w_up.astype(BF16), l1_moe_b_up, l1_moe_w_down.astype(BF16), l1_moe_b_down)
    (out,) = _rows(xa, [ys] * TOP_K, [5] * TOP_K, mod1, add_row_offsets=[k * B * L for k in range(TOP_K)],
                   rows_per_batch=L, ctx_rows=0, x_rows_per_batch=Lt, x_row_offset=C, tr=ROW_TILE // 2,
                   name="l1_moe_residual")
    return out.reshape(B, L, D)
```

```python
import functools

import numpy as np
import jax
import jax.numpy as jnp
from jax import lax
from jax.experimental import pallas as pl
from jax.experimental.pallas import tpu as pltpu

F32 = jnp.float32
BF16 = jnp.bfloat16

GRID_W = 64
N_MOD = 6
RMS_EPS = 1e-6
NEG_INF = -1e30
NA_HEAD_DIM = 128
NA_WIN_H = 8
NA_WIN_W = 16
RW_HEAD_DIM = 64
RW_GN_EPS = 64e-5
TOP_K = 4
SWIGLU_LIMIT = 7.0
SWIGLU_ALPHA = 1.702

V7X_LANES = 128
VMEM_LIMIT_BYTES = 56 * 1024 * 1024

ROW_TILE = 256
NA_QROWS = 4
NA_KROWS = 12
MOE_TILE = 256
WKV_T = 64
WKV_BLOCK = 256
WKV_LANES = 1024
WKV_SEG = 256


def _cparams(*sem, **kw):
    return pltpu.CompilerParams(dimension_semantics=sem, vmem_limit_bytes=VMEM_LIMIT_BYTES, **kw)


def _mm_body(*refs, n_a, n_pv, n_ev, n_em, n_out, prologue, epilogue, chunk):
    a_refs = refs[:n_a]
    pv_refs = refs[n_a:n_a + n_pv]
    w_ref = refs[n_a + n_pv]
    base = n_a + n_pv + 1
    ev_refs = refs[base:base + n_ev]
    em_refs = refs[base + n_ev:base + n_ev + n_em]
    out_refs = refs[base + n_ev + n_em:base + n_ev + n_em + n_out]
    scratch = refs[base + n_ev + n_em + n_out:]
    if prologue is not None:
        a_scr = scratch[0]

        @pl.when(pl.program_id(1) == 0)
        def _():
            pvs = [r[...] for r in pv_refs]

            def body(c, carry):
                rows = pl.ds(pl.multiple_of(c * chunk, chunk), chunk)
                a_scr[rows, :] = prologue(*[r[rows, :] for r in a_refs], *pvs).astype(BF16)
                return carry

            lax.fori_loop(0, a_scr.shape[0] // chunk, body, 0)

        a = a_scr[...]
    else:
        a = a_refs[0][...]
    acc = jnp.dot(a, w_ref[...].astype(BF16), preferred_element_type=F32)
    evs = [r[...] for r in ev_refs]
    ems = [r[...] for r in em_refs]
    if epilogue is None:
        out_refs[0][...] = acc.astype(out_refs[0].dtype)
    else:
        epilogue(acc, *evs, *ems, *out_refs)


def _matmul(a_list, w, *, out_dtypes=(F32,), prologue=None, prologue_vecs=(), epilogue=None,
            epi_vecs=(), epi_mats=(), a_col=0, tm=None, tn=None, name="mm"):
    M = a_list[0].shape[0]
    K, N = w.shape
    if tm is None:
        tm = 512 if M % 512 == 0 else (256 if M % 256 == 0 else M)
    if tn is None:
        tn = 512 if N % 512 == 0 else (256 if N % 256 == 0 else N)
    assert M % tm == 0 and N % tn == 0
    chunk = 32 if tm % 32 == 0 else tm
    in_specs = [pl.BlockSpec((tm, K), lambda m, n: (m, a_col)) for _ in a_list]
    in_specs += [pl.BlockSpec((1, K), lambda m, n: (0, 0)) for _ in prologue_vecs]
    in_specs += [pl.BlockSpec((K, tn), lambda m, n: (0, n))]
    in_specs += [pl.BlockSpec((1, tn), lambda m, n: (0, n)) for _ in epi_vecs]
    in_specs += [pl.BlockSpec((tm, tn), lambda m, n: (m, n)) for _ in epi_mats]
    out_shape = [jax.ShapeDtypeStruct((M, N), dt) for dt in out_dtypes]
    out_specs = [pl.BlockSpec((tm, tn), lambda m, n: (m, n)) for _ in out_dtypes]
    scratch = [pltpu.VMEM((tm, K), BF16)] if prologue is not None else []
    body = functools.partial(_mm_body, n_a=len(a_list), n_pv=len(prologue_vecs), n_ev=len(epi_vecs),
                             n_em=len(epi_mats), n_out=len(out_dtypes), prologue=prologue, epilogue=epilogue,
                             chunk=chunk)
    outs = pl.pallas_call(
        body, out_shape=out_shape, grid=(M // tm, N // tn), in_specs=in_specs, out_specs=out_specs,
        scratch_shapes=scratch, compiler_params=_cparams("parallel", "arbitrary"), name=name,
    )(*a_list, *prologue_vecs, w, *epi_vecs, *epi_mats)
    return outs[0] if len(out_dtypes) == 1 else outs


def _rows_body(*refs, n_add, gate_rows, ctx_src_tiles, do_norm, shift_row, scale_row, has_router, write_x, chunk):
    x_ref = refs[0]
    n_src = 1 if ctx_src_tiles is None else 2
    refs = refs[n_src - 1:]
    ctx_ref = refs[0]
    add_refs = refs[1:1 + n_add]
    mod_ref = refs[1 + n_add]
    pos = 2 + n_add
    if do_norm:
        nw_ref = refs[pos]
        pos += 1
    if has_router:
        rw_ref, rb_ref = refs[pos], refs[pos + 1]
        pos += 2
    outs = refs[pos:]
    oi = 0
    if write_x:
        xo_ref = outs[oi]
        oi += 1
    if do_norm:
        h_ref = outs[oi]
        oi += 1
    if has_router:
        lg_ref = outs[oi]
    gates = [mod_ref[0, g:g + 1, :] for g in gate_rows]
    if do_norm:
        nw = nw_ref[...]
        sh = mod_ref[0, shift_row:shift_row + 1, :]
        sc = 1.0 + mod_ref[0, scale_row:scale_row + 1, :]

    def run(src_ref):
        def body(c, carry):
            rows = pl.ds(pl.multiple_of(c * chunk, chunk), chunk)
            x = src_ref[rows, :]
            for a_ref, g in zip(add_refs, gates):
                x = x + g * a_ref[rows, :].astype(F32)
            if write_x:
                xo_ref[rows, :] = x
            if do_norm:
                y = x * lax.rsqrt(jnp.mean(x * x, axis=-1, keepdims=True) + RMS_EPS)
                h_ref[rows, :] = ((y * nw) * sc + sh).astype(h_ref.dtype)
            return carry

        lax.fori_loop(0, src_ref.shape[0] // chunk, body, 0)

    if ctx_src_tiles is None:
        run(x_ref)
    else:
        tiles_per_batch, ctx_tiles = ctx_src_tiles
        is_ctx = lax.rem(pl.program_id(0), tiles_per_batch) < ctx_tiles
        pl.when(is_ctx)(functools.partial(run, ctx_ref))
        pl.when(jnp.logical_not(is_ctx))(functools.partial(run, x_ref))
    if has_router:
        lg_ref[...] = jnp.dot(h_ref[...].astype(BF16), rw_ref[...], preferred_element_type=F32) + rb_ref[...]


def _rows(x, adds, gate_rows, mod, *, rows_per_batch, ctx_rows, x_rows_per_batch=None, x_row_offset=0, x_ctx=None,
          add_row_offsets=None, norm_w=None, shift_row=0, scale_row=1, h_dtype=None, router=None, write_x=True,
          tr=ROW_TILE, name="rows"):
    D = x.shape[1]
    B = mod.shape[0] // 2
    M = B * rows_per_batch
    x_rows_per_batch = x_rows_per_batch or rows_per_batch
    assert rows_per_batch % tr == 0 and ctx_rows % tr == 0 and x_rows_per_batch % tr == 0 and x_row_offset % tr == 0
    tiles_per_batch, ctx_tiles = rows_per_batch // tr, ctx_rows // tr
    x_tiles_per_batch, x_tile_offset = x_rows_per_batch // tr, x_row_offset // tr
    add_row_offsets = list(add_row_offsets or [0] * len(adds))
    assert all(off % tr == 0 for off in add_row_offsets)
    do_norm = norm_w is not None

    def mod_idx(i):
        return ((i // tiles_per_batch) * 2 + ((i % tiles_per_batch) >= ctx_tiles).astype(jnp.int32), 0, 0)

    row_spec = pl.BlockSpec((tr, D), lambda i: (i, 0))
    x_spec = pl.BlockSpec((tr, D), lambda i: ((i // tiles_per_batch) * x_tiles_per_batch + x_tile_offset
                                               + i % tiles_per_batch, 0))
    add_specs = [pl.BlockSpec((tr, D), lambda i, t=off // tr: (t + i, 0)) for off in add_row_offsets]
    src = [x]
    if x_ctx is not None:
        lat_tiles = tiles_per_batch - ctx_tiles
        x_spec = pl.BlockSpec((tr, D), lambda i: ((i // tiles_per_batch) * lat_tiles
                                                   + jnp.maximum(i % tiles_per_batch - ctx_tiles, 0), 0))
        ctx_spec = pl.BlockSpec((tr, D), lambda i: ((i // tiles_per_batch) * ctx_tiles
                                                     + jnp.minimum(i % tiles_per_batch, ctx_tiles - 1), 0))
        src = [x, x_ctx]
    args = [*src, *adds, mod]
    in_specs = [x_spec] + ([ctx_spec] if x_ctx is not None else []) + add_specs + [pl.BlockSpec((1, 8, D), mod_idx)]
    if do_norm:
        args.append(norm_w.reshape(1, D))
        in_specs.append(pl.BlockSpec((1, D), lambda i: (0, 0)))
    if router is not None:
        rw, rb = router
        args += [rw, rb]
        in_specs += [pl.BlockSpec(rw.shape, lambda i: (0, 0)), pl.BlockSpec(rb.shape, lambda i: (0, 0))]
    out_shape, out_specs = [], []
    if write_x:
        out_shape.append(jax.ShapeDtypeStruct((M, D), F32))
        out_specs.append(row_spec)
    if do_norm:
        out_shape.append(jax.ShapeDtypeStruct((M, D), h_dtype or BF16))
        out_specs.append(row_spec)
    if router is not None:
        out_shape.append(jax.ShapeDtypeStruct((M, V7X_LANES), F32))
        out_specs.append(pl.BlockSpec((tr, V7X_LANES), lambda i: (i, 0)))
    body = functools.partial(_rows_body, n_add=len(adds), gate_rows=tuple(gate_rows),
                             ctx_src_tiles=(tiles_per_batch, ctx_tiles) if x_ctx is not None else None,
                             do_norm=do_norm, shift_row=shift_row, scale_row=scale_row,
                             has_router=router is not None, write_x=write_x, chunk=32)
    return pl.pallas_call(body, out_shape=out_shape, grid=(M // tr,), in_specs=in_specs, out_specs=out_specs,
                          compiler_params=_cparams("parallel"), name=name)(*args)


def _norm_shift_body(x_ref, xp_ref, xn_ref, mod_ref, nw_ref, h_ref, ts_ref, hs, *, tiles_per_batch, ctx_tiles, chunk):
    tr, D = x_ref.shape
    half = D // 2
    nw = nw_ref[...]
    sh = mod_ref[0, 0:1, :]
    sc = 1.0 + mod_ref[0, 1:2, :]

    def norm(x):
        return ((x * lax.rsqrt(jnp.mean(x * x, axis=-1, keepdims=True) + RMS_EPS)) * nw) * sc + sh

    j = lax.rem(pl.program_id(0), tiles_per_batch)
    first = jnp.logical_or(j == 0, j == ctx_tiles)
    last = jnp.logical_or(j == ctx_tiles - 1, j == tiles_per_batch - 1)
    hs[0:8, :] = jnp.where(first, 0.0, norm(xp_ref[...]))
    hs[tr + 8:tr + 16, :] = jnp.where(last, 0.0, norm(xn_ref[...]))

    def norm_rows(c, carry):
        rows = pl.ds(pl.multiple_of(c * chunk, chunk), chunk)
        h = norm(x_ref[rows, :])
        h_ref[rows, :] = h.astype(h_ref.dtype)
        hs[pl.ds(pl.multiple_of(c * chunk + 8, 8), chunk), :] = h
        return carry

    lax.fori_loop(0, tr // chunk, norm_rows, 0)

    def shift_rows(c, carry):
        win = pl.ds(pl.multiple_of(c * chunk, 8), chunk + 16)
        prev = pltpu.roll(hs[win, :half], 1, axis=0)[8:8 + chunk]
        nxt = pltpu.roll(hs[win, half:], chunk + 15, axis=0)[8:8 + chunk]
        rows = pl.ds(pl.multiple_of(c * chunk, chunk), chunk)
        ts_ref[rows, :] = jnp.concatenate([prev, nxt], axis=1).astype(ts_ref.dtype)
        return carry

    lax.fori_loop(0, tr // chunk, shift_rows, 0)


def _norm_shift(x, mod, norm_w, *, rows_per_batch, ctx_rows):
    M, D = x.shape
    tr = ROW_TILE
    tiles_per_batch, ctx_tiles = rows_per_batch // tr, ctx_rows // tr
    assert rows_per_batch % tr == 0 and ctx_rows % tr == 0 and ctx_tiles >= 1
    sub = tr // 8

    def mod_idx(i):
        return ((i // tiles_per_batch) * 2 + ((i % tiles_per_batch) >= ctx_tiles).astype(jnp.int32), 0, 0)

    row_spec = pl.BlockSpec((tr, D), lambda i: (i, 0))
    body = functools.partial(_norm_shift_body, tiles_per_batch=tiles_per_batch, ctx_tiles=ctx_tiles, chunk=32)
    return pl.pallas_call(
        body, out_shape=[jax.ShapeDtypeStruct((M, D), BF16)] * 2, grid=(M // tr,),
        in_specs=[row_spec,
                  pl.BlockSpec((8, D), lambda i: (jnp.maximum(i * sub - 1, 0), 0)),
                  pl.BlockSpec((8, D), lambda i: (jnp.minimum((i + 1) * sub, M // 8 - 1), 0)),
                  pl.BlockSpec((1, 8, D), mod_idx), pl.BlockSpec((1, D), lambda i: (0, 0))],
        out_specs=[row_spec, row_spec], scratch_shapes=[pltpu.VMEM((tr + 16, D), F32)],
        compiler_params=_cparams("parallel"), name="norm_shift")(x, x, x, mod, norm_w.reshape(1, D))


def _ada_mod(c, c_ctx, ada_w, ada_b):
    B, D = c.shape
    cond = jnp.concatenate([c, c_ctx[None, :], jnp.zeros((8 - B - 1, D), F32)], axis=0)

    def epi(acc, bias, out_ref):
        out_ref[...] = acc + bias

    out = _matmul([cond], ada_w, prologue=lambda a: a * jax.nn.sigmoid(a), epilogue=epi,
                  epi_vecs=[ada_b.reshape(1, -1)], tm=8, tn=512, name="ada_mod")
    lat = out[:B].reshape(B, N_MOD, D)
    ctx = jnp.broadcast_to(out[B].reshape(1, N_MOD, D), (B, N_MOD, D))
    mod = jnp.stack([ctx, lat], axis=1).reshape(2 * B, N_MOD, D)
    return jnp.pad(mod, ((0, 0), (0, 8 - N_MOD), (0, 0)))


def _qkv_epilogue(acc, gain, out_ref, *, n_norm_tiles):
    tn = acc.shape[1]

    @pl.when(pl.program_id(1) < n_norm_tiles)
    def _():
        for i in range(tn // NA_HEAD_DIM):
            sl = slice(i * NA_HEAD_DIM, (i + 1) * NA_HEAD_DIM)
            z = acc[:, sl]
            z = z * lax.rsqrt(jnp.mean(z * z, axis=-1, keepdims=True) + RMS_EPS)
            out_ref[:, sl] = (z * gain[:, sl]).astype(out_ref.dtype)

    @pl.when(pl.program_id(1) >= n_norm_tiles)
    def _():
        out_ref[...] = acc.astype(out_ref.dtype)


def _na_bias_tables(rpb, rows):
    nblk = rows // NA_QROWS
    kh = min(NA_WIN_H, rows)
    assert kh == NA_WIN_H and rows >= NA_KROWS + NA_QROWS and rows % NA_QROWS == 0
    col = np.arange(GRID_W)
    col_start = np.clip(col - NA_WIN_W // 2, 0, GRID_W - NA_WIN_W)
    col_ok = (col[None, :] >= col_start[:, None]) & (col[None, :] < col_start[:, None] + NA_WIN_W)
    dc = np.clip(col[None, :] - col[:, None] + NA_WIN_W - 1, 0, 2 * NA_WIN_W - 2)
    tabs = []
    for j in (0, 1, nblk - 1):
        s0 = int(np.clip(NA_QROWS * j - NA_WIN_H // 2, 0, rows - NA_KROWS))
        qr = NA_QROWS * j + np.arange(NA_QROWS)
        rs = np.clip(qr - kh // 2, 0, rows - kh)
        kr = s0 + np.arange(NA_KROWS)
        row_ok = (kr[None, :] >= rs[:, None]) & (kr[None, :] < rs[:, None] + kh)
        dr = np.clip(kr[None, :] - qr[:, None] + NA_WIN_H - 1, 0, 2 * NA_WIN_H - 2)
        ok = row_ok[:, None, :, None] & col_ok[None, :, None, :]
        ok_f = ok.reshape(NA_QROWS * GRID_W, NA_KROWS * GRID_W)
        by_row = rpb.astype(F32)[:, dr.reshape(-1), :]
        spread = np.zeros((2 * NA_WIN_W - 1, GRID_W * GRID_W), np.float32)
        spread[dc.reshape(-1), np.arange(GRID_W * GRID_W)] = 1.0
        full = jnp.einsum("hpc,cq->hpq", by_row, jnp.asarray(spread), precision=lax.Precision.HIGHEST)
        full = full.reshape(-1, NA_QROWS, NA_KROWS, GRID_W, GRID_W).transpose(0, 1, 3, 2, 4)
        tabs.append(jnp.where(ok_f[None], full.reshape(-1, NA_QROWS * GRID_W, NA_KROWS * GRID_W), NEG_INF))
    return jnp.stack(tabs, axis=0)


def _na_body(q_ref, k_ref, v_ref, bias_ref, o_ref, *, C, rows):
    nq = NA_QROWS * GRID_W
    nk = NA_KROWS * GRID_W
    nblk = rows // NA_QROWS
    dn = (((1,), (1,)), ((), ()))
    kc = k_ref[0:C, :]
    vc = v_ref[0:C, :]
    s = lax.dot_general(q_ref[0:C, :], kc, dn, preferred_element_type=F32)
    p = jnp.exp(s - jnp.max(s, axis=-1, keepdims=True))
    o = jnp.dot(p.astype(BF16), vc, preferred_element_type=F32) / jnp.sum(p, axis=-1, keepdims=True)
    o_ref[0:C, :] = o.astype(o_ref.dtype)

    def body(j, carry):
        s0 = jnp.clip(NA_QROWS * j - NA_WIN_H // 2, 0, rows - NA_KROWS)
        cls = jnp.where(j == 0, 0, jnp.where(j == nblk - 1, 2, 1))
        qrows = pl.ds(pl.multiple_of(C + j * nq, GRID_W), nq)
        krows = pl.ds(pl.multiple_of(C + s0 * GRID_W, GRID_W), nk)
        qb = q_ref[qrows, :]
        kb = k_ref[krows, :]
        vb = v_ref[krows, :]
        s_lat = lax.dot_general(qb, kb, dn, preferred_element_type=F32) + bias_ref[cls, 0]
        s_ctx = lax.dot_general(qb, kc, dn, preferred_element_type=F32)
        m = jnp.maximum(jnp.max(s_lat, axis=-1, keepdims=True), jnp.max(s_ctx, axis=-1, keepdims=True))
        p_lat = jnp.exp(s_lat - m)
        p_ctx = jnp.exp(s_ctx - m)
        den = jnp.sum(p_lat, axis=-1, keepdims=True) + jnp.sum(p_ctx, axis=-1, keepdims=True)
        o = (jnp.dot(p_ctx.astype(BF16), vc, preferred_element_type=F32)
             + jnp.dot(p_lat.astype(BF16), vb, preferred_element_type=F32)) / den
        o_ref[qrows, :] = o.astype(o_ref.dtype)
        return carry

    lax.fori_loop(0, nblk, body, 0)


def _neighbourhood_attention(qkv, bias_tabs, *, B, C, L, D):
    H = D // NA_HEAD_DIM
    Lt = C + L
    rows = L // GRID_W
    blk = (Lt, NA_HEAD_DIM)
    body = functools.partial(_na_body, C=C, rows=rows)
    return pl.pallas_call(
        body, out_shape=jax.ShapeDtypeStruct((B * Lt, D), BF16), grid=(B, H),
        in_specs=[pl.BlockSpec(blk, lambda b, h: (b, h)),
                  pl.BlockSpec(blk, lambda b, h: (b, H + h)),
                  pl.BlockSpec(blk, lambda b, h: (b, 2 * H + h)),
                  pl.BlockSpec((3, 1) + bias_tabs.shape[2:], lambda b, h: (0, h, 0, 0))],
        out_specs=pl.BlockSpec(blk, lambda b, h: (b, h)),
        compiler_params=_cparams("parallel", "parallel"), name="na_attention",
    )(qkv, qkv, qkv, bias_tabs)


def _moe_body(be_ref, tok0_ref, tokn_ref, slot_ref, h_hbm, w_ref, wg_ref, bg_ref, wu_ref, bu_ref, wd_ref, bd_ref,
              ys_hbm, xbuf, ybuf, gsem, ssem):
    del be_ref
    tm = xbuf.shape[1]
    i = pl.program_id(0)
    n_blocks = pl.num_programs(0)
    cur = lax.rem(i, 2)

    def start_gather(tok_ref, buf):
        for r in range(tm):
            pltpu.make_async_copy(h_hbm.at[pl.ds(tok_ref[0, 0, r], 1)], xbuf.at[buf, pl.ds(r, 1)],
                                  gsem.at[buf]).start()

    def wait_gather(buf):
        pltpu.make_async_copy(h_hbm.at[pl.ds(0, tm)], xbuf.at[buf], gsem.at[buf]).wait()

    def wait_scatter(buf):
        pltpu.make_async_copy(ybuf.at[buf], ys_hbm.at[pl.ds(0, tm)], ssem.at[buf]).wait()

    @pl.when(i == 0)
    def _():
        start_gather(tok0_ref, 0)

    start_gather(tokn_ref, 1 - cur)
    wait_gather(cur)
    x = xbuf[cur].astype(BF16)
    g = jnp.minimum(jnp.dot(x, wg_ref[0], preferred_element_type=F32) + bg_ref[0], SWIGLU_LIMIT)
    u = jnp.clip(jnp.dot(x, wu_ref[0], preferred_element_type=F32) + bu_ref[0], -SWIGLU_LIMIT, SWIGLU_LIMIT)
    hdn = g * jax.nn.sigmoid(SWIGLU_ALPHA * g) * (u + 1.0)
    y = jnp.dot(hdn.astype(BF16), wd_ref[0], preferred_element_type=F32) + bd_ref[0]

    @pl.when(i >= 2)
    def _():
        wait_scatter(cur)

    ybuf[cur] = y * w_ref[...]

    for r in range(tm):
        pltpu.make_async_copy(ybuf.at[cur, pl.ds(r, 1)], ys_hbm.at[pl.ds(slot_ref[0, 0, r], 1)],
                              ssem.at[cur]).start()

    @pl.when(i == n_blocks - 1)
    def _():
        wait_scatter(1 - cur)
        wait_scatter(cur)
        wait_gather(1 - cur)


def _moe_experts(h_all, row_src, row_slot, row_w, blk_exp, wg, bg, wu, bu, wd, bd):
    M, D = h_all.shape
    E, _, De = wg.shape
    tm = MOE_TILE
    n_rows = row_src.shape[0]
    n_blocks = n_rows // tm
    assert n_blocks >= 2
    smem_blk = lambda idx: pl.BlockSpec((1, 1, tm), idx, memory_space=pltpu.SMEM)
    grid_spec = pltpu.PrefetchScalarGridSpec(
        num_scalar_prefetch=1, grid=(n_blocks,),
        in_specs=[smem_blk(lambda i, be: (0, 0, 0)),
                  smem_blk(lambda i, be: (jnp.minimum(i + 1, n_blocks - 1), 0, 0)),
                  smem_blk(lambda i, be: (i, 0, 0)),
                  pl.BlockSpec(memory_space=pl.ANY),
                  pl.BlockSpec((tm, 1), lambda i, be: (i, 0)),
                  pl.BlockSpec((1, D, De), lambda i, be: (be[i], 0, 0)),
                  pl.BlockSpec((1, 1, De), lambda i, be: (be[i], 0, 0)),
                  pl.BlockSpec((1, D, De), lambda i, be: (be[i], 0, 0)),
                  pl.BlockSpec((1, 1, De), lambda i, be: (be[i], 0, 0)),
                  pl.BlockSpec((1, De, D), lambda i, be: (be[i], 0, 0)),
                  pl.BlockSpec((1, 1, D), lambda i, be: (be[i], 0, 0))],
        out_specs=pl.BlockSpec(memory_space=pl.ANY),
        scratch_shapes=[pltpu.VMEM((2, tm, D), F32), pltpu.VMEM((2, tm, D), F32),
                        pltpu.SemaphoreType.DMA((2,)), pltpu.SemaphoreType.DMA((2,))])
    src3 = row_src.reshape(n_blocks, 1, tm)
    return pl.pallas_call(
        _moe_body, out_shape=jax.ShapeDtypeStruct((n_rows, D), F32), grid_spec=grid_spec,
        compiler_params=_cparams("arbitrary", disable_bounds_checks=True), name="moe_experts",
    )(blk_exp, src3, src3, row_slot.reshape(n_blocks, 1, tm), h_all, row_w.reshape(n_rows, 1), wg,
      bg.reshape(E, 1, De), wu, bu.reshape(E, 1, De), wd, bd.reshape(E, 1, D))


def _moe_ffn(h_all, logits, wg, bg, wu, bu, wd, bd, *, tok_rows_per_batch, batch_stride, row_offset):
    N = logits.shape[0]
    E = wg.shape[0]
    tm = MOE_TILE
    top_val, top_idx = lax.top_k(logits, TOP_K)
    top_w = jax.nn.softmax(top_val, axis=-1)
    flat_e = top_idx.reshape(-1)
    onehot = (flat_e[:, None] == jnp.arange(E, dtype=flat_e.dtype)[None, :]).astype(jnp.int32)
    csum = jnp.cumsum(onehot, axis=0)
    counts = csum[-1]
    rank = jnp.take_along_axis(csum, flat_e[:, None], axis=1)[:, 0] - 1
    padded = (counts + tm - 1) // tm * tm
    pad_end = jnp.cumsum(padded)
    pad_start = pad_end - padded
    dest = pad_start[flat_e] + rank
    n_blocks = (N * TOP_K + tm - 1) // tm + E
    n_rows = n_blocks * tm
    asg = jnp.arange(N * TOP_K, dtype=jnp.int32)
    packed = jnp.stack([asg + 1, lax.bitcast_convert_type(top_w.reshape(-1), jnp.int32)], axis=1)
    rows = jnp.zeros((n_rows, 2), jnp.int32).at[dest].set(packed)
    row_asg = rows[:, 0] - 1
    is_pad = row_asg < 0
    tok = jnp.maximum(row_asg, 0) // TOP_K
    row_src = (tok // tok_rows_per_batch) * batch_stride + row_offset + tok % tok_rows_per_batch
    row_w = lax.bitcast_convert_type(rows[:, 1], F32)
    row_slot = jnp.where(is_pad, N * TOP_K - 1 + jnp.cumsum(is_pad.astype(jnp.int32)),
                         (row_asg % TOP_K) * N + tok)
    blk_start = jnp.arange(n_blocks, dtype=jnp.int32) * tm
    blk_exp = jnp.minimum(jnp.sum(blk_start[:, None] >= pad_end[None, :], axis=1), E - 1).astype(jnp.int32)
    return _moe_experts(h_all, row_src, row_slot, row_w, blk_exp, wg, bg, wu, bu, wd, bd)


def _block_id(idx, size):
    return lax.shift_right_logical(idx, int(np.log2(size)))


def _split2(x):
    hi = x.astype(BF16)
    lo = (x - hi.astype(F32)).astype(BF16)
    return hi, lo


def _dot01_lhs(m01, x):
    hi, lo = _split2(x)
    return jnp.dot(m01, hi, preferred_element_type=F32) + jnp.dot(m01, lo, preferred_element_type=F32)


def _wkv_body(r_ref, k_ref, v_ref, pw_ref, pa_ref, pg_ref, w2_ref, a2_ref, g2_ref, w0_ref, a0_ref, kk_ref, ka_ref,
              rk_ref, gw_ref, gb_ref, o_ref, st_ref, rt_s, kt_s, at_s, bt_s, cm_s, bn_s, y_s, *, rev):
    T = WKV_T
    TB, LW = r_ref.shape
    n_pairs = LW // V7X_LANES
    n_chunks = TB // T
    pw_act = jnp.tanh(pw_ref[...]).astype(BF16)
    pa_act = pa_ref[...].astype(BF16)
    pg_act = jax.nn.sigmoid(pg_ref[...]).astype(BF16)

    @pl.when(pl.program_id(2) == 0)
    def _():
        st_ref[...] = jnp.zeros_like(st_ref)

    def earlier(p, q, strict):
        lo, hi = (p, q) if rev else (q, p)
        return lo < hi if strict else lo <= hi

    seg = min(LW, WKV_SEG)
    ri = lax.broadcasted_iota(jnp.int32, (TB, TB), 0)
    ci = lax.broadcasted_iota(jnp.int32, (TB, TB), 1)
    same_chunk = _block_id(ri, T) == _block_id(ci, T)
    tri = jnp.where(same_chunk & earlier(ri, ci, False), 1.0, 0.0).astype(BF16)
    li = _block_id(lax.broadcasted_iota(jnp.int32, (seg, seg), 0), RW_HEAD_DIM)
    lj = _block_id(lax.broadcasted_iota(jnp.int32, (seg, seg), 1), RW_HEAD_DIM)
    head_ones = jnp.where(li == lj, 1.0, 0.0).astype(BF16)

    def head_sum(z):
        return jnp.dot(z.astype(BF16), head_ones, preferred_element_type=F32)

    for sg in range(LW // seg):
        ls = slice(sg * seg, (sg + 1) * seg)
        r = r_ref[:, ls]
        k = k_ref[:, ls]
        e = jax.nn.sigmoid(jnp.dot(pw_act, w2_ref[:, ls], preferred_element_type=F32) + w0_ref[:, ls]) \
            * float(np.exp(-0.5))
        a = jax.nn.sigmoid(jnp.dot(pa_act, a2_ref[:, ls], preferred_element_type=F32) + a0_ref[:, ls])
        kkv = k * kk_ref[:, ls]
        kk = kkv * lax.rsqrt(head_sum(kkv * kkv) + 1e-12)
        kd = k * (1.0 + (a - 1.0) * ka_ref[:, ls])
        be = kk * a
        cum = _dot01_lhs(tri, e)
        ecum = jnp.exp(cum)
        cm_s[:, ls] = cum
        rt_s[:, ls] = r * jnp.exp(-cum)
        kt_s[:, ls] = kd * ecum
        at_s[:, ls] = -kk * jnp.exp(e - cum)
        bt_s[:, ls] = be * ecum
        bn_s[:, ls] = head_sum(r * kd * rk_ref[:, ls]) * v_ref[:, ls]

    lane = lax.broadcasted_iota(jnp.int32, (1, V7X_LANES), 1)
    m_a = jnp.where(lane < RW_HEAD_DIM, 1.0, 0.0).astype(F32)
    m_b = 1.0 - m_a
    pr = lax.broadcasted_iota(jnp.int32, (2 * T, 2 * T), 0)
    pc = lax.broadcasted_iota(jnp.int32, (2 * T, 2 * T), 1)
    same_head = _block_id(pr, T) == _block_id(pc, T)
    incl = same_head & earlier(pr, pc, False)
    strict = same_head & earlier(pr, pc, True)
    hr = _block_id(lax.broadcasted_iota(jnp.int32, (V7X_LANES, V7X_LANES), 0), RW_HEAD_DIM)
    hc = _block_id(lax.broadcasted_iota(jnp.int32, (V7X_LANES, V7X_LANES), 1), RW_HEAD_DIM)
    head_blk = hr == hc
    dn_nt = (((1,), (1,)), ((), ()))
    n_steps = int(np.log2(T))

    def stack(x):
        return jnp.concatenate([x * m_a, x * m_b], axis=0)

    def chunk_body(it, carry):
        c = (n_chunks - 1 - it) if rev else it
        rows = pl.ds(pl.multiple_of(c * T, T), T)
        rows_last = pl.ds(pl.multiple_of(c * T + (0 if rev else T - 8), 8), 8)
        last = slice(0, 1) if rev else slice(7, 8)
        lanes = [slice(p * V7X_LANES, (p + 1) * V7X_LANES) for p in range(n_pairs)]
        pairs = range(n_pairs)
        st = [st_ref[p] for p in pairs]
        st_b = [z.astype(BF16) for z in st]
        v = [v_ref[rows, ln] for ln in lanes]
        r_ms = [stack(rt_s[rows, ln]).astype(BF16) for ln in lanes]
        a_ms = [stack(at_s[rows, ln]).astype(BF16) for ln in lanes]
        k_ms = [stack(kt_s[rows, ln]).astype(BF16) for ln in lanes]
        b_ms = [stack(bt_s[rows, ln]).astype(BF16) for ln in lanes]
        v_ms = [stack(z).astype(BF16) for z in v]
        mm = [lax.dot_general(jnp.concatenate([r_ms[p], a_ms[p]], axis=0),
                              jnp.concatenate([k_ms[p], b_ms[p]], axis=0), dn_nt, preferred_element_type=F32)
              for p in pairs]
        m_rk = [jnp.where(incl, z[:2 * T, :2 * T], 0.0).astype(BF16) for z in mm]
        m_rb = [jnp.where(incl, z[:2 * T, 2 * T:], 0.0).astype(BF16) for z in mm]
        m_ak = [jnp.where(strict, z[2 * T:, :2 * T], 0.0).astype(BF16) for z in mm]
        nmat = [jnp.where(strict, z[2 * T:, 2 * T:], 0.0) for z in mm]
        x = [jnp.dot(jnp.concatenate([a_ms[p], m_ak[p]], axis=1), jnp.concatenate([st_b[p], v_ms[p]], axis=0),
                     preferred_element_type=F32) for p in pairs]
        for s in range(n_steps):
            nb = [z.astype(BF16) for z in nmat]
            if s < n_steps - 1:
                nx = [jnp.dot(nb[p], jnp.concatenate([nb[p], x[p].astype(BF16)], axis=1),
                              preferred_element_type=F32) for p in pairs]
                nmat = [z[:, :2 * T] for z in nx]
                x = [x[p] + nx[p][:, 2 * T:] for p in pairs]
            else:
                x = [x[p] + jnp.dot(nb[p], x[p].astype(BF16), preferred_element_type=F32) for p in pairs]
        y_stk = [jnp.dot(jnp.concatenate([r_ms[p], m_rk[p], m_rb[p]], axis=1),
                         jnp.concatenate([st_b[p], v_ms[p], x[p].astype(BF16)], axis=0),
                         preferred_element_type=F32) for p in pairs]
        w_end = [jnp.exp(-cm_s[rows_last, ln][last]) for ln in lanes]
        upd = []
        for p, ln in enumerate(lanes):
            u = x[p][:T] + x[p][T:]
            kbw_t = (jnp.concatenate([kt_s[rows, ln], bt_s[rows, ln]], axis=0) * w_end[p]).T
            upd.append(jnp.dot(kbw_t.astype(BF16), jnp.concatenate([v[p], u], axis=0).astype(BF16),
                               preferred_element_type=F32))
        for p, ln in enumerate(lanes):
            y_s[rows, ln] = y_stk[p][:T] + y_stk[p][T:]
            w_col = jnp.broadcast_to(w_end[p], (V7X_LANES, V7X_LANES)).T
            st_ref[p] = w_col * st[p] + jnp.where(head_blk, upd[p], 0.0)
        return carry

    lax.fori_loop(0, n_chunks, chunk_body, 0)

    inv_n = 1.0 / RW_HEAD_DIM
    for sg in range(LW // seg):
        ls = slice(sg * seg, (sg + 1) * seg)
        y = y_s[:, ls]
        yc = y - head_sum(y) * inv_n
        var = head_sum(yc * yc) * inv_n
        yn = yc * lax.rsqrt(var + RW_GN_EPS) * gw_ref[:, ls] + gb_ref[:, ls]
        gate = jnp.dot(pg_act, g2_ref[:, ls], preferred_element_type=F32)
        o_ref[:, ls] = ((yn + bn_s[:, ls]) * gate).astype(o_ref.dtype)


def _wkv(r, k, v, pw, pa, pg, w2, a2, g2, w0, a0, k_k, k_a, r_k, gn_w, gn_b, *, B, C, L, d):
    M, D = r.shape
    Lt = C + L
    rev = d == 1
    TB, LW = WKV_BLOCK, min(WKV_LANES, D)
    assert C % TB == 0 and L % TB == 0 and D % LW == 0
    nb, ncb = Lt // TB, C // TB

    def tblk(b, i):
        if not rev:
            return b * nb + i
        return b * nb + jnp.where(i < ncb, ncb - 1 - i, nb - 1 - i + ncb)

    blk = pl.BlockSpec((TB, LW), lambda b, gi, i: (tblk(b, i), gi))
    vec = pl.BlockSpec((1, LW), lambda b, gi, i: (0, gi))
    low = lambda w: pl.BlockSpec((TB, w.shape[0]), lambda b, gi, i: (tblk(b, i), d))
    up = lambda w: pl.BlockSpec((w.shape[0], LW), lambda b, gi, i: (0, gi))
    tbuf = pltpu.VMEM((TB, LW), F32)
    row = lambda z: z.reshape(1, D)
    return pl.pallas_call(
        functools.partial(_wkv_body, rev=rev), out_shape=jax.ShapeDtypeStruct((M, D), BF16),
        grid=(B, D // LW, nb),
        in_specs=[blk] * 3 + [low(w2), low(a2), low(g2), up(w2), up(a2), up(g2)] + [vec] * 7,
        out_specs=blk,
        scratch_shapes=[pltpu.VMEM((LW // V7X_LANES, V7X_LANES, V7X_LANES), F32)] + [tbuf] * 7,
        compiler_params=_cparams("parallel", "parallel", "arbitrary"), name="wkv7_bwd" if rev else "wkv7_fwd",
    )(r, k, v, pw, pa, pg, w2, a2, g2, row(w0), row(a0), row(k_k), row(k_a), row(r_k), row(gn_w), row(gn_b))


def _lerp(t, ts, mix):
    tf = t.astype(F32)
    return tf + (ts.astype(F32) - tf) * mix


def _rwkv7_time_mix(h, ts, p, *, B, C, L):
    M, D = h.shape
    mix = p["mix"]
    proj = lambda j, w, nm: _matmul([h, ts], w, prologue=_lerp, prologue_vecs=[mix[j].reshape(1, D)], name=nm)
    r = proj(0, p["w_r"].astype(BF16), "rw_r")
    k = proj(2, p["w_k"].astype(BF16), "rw_k")
    v = proj(3, p["w_v"].astype(BF16), "rw_v")
    lg = p["g1"].shape[2]
    lgp = -(-lg // V7X_LANES) * V7X_LANES
    w1c = jnp.concatenate([p["w1"][0], p["w1"][1]], axis=1).astype(BF16)
    a1c = jnp.concatenate([p["a1"][0], p["a1"][1]], axis=1).astype(BF16)
    g1p = jnp.pad(p["g1"], ((0, 0), (0, 0), (0, lgp - lg)))
    g1c = jnp.concatenate([g1p[0], g1p[1]], axis=1).astype(BF16)
    g2p = jnp.pad(p["g2"], ((0, 0), (0, lgp - lg), (0, 0))).astype(BF16)
    pw = proj(1, w1c, "rw_w1")
    pa = proj(4, a1c, "rw_a1")
    pg = proj(5, g1c, "rw_g1")

    o2 = [_wkv(r, k, v, pw, pa, pg, p["w2"][d].astype(BF16), p["a2"][d].astype(BF16), g2p[d], p["w0"][d],
               p["a0"][d], p["k_k"], p["k_a"], p["r_k"], p["gn_w"], p["gn_b"], B=B, C=C, L=L, d=d)
          for d in range(2)]
    return _matmul([o2[0], o2[1]], p["w_o"].astype(BF16),
                   prologue=lambda a, b: a.astype(F32) + b.astype(F32), name="rw_o")


def _router_params(router_w, router_b):
    D, E = router_w.shape
    rw = jnp.pad(router_w, ((0, 0), (0, V7X_LANES - E))).astype(BF16)
    rb = jnp.pad(router_b, (0, V7X_LANES - E)).reshape(1, V7X_LANES)
    return rw, rb


def kernel(x, c, ctx, c_ctx, l0_ada_w, l0_ada_b, l0_norm_mix, l0_norm_ffn, l0_na_w_qkv, l0_na_q_gain, l0_na_k_gain, l0_na_rpb, l0_na_w_o, l0_router_w, l0_router_b, l0_moe_w_gate, l0_moe_b_gate, l0_moe_w_up, l0_moe_b_up, l0_moe_w_down, l0_moe_b_down, l1_ada_w, l1_ada_b, l1_norm_mix, l1_norm_ffn, l1_rw_mix, l1_rw_w_r, l1_rw_w_k, l1_rw_w_v, l1_rw_k_k, l1_rw_k_a, l1_rw_r_k, l1_rw_w0, l1_rw_w1, l1_rw_w2, l1_rw_a0, l1_rw_a1, l1_rw_a2, l1_rw_g1, l1_rw_g2, l1_rw_gn_w, l1_rw_gn_b, l1_rw_w_o, l1_router_w, l1_router_b, l1_moe_w_gate, l1_moe_b_gate, l1_moe_w_up, l1_moe_b_up, l1_moe_w_down, l1_moe_b_down):
    B, L, D = x.shape
    C = ctx.shape[1]
    Lt = C + L
    M = B * Lt
    H = D // NA_HEAD_DIM
    E = l0_router_w.shape[1]
    assert C % ROW_TILE == 0 and L % ROW_TILE == 0 and L % GRID_W == 0
    tiles = dict(rows_per_batch=Lt, ctx_rows=C)

    mod0 = _ada_mod(c, c_ctx, l0_ada_w, l0_ada_b)
    xa, h = _rows(x.reshape(B * L, D), [], [], mod0, x_ctx=ctx.reshape(B * C, D), norm_w=l0_norm_mix, shift_row=0,
                  scale_row=1, name="l0_norm_mix", **tiles)
    scale = NA_HEAD_DIM ** -0.5
    gain = jnp.concatenate([jnp.tile(l0_na_q_gain * scale, H), jnp.tile(l0_na_k_gain, H), jnp.ones((D,), F32)])
    tn = 512 if D % 512 == 0 else D
    qkv = _matmul([h], l0_na_w_qkv.astype(BF16), out_dtypes=(BF16,),
                  epilogue=functools.partial(_qkv_epilogue, n_norm_tiles=2 * D // tn),
                  epi_vecs=[gain.reshape(1, 3 * D)], tn=tn, name="na_qkv")
    bias_tabs = _na_bias_tables(l0_na_rpb, L // GRID_W)
    att = _neighbourhood_attention(qkv, bias_tabs, B=B, C=C, L=L, D=D)
    y = _matmul([att], l0_na_w_o.astype(BF16), name="na_o")
    rw0, rb0 = _router_params(l0_router_w, l0_router_b)
    xa, h2, logits = _rows(xa, [y], [2], mod0, norm_w=l0_norm_ffn, shift_row=3, scale_row=4, h_dtype=F32,
                           router=(rw0, rb0), name="l0_norm_ffn", **tiles)
    ys = _moe_ffn(h2, logits[:, :E], l0_moe_w_gate.astype(BF16), l0_moe_b_gate, l0_moe_w_up.astype(BF16),
                  l0_moe_b_up, l0_moe_w_down.astype(BF16), l0_moe_b_down, tok_rows_per_batch=Lt, batch_stride=Lt,
                  row_offset=0)
    (xa,) = _rows(xa, [ys] * TOP_K, [5] * TOP_K, mod0, add_row_offsets=[k * M for k in range(TOP_K)],
                  tr=ROW_TILE // 2, name="l0_moe_residual", **tiles)

    mod1 = _ada_mod(c, c_ctx, l1_ada_w, l1_ada_b)
    h, ts = _norm_shift(xa, mod1, l1_norm_mix, **tiles)
    rwp = dict(mix=l1_rw_mix, w_r=l1_rw_w_r, w_k=l1_rw_w_k, w_v=l1_rw_w_v, k_k=l1_rw_k_k, k_a=l1_rw_k_a,
               r_k=l1_rw_r_k, w0=l1_rw_w0, w1=l1_rw_w1, w2=l1_rw_w2, a0=l1_rw_a0, a1=l1_rw_a1, a2=l1_rw_a2,
               g1=l1_rw_g1, g2=l1_rw_g2, gn_w=l1_rw_gn_w, gn_b=l1_rw_gn_b, w_o=l1_rw_w_o)
    y = _rwkv7_time_mix(h, ts, rwp, B=B, C=C, L=L)
    rw1, rb1 = _router_params(l1_router_w, l1_router_b)
    xa, h2, logits = _rows(xa, [y], [2], mod1, norm_w=l1_norm_ffn, shift_row=3, scale_row=4, h_dtype=F32,
                           router=(rw1, rb1), name="l1_norm_ffn", **tiles)
    lat_logits = logits.reshape(B, Lt, -1)[:, C:, :E].reshape(B * L, E)
    ys = _moe_ffn(h2, lat_logits, l1_moe_w_gate.astype(BF16), l1_moe_b_gate, l1_moe_w_up.astype(BF16),
                  l1_moe_b_up, l1_moe_w_down.astype(BF16), l1_moe_b_down, tok_rows_per_batch=L, batch_stride=Lt,
                  row_offset=C)
    (out,) = _rows(xa, [ys] * TOP_K, [5] * TOP_K, mod1, add_row_offsets=[k * B * L for k in range(TOP_K)],
                   rows_per_batch=L, ctx_rows=0, x_rows_per_batch=Lt, x_row_offset=C, tr=ROW_TILE // 2,
                   name="l1_moe_residual")
    return out.reshape(B, L, D)
```

```python
import functools

import numpy as np
import jax
import jax.numpy as jnp
from jax import lax
from jax.experimental import pallas as pl
from jax.experimental.pallas import tpu as pltpu

F32 = jnp.float32
BF16 = jnp.bfloat16

GRID_W = 64
N_MOD = 6
RMS_EPS = 1e-6
NEG_INF = -1e30
NA_HEAD_DIM = 128
NA_WIN_H = 8
NA_WIN_W = 16
RW_HEAD_DIM = 64
RW_GN_EPS = 64e-5
TOP_K = 4
SWIGLU_LIMIT = 7.0
SWIGLU_ALPHA = 1.702

V7X_LANES = 128
VMEM_LIMIT_BYTES = 56 * 1024 * 1024
MM_VMEM_BUDGET_BYTES = 44 * 1024 * 1024

ROW_TILE = 256
NA_QROWS = 4
NA_KROWS = 12
NA_UNROLL = 2
MOE_TILE = 256
WKV_T = 64
WKV_BLOCK = 256
WKV_LANES = 1024
WKV_SEG = 256


def _cparams(*sem, **kw):
    return pltpu.CompilerParams(dimension_semantics=sem, vmem_limit_bytes=VMEM_LIMIT_BYTES, **kw)


def _mm_body(*refs, n_a, n_pv, n_ev, n_em, n_out, prologue, epilogue, chunk):
    a_refs = refs[:n_a]
    pv_refs = refs[n_a:n_a + n_pv]
    w_ref = refs[n_a + n_pv]
    base = n_a + n_pv + 1
    ev_refs = refs[base:base + n_ev]
    em_refs = refs[base + n_ev:base + n_ev + n_em]
    out_refs = refs[base + n_ev + n_em:base + n_ev + n_em + n_out]
    scratch = refs[base + n_ev + n_em + n_out:]
    if prologue is not None:
        a_scr = scratch[0]

        @pl.when(pl.program_id(1) == 0)
        def _():
            pvs = [r[...] for r in pv_refs]

            def body(c, carry):
                rows = pl.ds(pl.multiple_of(c * chunk, chunk), chunk)
                a_scr[rows, :] = prologue(*[r[rows, :] for r in a_refs], *pvs).astype(BF16)
                return carry

            lax.fori_loop(0, a_scr.shape[0] // chunk, body, 0)

        a = a_scr[...]
    else:
        a = a_refs[0][...]
    acc = jnp.dot(a, w_ref[...].astype(BF16), preferred_element_type=F32)
    evs = [r[...] for r in ev_refs]
    ems = [r[...] for r in em_refs]
    if epilogue is None:
        out_refs[0][...] = acc.astype(out_refs[0].dtype)
    else:
        epilogue(acc, *evs, *ems, *out_refs)


def _mm_tiles(M, K, N, *, n_a, has_prologue, a_bytes, w_bytes, out_bytes):
    for tm, tn in ((768, 1024), (512, 1024), (768, 512), (512, 512), (256, 512), (256, 256)):
        if M % tm or N % tn:
            continue
        need = 2 * n_a * tm * K * a_bytes + 2 * K * tn * w_bytes + 2 * tm * tn * out_bytes
        need += tm * K * 2 if has_prologue else 0
        if need <= MM_VMEM_BUDGET_BYTES:
            return tm, tn
    return (256 if M % 256 == 0 else M), (256 if N % 256 == 0 else N)


def _matmul(a_list, w, *, out_dtypes=(F32,), prologue=None, prologue_vecs=(), epilogue=None,
            epi_vecs=(), epi_mats=(), a_col=0, tm=None, tn=None, name="mm"):
    M = a_list[0].shape[0]
    K, N = w.shape
    if tm is None or tn is None:
        tm, tn = _mm_tiles(M, K, N, n_a=len(a_list), has_prologue=prologue is not None,
                           a_bytes=a_list[0].dtype.itemsize, w_bytes=w.dtype.itemsize,
                           out_bytes=sum(jnp.dtype(dt).itemsize for dt in out_dtypes) + 4 * len(epi_mats))
    assert M % tm == 0 and N % tn == 0
    chunk = 32 if tm % 32 == 0 else tm
    in_specs = [pl.BlockSpec((tm, K), lambda m, n: (m, a_col)) for _ in a_list]
    in_specs += [pl.BlockSpec((1, K), lambda m, n: (0, 0)) for _ in prologue_vecs]
    in_specs += [pl.BlockSpec((K, tn), lambda m, n: (0, n))]
    in_specs += [pl.BlockSpec((1, tn), lambda m, n: (0, n)) for _ in epi_vecs]
    in_specs += [pl.BlockSpec((tm, tn), lambda m, n: (m, n)) for _ in epi_mats]
    out_shape = [jax.ShapeDtypeStruct((M, N), dt) for dt in out_dtypes]
    out_specs = [pl.BlockSpec((tm, tn), lambda m, n: (m, n)) for _ in out_dtypes]
    scratch = [pltpu.VMEM((tm, K), BF16)] if prologue is not None else []
    body = functools.partial(_mm_body, n_a=len(a_list), n_pv=len(prologue_vecs), n_ev=len(epi_vecs),
                             n_em=len(epi_mats), n_out=len(out_dtypes), prologue=prologue, epilogue=epilogue,
                             chunk=chunk)
    outs = pl.pallas_call(
        body, out_shape=out_shape, grid=(M // tm, N // tn), in_specs=in_specs, out_specs=out_specs,
        scratch_shapes=scratch, compiler_params=_cparams("parallel", "arbitrary"), name=name,
    )(*a_list, *prologue_vecs, w, *epi_vecs, *epi_mats)
    return outs[0] if len(out_dtypes) == 1 else outs


def _rows_body(*refs, n_add, gate_rows, ctx_src_tiles, do_norm, shift_row, scale_row, has_router, write_x, chunk):
    x_ref = refs[0]
    n_src = 1 if ctx_src_tiles is None else 2
    refs = refs[n_src - 1:]
    ctx_ref = refs[0]
    add_refs = refs[1:1 + n_add]
    mod_ref = refs[1 + n_add]
    pos = 2 + n_add
    if do_norm:
        nw_ref = refs[pos]
        pos += 1
    if has_router:
        rw_ref, rb_ref = refs[pos], refs[pos + 1]
        pos += 2
    outs = refs[pos:]
    oi = 0
    if write_x:
        xo_ref = outs[oi]
        oi += 1
    if do_norm:
        h_ref = outs[oi]
        oi += 1
    if has_router:
        lg_ref = outs[oi]
    gates = [mod_ref[0, g:g + 1, :] for g in gate_rows]
    if do_norm:
        nw = nw_ref[...]
        sh = mod_ref[0, shift_row:shift_row + 1, :]
        sc = 1.0 + mod_ref[0, scale_row:scale_row + 1, :]

    def run(src_ref):
        def body(c, carry):
            rows = pl.ds(pl.multiple_of(c * chunk, chunk), chunk)
            x = src_ref[rows, :]
            for a_ref, g in zip(add_refs, gates):
                x = x + g * a_ref[rows, :].astype(F32)
            if write_x:
                xo_ref[rows, :] = x
            if do_norm:
                y = x * lax.rsqrt(jnp.mean(x * x, axis=-1, keepdims=True) + RMS_EPS)
                h_ref[rows, :] = ((y * nw) * sc + sh).astype(h_ref.dtype)
            return carry

        lax.fori_loop(0, src_ref.shape[0] // chunk, body, 0)

    if ctx_src_tiles is None:
        run(x_ref)
    else:
        tiles_per_batch, ctx_tiles = ctx_src_tiles
        is_ctx = lax.rem(pl.program_id(0), tiles_per_batch) < ctx_tiles
        pl.when(is_ctx)(functools.partial(run, ctx_ref))
        pl.when(jnp.logical_not(is_ctx))(functools.partial(run, x_ref))
    if has_router:
        lg_ref[...] = jnp.dot(h_ref[...].astype(BF16), rw_ref[...], preferred_element_type=F32) + rb_ref[...]


def _rows(x, adds, gate_rows, mod, *, rows_per_batch, ctx_rows, x_rows_per_batch=None, x_row_offset=0, x_ctx=None,
          add_row_offsets=None, norm_w=None, shift_row=0, scale_row=1, h_dtype=None, router=None, write_x=True,
          tr=ROW_TILE, name="rows"):
    D = x.shape[1]
    B = mod.shape[0] // 2
    M = B * rows_per_batch
    x_rows_per_batch = x_rows_per_batch or rows_per_batch
    assert rows_per_batch % tr == 0 and ctx_rows % tr == 0 and x_rows_per_batch % tr == 0 and x_row_offset % tr == 0
    tiles_per_batch, ctx_tiles = rows_per_batch // tr, ctx_rows // tr
    x_tiles_per_batch, x_tile_offset = x_rows_per_batch // tr, x_row_offset // tr
    add_row_offsets = list(add_row_offsets or [0] * len(adds))
    assert all(off % tr == 0 for off in add_row_offsets)
    do_norm = norm_w is not None

    def mod_idx(i):
        return ((i // tiles_per_batch) * 2 + ((i % tiles_per_batch) >= ctx_tiles).astype(jnp.int32), 0, 0)

    row_spec = pl.BlockSpec((tr, D), lambda i: (i, 0))
    x_spec = pl.BlockSpec((tr, D), lambda i: ((i // tiles_per_batch) * x_tiles_per_batch + x_tile_offset
                                               + i % tiles_per_batch, 0))
    add_specs = [pl.BlockSpec((tr, D), lambda i, t=off // tr: (t + i, 0)) for off in add_row_offsets]
    src = [x]
    if x_ctx is not None:
        lat_tiles = tiles_per_batch - ctx_tiles
        x_spec = pl.BlockSpec((tr, D), lambda i: ((i // tiles_per_batch) * lat_tiles
                                                   + jnp.maximum(i % tiles_per_batch - ctx_tiles, 0), 0))
        ctx_spec = pl.BlockSpec((tr, D), lambda i: ((i // tiles_per_batch) * ctx_tiles
                                                     + jnp.minimum(i % tiles_per_batch, ctx_tiles - 1), 0))
        src = [x, x_ctx]
    args = [*src, *adds, mod]
    in_specs = [x_spec] + ([ctx_spec] if x_ctx is not None else []) + add_specs + [pl.BlockSpec((1, 8, D), mod_idx)]
    if do_norm:
        args.append(norm_w.reshape(1, D))
        in_specs.append(pl.BlockSpec((1, D), lambda i: (0, 0)))
    if router is not None:
        rw, rb = router
        args += [rw, rb]
        in_specs += [pl.BlockSpec(rw.shape, lambda i: (0, 0)), pl.BlockSpec(rb.shape, lambda i: (0, 0))]
    out_shape, out_specs = [], []
    if write_x:
        out_shape.append(jax.ShapeDtypeStruct((M, D), F32))
        out_specs.append(row_spec)
    if do_norm:
        out_shape.append(jax.ShapeDtypeStruct((M, D), h_dtype or BF16))
        out_specs.append(row_spec)
    if router is not None:
        out_shape.append(jax.ShapeDtypeStruct((M, V7X_LANES), F32))
        out_specs.append(pl.BlockSpec((tr, V7X_LANES), lambda i: (i, 0)))
    body = functools.partial(_rows_body, n_add=len(adds), gate_rows=tuple(gate_rows),
                             ctx_src_tiles=(tiles_per_batch, ctx_tiles) if x_ctx is not None else None,
                             do_norm=do_norm, shift_row=shift_row, scale_row=scale_row,
                             has_router=router is not None, write_x=write_x, chunk=32)
    return pl.pallas_call(body, out_shape=out_shape, grid=(M // tr,), in_specs=in_specs, out_specs=out_specs,
                          compiler_params=_cparams("parallel"), name=name)(*args)


def _norm_shift_body(x_ref, xp_ref, xn_ref, mod_ref, nw_ref, h_ref, ts_ref, hs, *, tiles_per_batch, ctx_tiles, chunk):
    tr, D = x_ref.shape
    half = D // 2
    nw = nw_ref[...]
    sh = mod_ref[0, 0:1, :]
    sc = 1.0 + mod_ref[0, 1:2, :]

    def norm(x):
        return ((x * lax.rsqrt(jnp.mean(x * x, axis=-1, keepdims=True) + RMS_EPS)) * nw) * sc + sh

    j = lax.rem(pl.program_id(0), tiles_per_batch)
    first = jnp.logical_or(j == 0, j == ctx_tiles)
    last = jnp.logical_or(j == ctx_tiles - 1, j == tiles_per_batch - 1)
    hs[0:8, :] = jnp.where(first, 0.0, norm(xp_ref[...]))
    hs[tr + 8:tr + 16, :] = jnp.where(last, 0.0, norm(xn_ref[...]))

    def norm_rows(c, carry):
        rows = pl.ds(pl.multiple_of(c * chunk, chunk), chunk)
        h = norm(x_ref[rows, :])
        h_ref[rows, :] = h.astype(h_ref.dtype)
        hs[pl.ds(pl.multiple_of(c * chunk + 8, 8), chunk), :] = h
        return carry

    lax.fori_loop(0, tr // chunk, norm_rows, 0)

    def shift_rows(c, carry):
        win = pl.ds(pl.multiple_of(c * chunk, 8), chunk + 16)
        prev = pltpu.roll(hs[win, :half], 1, axis=0)[8:8 + chunk]
        nxt = pltpu.roll(hs[win, half:], chunk + 15, axis=0)[8:8 + chunk]
        rows = pl.ds(pl.multiple_of(c * chunk, chunk), chunk)
        ts_ref[rows, :] = jnp.concatenate([prev, nxt], axis=1).astype(ts_ref.dtype)
        return carry

    lax.fori_loop(0, tr // chunk, shift_rows, 0)


def _norm_shift(x, mod, norm_w, *, rows_per_batch, ctx_rows):
    M, D = x.shape
    tr = ROW_TILE
    tiles_per_batch, ctx_tiles = rows_per_batch // tr, ctx_rows // tr
    assert rows_per_batch % tr == 0 and ctx_rows % tr == 0 and ctx_tiles >= 1
    sub = tr // 8

    def mod_idx(i):
        return ((i // tiles_per_batch) * 2 + ((i % tiles_per_batch) >= ctx_tiles).astype(jnp.int32), 0, 0)

    row_spec = pl.BlockSpec((tr, D), lambda i: (i, 0))
    body = functools.partial(_norm_shift_body, tiles_per_batch=tiles_per_batch, ctx_tiles=ctx_tiles, chunk=32)
    return pl.pallas_call(
        body, out_shape=[jax.ShapeDtypeStruct((M, D), BF16)] * 2, grid=(M // tr,),
        in_specs=[row_spec,
                  pl.BlockSpec((8, D), lambda i: (jnp.maximum(i * sub - 1, 0), 0)),
                  pl.BlockSpec((8, D), lambda i: (jnp.minimum((i + 1) * sub, M // 8 - 1), 0)),
                  pl.BlockSpec((1, 8, D), mod_idx), pl.BlockSpec((1, D), lambda i: (0, 0))],
        out_specs=[row_spec, row_spec], scratch_shapes=[pltpu.VMEM((tr + 16, D), F32)],
        compiler_params=_cparams("parallel"), name="norm_shift")(x, x, x, mod, norm_w.reshape(1, D))


def _ada_mod(c, c_ctx, ada_w, ada_b):
    B, D = c.shape
    cond = jnp.concatenate([c, c_ctx[None, :], jnp.zeros((8 - B - 1, D), F32)], axis=0)

    def epi(acc, bias, out_ref):
        out_ref[...] = acc + bias

    out = _matmul([cond], ada_w, prologue=lambda a: a * jax.nn.sigmoid(a), epilogue=epi,
                  epi_vecs=[ada_b.reshape(1, -1)], tm=8, tn=512, name="ada_mod")
    lat = out[:B].reshape(B, N_MOD, D)
    ctx = jnp.broadcast_to(out[B].reshape(1, N_MOD, D), (B, N_MOD, D))
    mod = jnp.stack([ctx, lat], axis=1).reshape(2 * B, N_MOD, D)
    return jnp.pad(mod, ((0, 0), (0, 8 - N_MOD), (0, 0)))


def _qkv_epilogue(acc, gain, out_ref, *, n_norm_tiles):
    tn = acc.shape[1]

    @pl.when(pl.program_id(1) < n_norm_tiles)
    def _():
        for i in range(tn // NA_HEAD_DIM):
            sl = slice(i * NA_HEAD_DIM, (i + 1) * NA_HEAD_DIM)
            z = acc[:, sl]
            z = z * lax.rsqrt(jnp.mean(z * z, axis=-1, keepdims=True) + RMS_EPS)
            out_ref[:, sl] = (z * gain[:, sl]).astype(out_ref.dtype)

    @pl.when(pl.program_id(1) >= n_norm_tiles)
    def _():
        out_ref[...] = acc.astype(out_ref.dtype)


def _na_bias_tables(rpb, rows):
    nblk = rows // NA_QROWS
    kh = min(NA_WIN_H, rows)
    assert kh == NA_WIN_H and rows >= NA_KROWS + NA_QROWS and rows % NA_QROWS == 0
    col = np.arange(GRID_W)
    col_start = np.clip(col - NA_WIN_W // 2, 0, GRID_W - NA_WIN_W)
    col_ok = (col[None, :] >= col_start[:, None]) & (col[None, :] < col_start[:, None] + NA_WIN_W)
    dc = np.clip(col[None, :] - col[:, None] + NA_WIN_W - 1, 0, 2 * NA_WIN_W - 2)
    tabs = []
    for j in (0, 1, nblk - 1):
        s0 = int(np.clip(NA_QROWS * j - NA_WIN_H // 2, 0, rows - NA_KROWS))
        qr = NA_QROWS * j + np.arange(NA_QROWS)
        rs = np.clip(qr - kh // 2, 0, rows - kh)
        kr = s0 + np.arange(NA_KROWS)
        row_ok = (kr[None, :] >= rs[:, None]) & (kr[None, :] < rs[:, None] + kh)
        dr = np.clip(kr[None, :] - qr[:, None] + NA_WIN_H - 1, 0, 2 * NA_WIN_H - 2)
        ok = row_ok[:, None, :, None] & col_ok[None, :, None, :]
        ok_f = ok.reshape(NA_QROWS * GRID_W, NA_KROWS * GRID_W)
        by_row = rpb.astype(F32)[:, dr.reshape(-1), :]
        spread = np.zeros((2 * NA_WIN_W - 1, GRID_W * GRID_W), np.float32)
        spread[dc.reshape(-1), np.arange(GRID_W * GRID_W)] = 1.0
        full = jnp.einsum("hpc,cq->hpq", by_row, jnp.asarray(spread), precision=lax.Precision.HIGHEST)
        full = full.reshape(-1, NA_QROWS, NA_KROWS, GRID_W, GRID_W).transpose(0, 1, 3, 2, 4)
        tabs.append(jnp.where(ok_f[None], full.reshape(-1, NA_QROWS * GRID_W, NA_KROWS * GRID_W), NEG_INF))
    return jnp.stack(tabs, axis=0)


def _na_body(q_ref, k_ref, v_ref, bias_ref, o_ref, *, C, rows):
    nq = NA_QROWS * GRID_W
    nk = NA_KROWS * GRID_W
    nblk = rows // NA_QROWS
    dn = (((1,), (1,)), ((), ()))
    kc = k_ref[0:C, :]
    vc = v_ref[0:C, :]
    s = lax.dot_general(q_ref[0:C, :], kc, dn, preferred_element_type=F32)
    p = jnp.exp(s - jnp.max(s, axis=-1, keepdims=True))
    o = jnp.dot(p.astype(BF16), vc, preferred_element_type=F32) / jnp.sum(p, axis=-1, keepdims=True)
    o_ref[0:C, :] = o.astype(o_ref.dtype)

    def body(jj, carry):
        js = [jj * NA_UNROLL + t for t in range(NA_UNROLL)]
        s0 = [jnp.clip(NA_QROWS * j - NA_WIN_H // 2, 0, rows - NA_KROWS) for j in js]
        cls = [jnp.where(j == 0, 0, jnp.where(j == nblk - 1, 2, 1)) for j in js]
        qrows = [pl.ds(pl.multiple_of(C + j * nq, GRID_W), nq) for j in js]
        krows = [pl.ds(pl.multiple_of(C + s * GRID_W, GRID_W), nk) for s in s0]
        qb = [q_ref[r, :] for r in qrows]
        s_lat = [lax.dot_general(q, k_ref[r, :], dn, preferred_element_type=F32) + bias_ref[c, 0]
                 for q, r, c in zip(qb, krows, cls)]
        s_ctx = [lax.dot_general(q, kc, dn, preferred_element_type=F32) for q in qb]
        m = [jnp.maximum(jnp.max(a, axis=-1, keepdims=True), jnp.max(b, axis=-1, keepdims=True))
             for a, b in zip(s_lat, s_ctx)]
        p_lat = [jnp.exp(a - mm) for a, mm in zip(s_lat, m)]
        p_ctx = [jnp.exp(b - mm) for b, mm in zip(s_ctx, m)]
        den = [jnp.sum(a, axis=-1, keepdims=True) + jnp.sum(b, axis=-1, keepdims=True)
               for a, b in zip(p_lat, p_ctx)]
        o = [jnp.dot(b.astype(BF16), vc, preferred_element_type=F32)
             + jnp.dot(a.astype(BF16), v_ref[r, :], preferred_element_type=F32)
             for a, b, r in zip(p_lat, p_ctx, krows)]
        for r, oo, dd in zip(qrows, o, den):
            o_ref[r, :] = (oo / dd).astype(o_ref.dtype)
        return carry

    assert nblk % NA_UNROLL == 0
    lax.fori_loop(0, nblk // NA_UNROLL, body, 0)


def _neighbourhood_attention(qkv, bias_tabs, *, B, C, L, D):
    H = D // NA_HEAD_DIM
    Lt = C + L
    rows = L // GRID_W
    blk = (Lt, NA_HEAD_DIM)
    body = functools.partial(_na_body, C=C, rows=rows)
    return pl.pallas_call(
        body, out_shape=jax.ShapeDtypeStruct((B * Lt, D), BF16), grid=(B, H),
        in_specs=[pl.BlockSpec(blk, lambda b, h: (b, h)),
                  pl.BlockSpec(blk, lambda b, h: (b, H + h)),
                  pl.BlockSpec(blk, lambda b, h: (b, 2 * H + h)),
                  pl.BlockSpec((3, 1) + bias_tabs.shape[2:], lambda b, h: (0, h, 0, 0))],
        out_specs=pl.BlockSpec(blk, lambda b, h: (b, h)),
        compiler_params=_cparams("parallel", "parallel"), name="na_attention",
    )(qkv, qkv, qkv, bias_tabs)


def _moe_body(be_ref, tok0_ref, tokn_ref, slot_ref, h_hbm, w_ref, wg_ref, bg_ref, wu_ref, bu_ref, wd_ref, bd_ref,
              ys_hbm, xbuf, ybuf, gsem, ssem):
    del be_ref
    tm = xbuf.shape[1]
    i = pl.program_id(0)
    n_blocks = pl.num_programs(0)
    cur = lax.rem(i, 2)

    def start_gather(tok_ref, buf):
        for r in range(tm):
            pltpu.make_async_copy(h_hbm.at[pl.ds(tok_ref[0, 0, r], 1)], xbuf.at[buf, pl.ds(r, 1)],
                                  gsem.at[buf]).start()

    def wait_gather(buf):
        pltpu.make_async_copy(h_hbm.at[pl.ds(0, tm)], xbuf.at[buf], gsem.at[buf]).wait()

    def wait_scatter(buf):
        pltpu.make_async_copy(ybuf.at[buf], ys_hbm.at[pl.ds(0, tm)], ssem.at[buf]).wait()

    @pl.when(i == 0)
    def _():
        start_gather(tok0_ref, 0)

    start_gather(tokn_ref, 1 - cur)
    wait_gather(cur)
    x = xbuf[cur].astype(BF16)
    g = jnp.minimum(jnp.dot(x, wg_ref[0], preferred_element_type=F32) + bg_ref[0], SWIGLU_LIMIT)
    u = jnp.clip(jnp.dot(x, wu_ref[0], preferred_element_type=F32) + bu_ref[0], -SWIGLU_LIMIT, SWIGLU_LIMIT)
    hdn = g * jax.nn.sigmoid(SWIGLU_ALPHA * g) * (u + 1.0)
    y = jnp.dot(hdn.astype(BF16), wd_ref[0], preferred_element_type=F32) + bd_ref[0]

    @pl.when(i >= 2)
    def _():
        wait_scatter(cur)

    ybuf[cur] = y * w_ref[...]

    for r in range(tm):
        pltpu.make_async_copy(ybuf.at[cur, pl.ds(r, 1)], ys_hbm.at[pl.ds(slot_ref[0, 0, r], 1)],
                              ssem.at[cur]).start()

    @pl.when(i == n_blocks - 1)
    def _():
        wait_scatter(1 - cur)
        wait_scatter(cur)
        wait_gather(1 - cur)


def _moe_experts(h_all, row_src, row_slot, row_w, blk_exp, wg, bg, wu, bu, wd, bd):
    M, D = h_all.shape
    E, _, De = wg.shape
    tm = MOE_TILE
    n_rows = row_src.shape[0]
    n_blocks = n_rows // tm
    assert n_blocks >= 2
    smem_blk = lambda idx: pl.BlockSpec((1, 1, tm), idx, memory_space=pltpu.SMEM)
    grid_spec = pltpu.PrefetchScalarGridSpec(
        num_scalar_prefetch=1, grid=(n_blocks,),
        in_specs=[smem_blk(lambda i, be: (0, 0, 0)),
                  smem_blk(lambda i, be: (jnp.minimum(i + 1, n_blocks - 1), 0, 0)),
                  smem_blk(lambda i, be: (i, 0, 0)),
                  pl.BlockSpec(memory_space=pl.ANY),
                  pl.BlockSpec((tm, 1), lambda i, be: (i, 0)),
                  pl.BlockSpec((1, D, De), lambda i, be: (be[i], 0, 0)),
                  pl.BlockSpec((1, 1, De), lambda i, be: (be[i], 0, 0)),
                  pl.BlockSpec((1, D, De), lambda i, be: (be[i], 0, 0)),
                  pl.BlockSpec((1, 1, De), lambda i, be: (be[i], 0, 0)),
                  pl.BlockSpec((1, De, D), lambda i, be: (be[i], 0, 0)),
                  pl.BlockSpec((1, 1, D), lambda i, be: (be[i], 0, 0))],
        out_specs=pl.BlockSpec(memory_space=pl.ANY),
        scratch_shapes=[pltpu.VMEM((2, tm, D), F32), pltpu.VMEM((2, tm, D), F32),
                        pltpu.SemaphoreType.DMA((2,)), pltpu.SemaphoreType.DMA((2,))])
    src3 = row_src.reshape(n_blocks, 1, tm)
    return pl.pallas_call(
        _moe_body, out_shape=jax.ShapeDtypeStruct((n_rows, D), F32), grid_spec=grid_spec,
        compiler_params=_cparams("arbitrary", disable_bounds_checks=True), name="moe_experts",
    )(blk_exp, src3, src3, row_slot.reshape(n_blocks, 1, tm), h_all, row_w.reshape(n_rows, 1), wg,
      bg.reshape(E, 1, De), wu, bu.reshape(E, 1, De), wd, bd.reshape(E, 1, D))


def _moe_ffn(h_all, logits, wg, bg, wu, bu, wd, bd, *, tok_rows_per_batch, batch_stride, row_offset):
    N = logits.shape[0]
    E = wg.shape[0]
    tm = MOE_TILE
    top_val, top_idx = lax.top_k(logits, TOP_K)
    top_w = jax.nn.softmax(top_val, axis=-1)
    flat_e = top_idx.reshape(-1)
    onehot = (flat_e[:, None] == jnp.arange(E, dtype=flat_e.dtype)[None, :]).astype(jnp.int32)
    csum = jnp.cumsum(onehot, axis=0)
    counts = csum[-1]
    rank = jnp.take_along_axis(csum, flat_e[:, None], axis=1)[:, 0] - 1
    padded = (counts + tm - 1) // tm * tm
    pad_end = jnp.cumsum(padded)
    pad_start = pad_end - padded
    dest = pad_start[flat_e] + rank
    n_blocks = (N * TOP_K + tm - 1) // tm + E
    n_rows = n_blocks * tm
    asg = jnp.arange(N * TOP_K, dtype=jnp.int32)
    packed = jnp.stack([asg + 1, lax.bitcast_convert_type(top_w.reshape(-1), jnp.int32)], axis=1)
    rows = jnp.zeros((n_rows, 2), jnp.int32).at[dest].set(packed)
    row_asg = rows[:, 0] - 1
    is_pad = row_asg < 0
    tok = jnp.maximum(row_asg, 0) // TOP_K
    row_src = (tok // tok_rows_per_batch) * batch_stride + row_offset + tok % tok_rows_per_batch
    row_w = lax.bitcast_convert_type(rows[:, 1], F32)
    row_slot = jnp.where(is_pad, N * TOP_K - 1 + jnp.cumsum(is_pad.astype(jnp.int32)),
                         (row_asg % TOP_K) * N + tok)
    blk_start = jnp.arange(n_blocks, dtype=jnp.int32) * tm
    blk_exp = jnp.minimum(jnp.sum(blk_start[:, None] >= pad_end[None, :], axis=1), E - 1).astype(jnp.int32)
    return _moe_experts(h_all, row_src, row_slot, row_w, blk_exp, wg, bg, wu, bu, wd, bd)


def _block_id(idx, size):
    return lax.shift_right_logical(idx, int(np.log2(size)))


def _split2(x):
    hi = x.astype(BF16)
    lo = (x - hi.astype(F32)).astype(BF16)
    return hi, lo


def _dot01_lhs(m01, x):
    hi, lo = _split2(x)
    return jnp.dot(m01, hi, preferred_element_type=F32) + jnp.dot(m01, lo, preferred_element_type=F32)


def _wkv_body(r_ref, k_ref, v_ref, pw_ref, pa_ref, pg_ref, w2_ref, a2_ref, g2_ref, w0_ref, a0_ref, kk_ref, ka_ref,
              rk_ref, gw_ref, gb_ref, o_ref, st_ref, rt_s, kt_s, at_s, bt_s, cm_s, bn_s, y_s, *, rev):
    T = WKV_T
    TB, LW = r_ref.shape
    n_pairs = LW // V7X_LANES
    n_chunks = TB // T
    pw_act = jnp.tanh(pw_ref[...]).astype(BF16)
    pa_act = pa_ref[...].astype(BF16)
    pg_act = jax.nn.sigmoid(pg_ref[...]).astype(BF16)

    @pl.when(pl.program_id(2) == 0)
    def _():
        st_ref[...] = jnp.zeros_like(st_ref)

    def earlier(p, q, strict):
        lo, hi = (p, q) if rev else (q, p)
        return lo < hi if strict else lo <= hi

    seg = min(LW, WKV_SEG)
    ri = lax.broadcasted_iota(jnp.int32, (TB, TB), 0)
    ci = lax.broadcasted_iota(jnp.int32, (TB, TB), 1)
    same_chunk = _block_id(ri, T) == _block_id(ci, T)
    tri = jnp.where(same_chunk & earlier(ri, ci, False), 1.0, 0.0).astype(BF16)
    li = _block_id(lax.broadcasted_iota(jnp.int32, (seg, seg), 0), RW_HEAD_DIM)
    lj = _block_id(lax.broadcasted_iota(jnp.int32, (seg, seg), 1), RW_HEAD_DIM)
    head_ones = jnp.where(li == lj, 1.0, 0.0).astype(BF16)

    def head_sum(z):
        return jnp.dot(z.astype(BF16), head_ones, preferred_element_type=F32)

    for sg in range(LW // seg):
        ls = slice(sg * seg, (sg + 1) * seg)
        r = r_ref[:, ls]
        k = k_ref[:, ls]
        e = jax.nn.sigmoid(jnp.dot(pw_act, w2_ref[:, ls], preferred_element_type=F32) + w0_ref[:, ls]) \
            * float(np.exp(-0.5))
        a = jax.nn.sigmoid(jnp.dot(pa_act, a2_ref[:, ls], preferred_element_type=F32) + a0_ref[:, ls])
        kkv = k * kk_ref[:, ls]
        kk = kkv * lax.rsqrt(head_sum(kkv * kkv) + 1e-12)
        kd = k * (1.0 + (a - 1.0) * ka_ref[:, ls])
        be = kk * a
        cum = _dot01_lhs(tri, e)
        ecum = jnp.exp(cum)
        cm_s[:, ls] = cum
        rt_s[:, ls] = r * jnp.exp(-cum)
        kt_s[:, ls] = kd * ecum
        at_s[:, ls] = -kk * jnp.exp(e - cum)
        bt_s[:, ls] = be * ecum
        bn_s[:, ls] = head_sum(r * kd * rk_ref[:, ls]) * v_ref[:, ls]

    lane = lax.broadcasted_iota(jnp.int32, (1, V7X_LANES), 1)
    m_a = jnp.where(lane < RW_HEAD_DIM, 1.0, 0.0).astype(F32)
    m_b = 1.0 - m_a
    pr = lax.broadcasted_iota(jnp.int32, (2 * T, 2 * T), 0)
    pc = lax.broadcasted_iota(jnp.int32, (2 * T, 2 * T), 1)
    same_head = _block_id(pr, T) == _block_id(pc, T)
    incl = same_head & earlier(pr, pc, False)
    strict = same_head & earlier(pr, pc, True)
    hr = _block_id(lax.broadcasted_iota(jnp.int32, (V7X_LANES, V7X_LANES), 0), RW_HEAD_DIM)
    hc = _block_id(lax.broadcasted_iota(jnp.int32, (V7X_LANES, V7X_LANES), 1), RW_HEAD_DIM)
    head_blk = hr == hc
    dn_nt = (((1,), (1,)), ((), ()))
    n_steps = int(np.log2(T))

    def stack(x):
        return jnp.concatenate([x * m_a, x * m_b], axis=0)

    def chunk_body(it, carry):
        c = (n_chunks - 1 - it) if rev else it
        rows = pl.ds(pl.multiple_of(c * T, T), T)
        rows_last = pl.ds(pl.multiple_of(c * T + (0 if rev else T - 8), 8), 8)
        last = slice(0, 1) if rev else slice(7, 8)
        lanes = [slice(p * V7X_LANES, (p + 1) * V7X_LANES) for p in range(n_pairs)]
        pairs = range(n_pairs)
        st = [st_ref[p] for p in pairs]
        st_b = [z.astype(BF16) for z in st]
        v = [v_ref[rows, ln] for ln in lanes]
        r_ms = [stack(rt_s[rows, ln]).astype(BF16) for ln in lanes]
        a_ms = [stack(at_s[rows, ln]).astype(BF16) for ln in lanes]
        k_ms = [stack(kt_s[rows, ln]).astype(BF16) for ln in lanes]
        b_ms = [stack(bt_s[rows, ln]).astype(BF16) for ln in lanes]
        v_ms = [stack(z).astype(BF16) for z in v]
        mm = [lax.dot_general(jnp.concatenate([r_ms[p], a_ms[p]], axis=0),
                              jnp.concatenate([k_ms[p], b_ms[p]], axis=0), dn_nt, preferred_element_type=F32)
              for p in pairs]
        m_rk = [jnp.where(incl, z[:2 * T, :2 * T], 0.0).astype(BF16) for z in mm]
        m_rb = [jnp.where(incl, z[:2 * T, 2 * T:], 0.0).astype(BF16) for z in mm]
        m_ak = [jnp.where(strict, z[2 * T:, :2 * T], 0.0).astype(BF16) for z in mm]
        nmat = [jnp.where(strict, z[2 * T:, 2 * T:], 0.0) for z in mm]
        x = [jnp.dot(jnp.concatenate([a_ms[p], m_ak[p]], axis=1), jnp.concatenate([st_b[p], v_ms[p]], axis=0),
                     preferred_element_type=F32) for p in pairs]
        for s in range(n_steps):
            nb = [z.astype(BF16) for z in nmat]
            if s < n_steps - 1:
                nx = [jnp.dot(nb[p], jnp.concatenate([nb[p], x[p].astype(BF16)], axis=1),
                              preferred_element_type=F32) for p in pairs]
                nmat = [z[:, :2 * T] for z in nx]
                x = [x[p] + nx[p][:, 2 * T:] for p in pairs]
            else:
                x = [x[p] + jnp.dot(nb[p], x[p].astype(BF16), preferred_element_type=F32) for p in pairs]
        y_stk = [jnp.dot(jnp.concatenate([r_ms[p], m_rk[p], m_rb[p]], axis=1),
                         jnp.concatenate([st_b[p], v_ms[p], x[p].astype(BF16)], axis=0),
                         preferred_element_type=F32) for p in pairs]
        w_end = [jnp.exp(-cm_s[rows_last, ln][last]) for ln in lanes]
        upd = []
        for p, ln in enumerate(lanes):
            u = x[p][:T] + x[p][T:]
            kbw_t = (jnp.concatenate([kt_s[rows, ln], bt_s[rows, ln]], axis=0) * w_end[p]).T
            upd.append(jnp.dot(kbw_t.astype(BF16), jnp.concatenate([v[p], u], axis=0).astype(BF16),
                               preferred_element_type=F32))
        for p, ln in enumerate(lanes):
            y_s[rows, ln] = y_stk[p][:T] + y_stk[p][T:]
            w_col = jnp.broadcast_to(w_end[p], (V7X_LANES, V7X_LANES)).T
            st_ref[p] = w_col * st[p] + jnp.where(head_blk, upd[p], 0.0)
        return carry

    lax.fori_loop(0, n_chunks, chunk_body, 0)

    inv_n = 1.0 / RW_HEAD_DIM
    for sg in range(LW // seg):
        ls = slice(sg * seg, (sg + 1) * seg)
        y = y_s[:, ls]
        yc = y - head_sum(y) * inv_n
        var = head_sum(yc * yc) * inv_n
        yn = yc * lax.rsqrt(var + RW_GN_EPS) * gw_ref[:, ls] + gb_ref[:, ls]
        gate = jnp.dot(pg_act, g2_ref[:, ls], preferred_element_type=F32)
        o_ref[:, ls] = ((yn + bn_s[:, ls]) * gate).astype(o_ref.dtype)


def _wkv(r, k, v, pw, pa, pg, w2, a2, g2, w0, a0, k_k, k_a, r_k, gn_w, gn_b, *, B, C, L, d):
    M, D = r.shape
    Lt = C + L
    rev = d == 1
    TB, LW = WKV_BLOCK, min(WKV_LANES, D)
    assert C % TB == 0 and L % TB == 0 and D % LW == 0
    nb, ncb = Lt // TB, C // TB

    def tblk(b, i):
        if not rev:
            return b * nb + i
        return b * nb + jnp.where(i < ncb, ncb - 1 - i, nb - 1 - i + ncb)

    blk = pl.BlockSpec((TB, LW), lambda b, gi, i: (tblk(b, i), gi))
    vec = pl.BlockSpec((1, LW), lambda b, gi, i: (0, gi))
    low = lambda w: pl.BlockSpec((TB, w.shape[0]), lambda b, gi, i: (tblk(b, i), d))
    up = lambda w: pl.BlockSpec((w.shape[0], LW), lambda b, gi, i: (0, gi))
    tbuf = pltpu.VMEM((TB, LW), F32)
    row = lambda z: z.reshape(1, D)
    return pl.pallas_call(
        functools.partial(_wkv_body, rev=rev), out_shape=jax.ShapeDtypeStruct((M, D), BF16),
        grid=(B, D // LW, nb),
        in_specs=[blk] * 3 + [low(w2), low(a2), low(g2), up(w2), up(a2), up(g2)] + [vec] * 7,
        out_specs=blk,
        scratch_shapes=[pltpu.VMEM((LW // V7X_LANES, V7X_LANES, V7X_LANES), F32)] + [tbuf] * 7,
        compiler_params=_cparams("parallel", "parallel", "arbitrary"), name="wkv7_bwd" if rev else "wkv7_fwd",
    )(r, k, v, pw, pa, pg, w2, a2, g2, row(w0), row(a0), row(k_k), row(k_a), row(r_k), row(gn_w), row(gn_b))


def _lerp(t, ts, mix):
    tf = t.astype(F32)
    return tf + (ts.astype(F32) - tf) * mix


def _rwkv7_time_mix(h, ts, p, *, B, C, L):
    M, D = h.shape
    mix = p["mix"]
    proj = lambda j, w, nm: _matmul([h, ts], w, prologue=_lerp, prologue_vecs=[mix[j].reshape(1, D)], name=nm)
    r = proj(0, p["w_r"].astype(BF16), "rw_r")
    k = proj(2, p["w_k"].astype(BF16), "rw_k")
    v = proj(3, p["w_v"].astype(BF16), "rw_v")
    lg = p["g1"].shape[2]
    lgp = -(-lg // V7X_LANES) * V7X_LANES
    w1c = jnp.concatenate([p["w1"][0], p["w1"][1]], axis=1).astype(BF16)
    a1c = jnp.concatenate([p["a1"][0], p["a1"][1]], axis=1).astype(BF16)
    g1p = jnp.pad(p["g1"], ((0, 0), (0, 0), (0, lgp - lg)))
    g1c = jnp.concatenate([g1p[0], g1p[1]], axis=1).astype(BF16)
    g2p = jnp.pad(p["g2"], ((0, 0), (0, lgp - lg), (0, 0))).astype(BF16)
    pw = proj(1, w1c, "rw_w1")
    pa = proj(4, a1c, "rw_a1")
    pg = proj(5, g1c, "rw_g1")

    o2 = [_wkv(r, k, v, pw, pa, pg, p["w2"][d].astype(BF16), p["a2"][d].astype(BF16), g2p[d], p["w0"][d],
               p["a0"][d], p["k_k"], p["k_a"], p["r_k"], p["gn_w"], p["gn_b"], B=B, C=C, L=L, d=d)
          for d in range(2)]
    return _matmul([o2[0], o2[1]], p["w_o"].astype(BF16),
                   prologue=lambda a, b: a.astype(F32) + b.astype(F32), name="rw_o")


def _router_params(router_w, router_b):
    D, E = router_w.shape
    rw = jnp.pad(router_w, ((0, 0), (0, V7X_LANES - E))).astype(BF16)
    rb = jnp.pad(router_b, (0, V7X_LANES - E)).reshape(1, V7X_LANES)
    return rw, rb


def kernel(x, c, ctx, c_ctx, l0_ada_w, l0_ada_b, l0_norm_mix, l0_norm_ffn, l0_na_w_qkv, l0_na_q_gain, l0_na_k_gain, l0_na_rpb, l0_na_w_o, l0_router_w, l0_router_b, l0_moe_w_gate, l0_moe_b_gate, l0_moe_w_up, l0_moe_b_up, l0_moe_w_down, l0_moe_b_down, l1_ada_w, l1_ada_b, l1_norm_mix, l1_norm_ffn, l1_rw_mix, l1_rw_w_r, l1_rw_w_k, l1_rw_w_v, l1_rw_k_k, l1_rw_k_a, l1_rw_r_k, l1_rw_w0, l1_rw_w1, l1_rw_w2, l1_rw_a0, l1_rw_a1, l1_rw_a2, l1_rw_g1, l1_rw_g2, l1_rw_gn_w, l1_rw_gn_b, l1_rw_w_o, l1_router_w, l1_router_b, l1_moe_w_gate, l1_moe_b_gate, l1_moe_w_up, l1_moe_b_up, l1_moe_w_down, l1_moe_b_down):
    B, L, D = x.shape
    C = ctx.shape[1]
    Lt = C + L
    M = B * Lt
    H = D // NA_HEAD_DIM
    E = l0_router_w.shape[1]
    assert C % ROW_TILE == 0 and L % ROW_TILE == 0 and L % GRID_W == 0
    tiles = dict(rows_per_batch=Lt, ctx_rows=C)

    mod0 = _ada_mod(c, c_ctx, l0_ada_w, l0_ada_b)
    xa, h = _rows(x.reshape(B * L, D), [], [], mod0, x_ctx=ctx.reshape(B * C, D), norm_w=l0_norm_mix, shift_row=0,
                  scale_row=1, name="l0_norm_mix", **tiles)
    scale = NA_HEAD_DIM ** -0.5
    gain = jnp.concatenate([jnp.tile(l0_na_q_gain * scale, H), jnp.tile(l0_na_k_gain, H), jnp.ones((D,), F32)])
    tm, tn = _mm_tiles(M, D, 3 * D, n_a=1, has_prologue=False, a_bytes=2, w_bytes=2, out_bytes=2)
    assert D % tn == 0 and tn % NA_HEAD_DIM == 0
    qkv = _matmul([h], l0_na_w_qkv.astype(BF16), out_dtypes=(BF16,),
                  epilogue=functools.partial(_qkv_epilogue, n_norm_tiles=2 * D // tn),
                  epi_vecs=[gain.reshape(1, 3 * D)], tm=tm, tn=tn, name="na_qkv")
    bias_tabs = _na_bias_tables(l0_na_rpb, L // GRID_W)
    att = _neighbourhood_attention(qkv, bias_tabs, B=B, C=C, L=L, D=D)
    y = _matmul([att], l0_na_w_o.astype(BF16), name="na_o")
    rw0, rb0 = _router_params(l0_router_w, l0_router_b)
    xa, h2, logits = _rows(xa, [y], [2], mod0, norm_w=l0_norm_ffn, shift_row=3, scale_row=4, h_dtype=F32,
                           router=(rw0, rb0), name="l0_norm_ffn", **tiles)
    ys = _moe_ffn(h2, logits[:, :E], l0_moe_w_gate.astype(BF16), l0_moe_b_gate, l0_moe_w_up.astype(BF16),
                  l0_moe_b_up, l0_moe_w_down.astype(BF16), l0_moe_b_down, tok_rows_per_batch=Lt, batch_stride=Lt,
                  row_offset=0)
    (xa,) = _rows(xa, [ys] * TOP_K, [5] * TOP_K, mod0, add_row_offsets=[k * M for k in range(TOP_K)],
                  tr=ROW_TILE // 2, name="l0_moe_residual", **tiles)

    mod1 = _ada_mod(c, c_ctx, l1_ada_w, l1_ada_b)
    h, ts = _norm_shift(xa, mod1, l1_norm_mix, **tiles)
    rwp = dict(mix=l1_rw_mix, w_r=l1_rw_w_r, w_k=l1_rw_w_k, w_v=l1_rw_w_v, k_k=l1_rw_k_k, k_a=l1_rw_k_a,
               r_k=l1_rw_r_k, w0=l1_rw_w0, w1=l1_rw_w1, w2=l1_rw_w2, a0=l1_rw_a0, a1=l1_rw_a1, a2=l1_rw_a2,
               g1=l1_rw_g1, g2=l1_rw_g2, gn_w=l1_rw_gn_w, gn_b=l1_rw_gn_b, w_o=l1_rw_w_o)
    y = _rwkv7_time_mix(h, ts, rwp, B=B, C=C, L=L)
    rw1, rb1 = _router_params(l1_router_w, l1_router_b)
    xa, h2, logits = _rows(xa, [y], [2], mod1, norm_w=l1_norm_ffn, shift_row=3, scale_row=4, h_dtype=F32,
                           router=(rw1, rb1), name="l1_norm_ffn", **tiles)
    lat_logits = logits.reshape(B, Lt, -1)[:, C:, :E].reshape(B * L, E)
    ys = _moe_ffn(h2, lat_logits, l1_moe_w_gate.astype(BF16), l1_moe_b_gate, l1_moe_w_up.astype(BF16),
                  l1_moe_b_up, l1_moe_w_down.astype(BF16), l1_moe_b_down, tok_rows_per_batch=L, batch_stride=Lt,
                  row_offset=C)
    (out,) = _rows(xa, [ys] * TOP_K, [5] * TOP_K, mod1, add_row_offsets=[k * B * L for k in range(TOP_K)],
                   rows_per_batch=L, ctx_rows=0, x_rows_per_batch=Lt, x_row_offset=C, tr=ROW_TILE // 2,
                   name="l1_moe_residual")
    return out.reshape(B, L, D)
```

```python
import functools

import numpy as np
import jax
import jax.numpy as jnp
from jax import lax
from jax.experimental import pallas as pl
from jax.experimental.pallas import tpu as pltpu

F32 = jnp.float32
BF16 = jnp.bfloat16

GRID_W = 64
N_MOD = 6
RMS_EPS = 1e-6
NEG_INF = -1e30
NA_HEAD_DIM = 128
NA_WIN_H = 8
NA_WIN_W = 16
RW_HEAD_DIM = 64
RW_GN_EPS = 64e-5
TOP_K = 4
SWIGLU_LIMIT = 7.0
SWIGLU_ALPHA = 1.702

V7X_LANES = 128
VMEM_LIMIT_BYTES = 56 * 1024 * 1024
MM_VMEM_BUDGET_BYTES = 44 * 1024 * 1024

ROW_TILE = 256
NA_QROWS = 4
NA_KROWS = 12
NA_UNROLL = 4
MOE_TILE = 256
WKV_T = 64
WKV_BLOCK = 256
WKV_LANES = 2048
WKV_SEG = 256


def _cparams(*sem, **kw):
    return pltpu.CompilerParams(dimension_semantics=sem, vmem_limit_bytes=VMEM_LIMIT_BYTES, **kw)


def _mm_body(*refs, n_a, n_pv, n_ev, n_em, n_out, prologue, epilogue, chunk):
    a_refs = refs[:n_a]
    pv_refs = refs[n_a:n_a + n_pv]
    w_ref = refs[n_a + n_pv]
    base = n_a + n_pv + 1
    ev_refs = refs[base:base + n_ev]
    em_refs = refs[base + n_ev:base + n_ev + n_em]
    out_refs = refs[base + n_ev + n_em:base + n_ev + n_em + n_out]
    scratch = refs[base + n_ev + n_em + n_out:]
    if prologue is not None:
        a_scr = scratch[0]

        @pl.when(pl.program_id(1) == 0)
        def _():
            pvs = [r[...] for r in pv_refs]

            def body(c, carry):
                rows = pl.ds(pl.multiple_of(c * chunk, chunk), chunk)
                a_scr[rows, :] = prologue(*[r[rows, :] for r in a_refs], *pvs).astype(BF16)
                return carry

            lax.fori_loop(0, a_scr.shape[0] // chunk, body, 0)

        a = a_scr[...]
    else:
        a = a_refs[0][...]
    acc = jnp.dot(a, w_ref[...].astype(BF16), preferred_element_type=F32)
    evs = [r[...] for r in ev_refs]
    ems = [r[...] for r in em_refs]
    if epilogue is None:
        out_refs[0][...] = acc.astype(out_refs[0].dtype)
    else:
        epilogue(acc, *evs, *ems, *out_refs)


def _mm_tiles(M, K, N, *, n_a, has_prologue, a_bytes, w_bytes, out_bytes):
    for tm, tn in ((768, 1024), (512, 1024), (768, 512), (512, 512), (256, 512), (256, 256)):
        if M % tm or N % tn:
            continue
        need = 2 * n_a * tm * K * a_bytes + 2 * K * tn * w_bytes + 2 * tm * tn * out_bytes
        need += tm * K * 2 if has_prologue else 0
        if need <= MM_VMEM_BUDGET_BYTES:
            return tm, tn
    return (256 if M % 256 == 0 else M), (256 if N % 256 == 0 else N)


def _matmul(a_list, w, *, out_dtypes=(F32,), prologue=None, prologue_vecs=(), epilogue=None,
            epi_vecs=(), epi_mats=(), a_col=0, tm=None, tn=None, name="mm"):
    M = a_list[0].shape[0]
    K, N = w.shape
    if tm is None or tn is None:
        tm, tn = _mm_tiles(M, K, N, n_a=len(a_list), has_prologue=prologue is not None,
                           a_bytes=a_list[0].dtype.itemsize, w_bytes=w.dtype.itemsize,
                           out_bytes=sum(jnp.dtype(dt).itemsize for dt in out_dtypes) + 4 * len(epi_mats))
    assert M % tm == 0 and N % tn == 0
    chunk = 32 if tm % 32 == 0 else tm
    in_specs = [pl.BlockSpec((tm, K), lambda m, n: (m, a_col)) for _ in a_list]
    in_specs += [pl.BlockSpec((1, K), lambda m, n: (0, 0)) for _ in prologue_vecs]
    in_specs += [pl.BlockSpec((K, tn), lambda m, n: (0, n))]
    in_specs += [pl.BlockSpec((1, tn), lambda m, n: (0, n)) for _ in epi_vecs]
    in_specs += [pl.BlockSpec((tm, tn), lambda m, n: (m, n)) for _ in epi_mats]
    out_shape = [jax.ShapeDtypeStruct((M, N), dt) for dt in out_dtypes]
    out_specs = [pl.BlockSpec((tm, tn), lambda m, n: (m, n)) for _ in out_dtypes]
    scratch = [pltpu.VMEM((tm, K), BF16)] if prologue is not None else []
    body = functools.partial(_mm_body, n_a=len(a_list), n_pv=len(prologue_vecs), n_ev=len(epi_vecs),
                             n_em=len(epi_mats), n_out=len(out_dtypes), prologue=prologue, epilogue=epilogue,
                             chunk=chunk)
    outs = pl.pallas_call(
        body, out_shape=out_shape, grid=(M // tm, N // tn), in_specs=in_specs, out_specs=out_specs,
        scratch_shapes=scratch, compiler_params=_cparams("parallel", "arbitrary"), name=name,
    )(*a_list, *prologue_vecs, w, *epi_vecs, *epi_mats)
    return outs[0] if len(out_dtypes) == 1 else outs


def _rows_body(*refs, n_add, gate_rows, ctx_src_tiles, do_norm, shift_row, scale_row, has_router, write_x, chunk):
    x_ref = refs[0]
    n_src = 1 if ctx_src_tiles is None else 2
    refs = refs[n_src - 1:]
    ctx_ref = refs[0]
    add_refs = refs[1:1 + n_add]
    mod_ref = refs[1 + n_add]
    pos = 2 + n_add
    if do_norm:
        nw_ref = refs[pos]
        pos += 1
    if has_router:
        rw_ref, rb_ref = refs[pos], refs[pos + 1]
        pos += 2
    outs = refs[pos:]
    oi = 0
    if write_x:
        xo_ref = outs[oi]
        oi += 1
    if do_norm:
        h_ref = outs[oi]
        oi += 1
    if has_router:
        lg_ref = outs[oi]
    gates = [mod_ref[0, g:g + 1, :] for g in gate_rows]
    if do_norm:
        nw = nw_ref[...]
        sh = mod_ref[0, shift_row:shift_row + 1, :]
        sc = 1.0 + mod_ref[0, scale_row:scale_row + 1, :]

    def run(src_ref):
        def body(c, carry):
            rows = pl.ds(pl.multiple_of(c * chunk, chunk), chunk)
            x = src_ref[rows, :]
            for a_ref, g in zip(add_refs, gates):
                x = x + g * a_ref[rows, :].astype(F32)
            if write_x:
                xo_ref[rows, :] = x
            if do_norm:
                y = x * lax.rsqrt(jnp.mean(x * x, axis=-1, keepdims=True) + RMS_EPS)
                h_ref[rows, :] = ((y * nw) * sc + sh).astype(h_ref.dtype)
            return carry

        lax.fori_loop(0, src_ref.shape[0] // chunk, body, 0)

    if ctx_src_tiles is None:
        run(x_ref)
    else:
        tiles_per_batch, ctx_tiles = ctx_src_tiles
        is_ctx = lax.rem(pl.program_id(0), tiles_per_batch) < ctx_tiles
        pl.when(is_ctx)(functools.partial(run, ctx_ref))
        pl.when(jnp.logical_not(is_ctx))(functools.partial(run, x_ref))
    if has_router:
        lg_ref[...] = jnp.dot(h_ref[...].astype(BF16), rw_ref[...], preferred_element_type=F32) + rb_ref[...]


def _rows(x, adds, gate_rows, mod, *, rows_per_batch, ctx_rows, x_rows_per_batch=None, x_row_offset=0, x_ctx=None,
          add_row_offsets=None, norm_w=None, shift_row=0, scale_row=1, h_dtype=None, router=None, write_x=True,
          tr=ROW_TILE, name="rows"):
    D = x.shape[1]
    B = mod.shape[0] // 2
    M = B * rows_per_batch
    x_rows_per_batch = x_rows_per_batch or rows_per_batch
    assert rows_per_batch % tr == 0 and ctx_rows % tr == 0 and x_rows_per_batch % tr == 0 and x_row_offset % tr == 0
    tiles_per_batch, ctx_tiles = rows_per_batch // tr, ctx_rows // tr
    x_tiles_per_batch, x_tile_offset = x_rows_per_batch // tr, x_row_offset // tr
    add_row_offsets = list(add_row_offsets or [0] * len(adds))
    assert all(off % tr == 0 for off in add_row_offsets)
    do_norm = norm_w is not None

    def mod_idx(i):
        return ((i // tiles_per_batch) * 2 + ((i % tiles_per_batch) >= ctx_tiles).astype(jnp.int32), 0, 0)

    row_spec = pl.BlockSpec((tr, D), lambda i: (i, 0))
    x_spec = pl.BlockSpec((tr, D), lambda i: ((i // tiles_per_batch) * x_tiles_per_batch + x_tile_offset
                                               + i % tiles_per_batch, 0))
    add_specs = [pl.BlockSpec((tr, D), lambda i, t=off // tr: (t + i, 0)) for off in add_row_offsets]
    src = [x]
    if x_ctx is not None:
        lat_tiles = tiles_per_batch - ctx_tiles
        x_spec = pl.BlockSpec((tr, D), lambda i: ((i // tiles_per_batch) * lat_tiles
                                                   + jnp.maximum(i % tiles_per_batch - ctx_tiles, 0), 0))
        ctx_spec = pl.BlockSpec((tr, D), lambda i: ((i // tiles_per_batch) * ctx_tiles
                                                     + jnp.minimum(i % tiles_per_batch, ctx_tiles - 1), 0))
        src = [x, x_ctx]
    args = [*src, *adds, mod]
    in_specs = [x_spec] + ([ctx_spec] if x_ctx is not None else []) + add_specs + [pl.BlockSpec((1, 8, D), mod_idx)]
    if do_norm:
        args.append(norm_w.reshape(1, D))
        in_specs.append(pl.BlockSpec((1, D), lambda i: (0, 0)))
    if router is not None:
        rw, rb = router
        args += [rw, rb]
        in_specs += [pl.BlockSpec(rw.shape, lambda i: (0, 0)), pl.BlockSpec(rb.shape, lambda i: (0, 0))]
    out_shape, out_specs = [], []
    if write_x:
        out_shape.append(jax.ShapeDtypeStruct((M, D), F32))
        out_specs.append(row_spec)
    if do_norm:
        out_shape.append(jax.ShapeDtypeStruct((M, D), h_dtype or BF16))
        out_specs.append(row_spec)
    if router is not None:
        out_shape.append(jax.ShapeDtypeStruct((M, V7X_LANES), F32))
        out_specs.append(pl.BlockSpec((tr, V7X_LANES), lambda i: (i, 0)))
    body = functools.partial(_rows_body, n_add=len(adds), gate_rows=tuple(gate_rows),
                             ctx_src_tiles=(tiles_per_batch, ctx_tiles) if x_ctx is not None else None,
                             do_norm=do_norm, shift_row=shift_row, scale_row=scale_row,
                             has_router=router is not None, write_x=write_x, chunk=32)
    return pl.pallas_call(body, out_shape=out_shape, grid=(M // tr,), in_specs=in_specs, out_specs=out_specs,
                          compiler_params=_cparams("parallel"), name=name)(*args)


def _norm_shift_body(x_ref, xp_ref, xn_ref, mod_ref, nw_ref, h_ref, ts_ref, hs, *, tiles_per_batch, ctx_tiles, chunk):
    tr, D = x_ref.shape
    half = D // 2
    nw = nw_ref[...]
    sh = mod_ref[0, 0:1, :]
    sc = 1.0 + mod_ref[0, 1:2, :]

    def norm(x):
        return ((x * lax.rsqrt(jnp.mean(x * x, axis=-1, keepdims=True) + RMS_EPS)) * nw) * sc + sh

    j = lax.rem(pl.program_id(0), tiles_per_batch)
    first = jnp.logical_or(j == 0, j == ctx_tiles)
    last = jnp.logical_or(j == ctx_tiles - 1, j == tiles_per_batch - 1)
    hs[0:8, :] = jnp.where(first, 0.0, norm(xp_ref[...]))
    hs[tr + 8:tr + 16, :] = jnp.where(last, 0.0, norm(xn_ref[...]))

    def norm_rows(c, carry):
        rows = pl.ds(pl.multiple_of(c * chunk, chunk), chunk)
        h = norm(x_ref[rows, :])
        h_ref[rows, :] = h.astype(h_ref.dtype)
        hs[pl.ds(pl.multiple_of(c * chunk + 8, 8), chunk), :] = h
        return carry

    lax.fori_loop(0, tr // chunk, norm_rows, 0)

    def shift_rows(c, carry):
        win = pl.ds(pl.multiple_of(c * chunk, 8), chunk + 16)
        prev = pltpu.roll(hs[win, :half], 1, axis=0)[8:8 + chunk]
        nxt = pltpu.roll(hs[win, half:], chunk + 15, axis=0)[8:8 + chunk]
        rows = pl.ds(pl.multiple_of(c * chunk, chunk), chunk)
        ts_ref[rows, :] = jnp.concatenate([prev, nxt], axis=1).astype(ts_ref.dtype)
        return carry

    lax.fori_loop(0, tr // chunk, shift_rows, 0)


def _norm_shift(x, mod, norm_w, *, rows_per_batch, ctx_rows):
    M, D = x.shape
    tr = ROW_TILE
    tiles_per_batch, ctx_tiles = rows_per_batch // tr, ctx_rows // tr
    assert rows_per_batch % tr == 0 and ctx_rows % tr == 0 and ctx_tiles >= 1
    sub = tr // 8

    def mod_idx(i):
        return ((i // tiles_per_batch) * 2 + ((i % tiles_per_batch) >= ctx_tiles).astype(jnp.int32), 0, 0)

    row_spec = pl.BlockSpec((tr, D), lambda i: (i, 0))
    body = functools.partial(_norm_shift_body, tiles_per_batch=tiles_per_batch, ctx_tiles=ctx_tiles, chunk=32)
    return pl.pallas_call(
        body, out_shape=[jax.ShapeDtypeStruct((M, D), BF16)] * 2, grid=(M // tr,),
        in_specs=[row_spec,
                  pl.BlockSpec((8, D), lambda i: (jnp.maximum(i * sub - 1, 0), 0)),
                  pl.BlockSpec((8, D), lambda i: (jnp.minimum((i + 1) * sub, M // 8 - 1), 0)),
                  pl.BlockSpec((1, 8, D), mod_idx), pl.BlockSpec((1, D), lambda i: (0, 0))],
        out_specs=[row_spec, row_spec], scratch_shapes=[pltpu.VMEM((tr + 16, D), F32)],
        compiler_params=_cparams("parallel"), name="norm_shift")(x, x, x, mod, norm_w.reshape(1, D))


def _ada_mod(c, c_ctx, ada_w, ada_b):
    B, D = c.shape
    cond = jnp.concatenate([c, c_ctx[None, :], jnp.zeros((8 - B - 1, D), F32)], axis=0)

    def epi(acc, bias, out_ref):
        out_ref[...] = acc + bias

    out = _matmul([cond], ada_w, prologue=lambda a: a * jax.nn.sigmoid(a), epilogue=epi,
                  epi_vecs=[ada_b.reshape(1, -1)], tm=8, tn=512, name="ada_mod")
    lat = out[:B].reshape(B, N_MOD, D)
    ctx = jnp.broadcast_to(out[B].reshape(1, N_MOD, D), (B, N_MOD, D))
    mod = jnp.stack([ctx, lat], axis=1).reshape(2 * B, N_MOD, D)
    return jnp.pad(mod, ((0, 0), (0, 8 - N_MOD), (0, 0)))


def _qkv_epilogue(acc, gain, out_ref, *, n_norm_tiles):
    tn = acc.shape[1]

    @pl.when(pl.program_id(1) < n_norm_tiles)
    def _():
        for i in range(tn // NA_HEAD_DIM):
            sl = slice(i * NA_HEAD_DIM, (i + 1) * NA_HEAD_DIM)
            z = acc[:, sl]
            z = z * lax.rsqrt(jnp.mean(z * z, axis=-1, keepdims=True) + RMS_EPS)
            out_ref[:, sl] = (z * gain[:, sl]).astype(out_ref.dtype)

    @pl.when(pl.program_id(1) >= n_norm_tiles)
    def _():
        out_ref[...] = acc.astype(out_ref.dtype)


def _na_bias_tables(rpb, rows):
    nblk = rows // NA_QROWS
    kh = min(NA_WIN_H, rows)
    assert kh == NA_WIN_H and rows >= NA_KROWS + NA_QROWS and rows % NA_QROWS == 0
    col = np.arange(GRID_W)
    col_start = np.clip(col - NA_WIN_W // 2, 0, GRID_W - NA_WIN_W)
    col_ok = (col[None, :] >= col_start[:, None]) & (col[None, :] < col_start[:, None] + NA_WIN_W)
    dc = np.clip(col[None, :] - col[:, None] + NA_WIN_W - 1, 0, 2 * NA_WIN_W - 2)
    tabs = []
    for j in (0, 1, nblk - 1):
        s0 = int(np.clip(NA_QROWS * j - NA_WIN_H // 2, 0, rows - NA_KROWS))
        qr = NA_QROWS * j + np.arange(NA_QROWS)
        rs = np.clip(qr - kh // 2, 0, rows - kh)
        kr = s0 + np.arange(NA_KROWS)
        row_ok = (kr[None, :] >= rs[:, None]) & (kr[None, :] < rs[:, None] + kh)
        dr = np.clip(kr[None, :] - qr[:, None] + NA_WIN_H - 1, 0, 2 * NA_WIN_H - 2)
        ok = row_ok[:, None, :, None] & col_ok[None, :, None, :]
        ok_f = ok.reshape(NA_QROWS * GRID_W, NA_KROWS * GRID_W)
        by_row = rpb.astype(F32)[:, dr.reshape(-1), :]
        spread = np.zeros((2 * NA_WIN_W - 1, GRID_W * GRID_W), np.float32)
        spread[dc.reshape(-1), np.arange(GRID_W * GRID_W)] = 1.0
        full = jnp.einsum("hpc,cq->hpq", by_row, jnp.asarray(spread), precision=lax.Precision.HIGHEST)
        full = full.reshape(-1, NA_QROWS, NA_KROWS, GRID_W, GRID_W).transpose(0, 1, 3, 2, 4)
        tabs.append(jnp.where(ok_f[None], full.reshape(-1, NA_QROWS * GRID_W, NA_KROWS * GRID_W), NEG_INF))
    return jnp.stack(tabs, axis=0)


def _na_body(q_ref, k_ref, v_ref, bias_ref, o_ref, *, C, rows):
    nq = NA_QROWS * GRID_W
    nk = NA_KROWS * GRID_W
    nblk = rows // NA_QROWS
    dn = (((1,), (1,)), ((), ()))
    kc = k_ref[0:C, :]
    vc = v_ref[0:C, :]
    s = lax.dot_general(q_ref[0:C, :], kc, dn, preferred_element_type=F32)
    p = jnp.exp(s - jnp.max(s, axis=-1, keepdims=True))
    o = jnp.dot(p.astype(BF16), vc, preferred_element_type=F32) / jnp.sum(p, axis=-1, keepdims=True)
    o_ref[0:C, :] = o.astype(o_ref.dtype)

    def body(jj, carry):
        js = [jj * NA_UNROLL + t for t in range(NA_UNROLL)]
        s0 = [jnp.clip(NA_QROWS * j - NA_WIN_H // 2, 0, rows - NA_KROWS) for j in js]
        cls = [jnp.where(j == 0, 0, jnp.where(j == nblk - 1, 2, 1)) for j in js]
        qrows = [pl.ds(pl.multiple_of(C + j * nq, GRID_W), nq) for j in js]
        krows = [pl.ds(pl.multiple_of(C + s * GRID_W, GRID_W), nk) for s in s0]
        qb = [q_ref[r, :] for r in qrows]
        s_lat = [lax.dot_general(q, k_ref[r, :], dn, preferred_element_type=F32) + bias_ref[c, 0]
                 for q, r, c in zip(qb, krows, cls)]
        s_ctx = [lax.dot_general(q, kc, dn, preferred_element_type=F32) for q in qb]
        m = [jnp.maximum(jnp.max(a, axis=-1, keepdims=True), jnp.max(b, axis=-1, keepdims=True))
             for a, b in zip(s_lat, s_ctx)]
        p_lat = [jnp.exp(a - mm) for a, mm in zip(s_lat, m)]
        p_ctx = [jnp.exp(b - mm) for b, mm in zip(s_ctx, m)]
        den = [jnp.sum(a, axis=-1, keepdims=True) + jnp.sum(b, axis=-1, keepdims=True)
               for a, b in zip(p_lat, p_ctx)]
        o = [jnp.dot(b.astype(BF16), vc, preferred_element_type=F32)
             + jnp.dot(a.astype(BF16), v_ref[r, :], preferred_element_type=F32)
             for a, b, r in zip(p_lat, p_ctx, krows)]
        for r, oo, dd in zip(qrows, o, den):
            o_ref[r, :] = (oo / dd).astype(o_ref.dtype)
        return carry

    assert nblk % NA_UNROLL == 0
    lax.fori_loop(0, nblk // NA_UNROLL, body, 0)


def _neighbourhood_attention(qkv, bias_tabs, *, B, C, L, D):
    H = D // NA_HEAD_DIM
    Lt = C + L
    rows = L // GRID_W
    blk = (Lt, NA_HEAD_DIM)
    body = functools.partial(_na_body, C=C, rows=rows)
    return pl.pallas_call(
        body, out_shape=jax.ShapeDtypeStruct((B * Lt, D), BF16), grid=(B, H),
        in_specs=[pl.BlockSpec(blk, lambda b, h: (b, h)),
                  pl.BlockSpec(blk, lambda b, h: (b, H + h)),
                  pl.BlockSpec(blk, lambda b, h: (b, 2 * H + h)),
                  pl.BlockSpec((3, 1) + bias_tabs.shape[2:], lambda b, h: (0, h, 0, 0))],
        out_specs=pl.BlockSpec(blk, lambda b, h: (b, h)),
        compiler_params=_cparams("parallel", "parallel"), name="na_attention",
    )(qkv, qkv, qkv, bias_tabs)


def _moe_body(be_ref, na_ref, tok0_ref, tokn_ref, slot_ref, h_hbm, w_ref, wg_ref, bg_ref, wu_ref, bu_ref, wd_ref,
              bd_ref, ys_hbm, xbuf, ybuf, gsem, ssem):
    del be_ref
    tm = xbuf.shape[1]
    i = pl.program_id(0)
    n_act = na_ref[0]
    cur = lax.rem(i, 2)

    def start_gather(tok_ref, buf):
        for r in range(tm):
            pltpu.make_async_copy(h_hbm.at[pl.ds(tok_ref[0, 0, r], 1)], xbuf.at[buf, pl.ds(r, 1)],
                                  gsem.at[buf]).start()

    def wait_gather(buf):
        pltpu.make_async_copy(h_hbm.at[pl.ds(0, tm)], xbuf.at[buf], gsem.at[buf]).wait()

    def wait_scatter(buf):
        pltpu.make_async_copy(ybuf.at[buf], ys_hbm.at[pl.ds(0, tm)], ssem.at[buf]).wait()

    @pl.when(i == 0)
    def _():
        start_gather(tok0_ref, 0)

    @pl.when(i < n_act)
    def _():
        start_gather(tokn_ref, 1 - cur)
        wait_gather(cur)
        x = xbuf[cur].astype(BF16)
        g = jnp.minimum(jnp.dot(x, wg_ref[0], preferred_element_type=F32) + bg_ref[0], SWIGLU_LIMIT)
        u = jnp.clip(jnp.dot(x, wu_ref[0], preferred_element_type=F32) + bu_ref[0], -SWIGLU_LIMIT, SWIGLU_LIMIT)
        hdn = g * jax.nn.sigmoid(SWIGLU_ALPHA * g) * (u + 1.0)
        y = jnp.dot(hdn.astype(BF16), wd_ref[0], preferred_element_type=F32) + bd_ref[0]

        @pl.when(i >= 2)
        def _():
            wait_scatter(cur)

        ybuf[cur] = y * w_ref[...]

        for r in range(tm):
            pltpu.make_async_copy(ybuf.at[cur, pl.ds(r, 1)], ys_hbm.at[pl.ds(slot_ref[0, 0, r], 1)],
                                  ssem.at[cur]).start()

    @pl.when(i == n_act - 1)
    def _():
        wait_scatter(1 - cur)
        wait_scatter(cur)
        wait_gather(1 - cur)

    @pl.when(i == n_act)
    def _():
        ybuf[0] = jnp.zeros(ybuf.shape[1:], ybuf.dtype)

    @pl.when(i >= n_act)
    def _():
        fill = pltpu.make_async_copy(ybuf.at[0], ys_hbm.at[pl.ds(pl.multiple_of(i * tm, tm), tm)], ssem.at[0])
        fill.start()
        fill.wait()


def _moe_experts(h_all, row_src, row_slot, row_w, blk_exp, n_act, wg, bg, wu, bu, wd, bd):
    M, D = h_all.shape
    E, _, De = wg.shape
    tm = MOE_TILE
    n_rows = row_src.shape[0]
    n_blocks = n_rows // tm
    assert n_blocks >= 2
    smem_blk = lambda idx: pl.BlockSpec((1, 1, tm), idx, memory_space=pltpu.SMEM)
    grid_spec = pltpu.PrefetchScalarGridSpec(
        num_scalar_prefetch=2, grid=(n_blocks,),
        in_specs=[smem_blk(lambda i, be, na: (0, 0, 0)),
                  smem_blk(lambda i, be, na: (jnp.minimum(i + 1, n_blocks - 1), 0, 0)),
                  smem_blk(lambda i, be, na: (i, 0, 0)),
                  pl.BlockSpec(memory_space=pl.ANY),
                  pl.BlockSpec((tm, 1), lambda i, be, na: (i, 0)),
                  pl.BlockSpec((1, D, De), lambda i, be, na: (be[i], 0, 0)),
                  pl.BlockSpec((1, 1, De), lambda i, be, na: (be[i], 0, 0)),
                  pl.BlockSpec((1, D, De), lambda i, be, na: (be[i], 0, 0)),
                  pl.BlockSpec((1, 1, De), lambda i, be, na: (be[i], 0, 0)),
                  pl.BlockSpec((1, De, D), lambda i, be, na: (be[i], 0, 0)),
                  pl.BlockSpec((1, 1, D), lambda i, be, na: (be[i], 0, 0))],
        out_specs=pl.BlockSpec(memory_space=pl.ANY),
        scratch_shapes=[pltpu.VMEM((2, tm, D), F32), pltpu.VMEM((2, tm, D), F32),
                        pltpu.SemaphoreType.DMA((2,)), pltpu.SemaphoreType.DMA((2,))])
    src3 = row_src.reshape(n_blocks, 1, tm)
    return pl.pallas_call(
        _moe_body, out_shape=jax.ShapeDtypeStruct((n_rows, D), F32), grid_spec=grid_spec,
        compiler_params=_cparams("arbitrary", disable_bounds_checks=True), name="moe_experts",
    )(blk_exp, n_act, src3, src3, row_slot.reshape(n_blocks, 1, tm), h_all, row_w.reshape(n_rows, 1), wg,
      bg.reshape(E, 1, De), wu, bu.reshape(E, 1, De), wd, bd.reshape(E, 1, D))


def _moe_ffn(h_all, logits, wg, bg, wu, bu, wd, bd, *, tok_rows_per_batch, batch_stride, row_offset):
    N = logits.shape[0]
    E = wg.shape[0]
    tm = MOE_TILE
    top_val, top_idx = lax.top_k(logits, TOP_K)
    top_w = jax.nn.softmax(top_val, axis=-1)
    flat_e = top_idx.reshape(-1)
    onehot = (flat_e[:, None] == jnp.arange(E, dtype=flat_e.dtype)[None, :]).astype(jnp.int32)
    csum = jnp.cumsum(onehot, axis=0)
    counts = csum[-1]
    rank = jnp.take_along_axis(csum, flat_e[:, None], axis=1)[:, 0] - 1
    padded = (counts + tm - 1) // tm * tm
    pad_end = jnp.cumsum(padded)
    pad_start = pad_end - padded
    dest = pad_start[flat_e] + rank
    n_blocks = (N * TOP_K + tm - 1) // tm + E
    n_rows = n_blocks * tm
    asg = jnp.arange(N * TOP_K, dtype=jnp.int32)
    packed = jnp.stack([asg + 1, lax.bitcast_convert_type(top_w.reshape(-1), jnp.int32)], axis=1)
    rows = jnp.zeros((n_rows, 2), jnp.int32).at[dest].set(packed)
    row_asg = rows[:, 0] - 1
    is_pad = row_asg < 0
    tok = jnp.maximum(row_asg, 0) // TOP_K
    row_src = (tok // tok_rows_per_batch) * batch_stride + row_offset + tok % tok_rows_per_batch
    row_w = lax.bitcast_convert_type(rows[:, 1], F32)
    row_slot = jnp.where(is_pad, N * TOP_K - 1 + jnp.cumsum(is_pad.astype(jnp.int32)),
                         (row_asg % TOP_K) * N + tok)
    blk_start = jnp.arange(n_blocks, dtype=jnp.int32) * tm
    blk_exp = jnp.minimum(jnp.sum(blk_start[:, None] >= pad_end[None, :], axis=1), E - 1).astype(jnp.int32)
    n_act = (pad_end[-1:] // tm).astype(jnp.int32)
    return _moe_experts(h_all, row_src, row_slot, row_w, blk_exp, n_act, wg, bg, wu, bu, wd, bd)


def _block_id(idx, size):
    return lax.shift_right_logical(idx, int(np.log2(size)))


def _split2(x):
    hi = x.astype(BF16)
    lo = (x - hi.astype(F32)).astype(BF16)
    return hi, lo


def _dot01_lhs(m01, x):
    hi, lo = _split2(x)
    return jnp.dot(m01, hi, preferred_element_type=F32) + jnp.dot(m01, lo, preferred_element_type=F32)


def _wkv_body(r_ref, k_ref, v_ref, pw_ref, pa_ref, pg_ref, w2_ref, a2_ref, g2_ref, w0_ref, a0_ref, kk_ref, ka_ref,
              rk_ref, gw_ref, gb_ref, o_ref, st_ref, rt_s, kt_s, at_s, bt_s, cm_s, bn_s, y_s, *, rev):
    T = WKV_T
    TB, LW = r_ref.shape
    n_pairs = LW // V7X_LANES
    n_chunks = TB // T
    pw_act = jnp.tanh(pw_ref[...]).astype(BF16)
    pa_act = pa_ref[...].astype(BF16)
    pg_act = jax.nn.sigmoid(pg_ref[...]).astype(BF16)

    @pl.when(pl.program_id(2) == 0)
    def _():
        st_ref[...] = jnp.zeros_like(st_ref)

    def earlier(p, q, strict):
        lo, hi = (p, q) if rev else (q, p)
        return lo < hi if strict else lo <= hi

    seg = min(LW, WKV_SEG)
    ri = lax.broadcasted_iota(jnp.int32, (TB, TB), 0)
    ci = lax.broadcasted_iota(jnp.int32, (TB, TB), 1)
    same_chunk = _block_id(ri, T) == _block_id(ci, T)
    tri = jnp.where(same_chunk & earlier(ri, ci, False), 1.0, 0.0).astype(BF16)
    li = _block_id(lax.broadcasted_iota(jnp.int32, (seg, seg), 0), RW_HEAD_DIM)
    lj = _block_id(lax.broadcasted_iota(jnp.int32, (seg, seg), 1), RW_HEAD_DIM)
    head_ones = jnp.where(li == lj, 1.0, 0.0).astype(BF16)

    def head_sum(z):
        return jnp.dot(z.astype(BF16), head_ones, preferred_element_type=F32)

    for sg in range(LW // seg):
        ls = slice(sg * seg, (sg + 1) * seg)
        r = r_ref[:, ls]
        k = k_ref[:, ls]
        e = jax.nn.sigmoid(jnp.dot(pw_act, w2_ref[:, ls], preferred_element_type=F32) + w0_ref[:, ls]) \
            * float(np.exp(-0.5))
        a = jax.nn.sigmoid(jnp.dot(pa_act, a2_ref[:, ls], preferred_element_type=F32) + a0_ref[:, ls])
        kkv = k * kk_ref[:, ls]
        kk = kkv * lax.rsqrt(head_sum(kkv * kkv) + 1e-12)
        kd = k * (1.0 + (a - 1.0) * ka_ref[:, ls])
        be = kk * a
        cum = _dot01_lhs(tri, e)
        ecum = jnp.exp(cum)
        cm_s[:, ls] = cum
        rt_s[:, ls] = r * jnp.exp(-cum)
        kt_s[:, ls] = kd * ecum
        at_s[:, ls] = -kk * jnp.exp(e - cum)
        bt_s[:, ls] = be * ecum
        bn_s[:, ls] = head_sum(r * kd * rk_ref[:, ls]) * v_ref[:, ls]

    lane = lax.broadcasted_iota(jnp.int32, (1, V7X_LANES), 1)
    m_a = jnp.where(lane < RW_HEAD_DIM, 1.0, 0.0).astype(F32)
    m_b = 1.0 - m_a
    pr = lax.broadcasted_iota(jnp.int32, (2 * T, 2 * T), 0)
    pc = lax.broadcasted_iota(jnp.int32, (2 * T, 2 * T), 1)
    same_head = _block_id(pr, T) == _block_id(pc, T)
    incl = same_head & earlier(pr, pc, False)
    strict = same_head & earlier(pr, pc, True)
    hr = _block_id(lax.broadcasted_iota(jnp.int32, (V7X_LANES, V7X_LANES), 0), RW_HEAD_DIM)
    hc = _block_id(lax.broadcasted_iota(jnp.int32, (V7X_LANES, V7X_LANES), 1), RW_HEAD_DIM)
    head_blk = hr == hc
    dn_nt = (((1,), (1,)), ((), ()))
    n_steps = int(np.log2(T))

    def stack(x):
        return jnp.concatenate([x * m_a, x * m_b], axis=0)

    def chunk_body(it, carry):
        c = (n_chunks - 1 - it) if rev else it
        rows = pl.ds(pl.multiple_of(c * T, T), T)
        rows_last = pl.ds(pl.multiple_of(c * T + (0 if rev else T - 8), 8), 8)
        last = slice(0, 1) if rev else slice(7, 8)
        lanes = [slice(p * V7X_LANES, (p + 1) * V7X_LANES) for p in range(n_pairs)]
        pairs = range(n_pairs)
        st = [st_ref[p] for p in pairs]
        st_b = [z.astype(BF16) for z in st]
        v = [v_ref[rows, ln] for ln in lanes]
        r_ms = [stack(rt_s[rows, ln]).astype(BF16) for ln in lanes]
        a_ms = [stack(at_s[rows, ln]).astype(BF16) for ln in lanes]
        k_ms = [stack(kt_s[rows, ln]).astype(BF16) for ln in lanes]
        b_ms = [stack(bt_s[rows, ln]).astype(BF16) for ln in lanes]
        v_ms = [stack(z).astype(BF16) for z in v]
        mm = [lax.dot_general(jnp.concatenate([r_ms[p], a_ms[p]], axis=0),
                              jnp.concatenate([k_ms[p], b_ms[p]], axis=0), dn_nt, preferred_element_type=F32)
              for p in pairs]
        m_rk = [jnp.where(incl, z[:2 * T, :2 * T], 0.0).astype(BF16) for z in mm]
        m_rb = [jnp.where(incl, z[:2 * T, 2 * T:], 0.0).astype(BF16) for z in mm]
        m_ak = [jnp.where(strict, z[2 * T:, :2 * T], 0.0).astype(BF16) for z in mm]
        nmat = [jnp.where(strict, z[2 * T:, 2 * T:], 0.0) for z in mm]
        x = [jnp.dot(jnp.concatenate([a_ms[p], m_ak[p]], axis=1), jnp.concatenate([st_b[p], v_ms[p]], axis=0),
                     preferred_element_type=F32) for p in pairs]
        for s in range(n_steps):
            nb = [z.astype(BF16) for z in nmat]
            if s < n_steps - 1:
                nx = [jnp.dot(nb[p], jnp.concatenate([nb[p], x[p].astype(BF16)], axis=1),
                              preferred_element_type=F32) for p in pairs]
                nmat = [z[:, :2 * T] for z in nx]
                x = [x[p] + nx[p][:, 2 * T:] for p in pairs]
            else:
                x = [x[p] + jnp.dot(nb[p], x[p].astype(BF16), preferred_element_type=F32) for p in pairs]
        y_stk = [jnp.dot(jnp.concatenate([r_ms[p], m_rk[p], m_rb[p]], axis=1),
                         jnp.concatenate([st_b[p], v_ms[p], x[p].astype(BF16)], axis=0),
                         preferred_element_type=F32) for p in pairs]
        w_end = [jnp.exp(-cm_s[rows_last, ln][last]) for ln in lanes]
        upd = []
        for p, ln in enumerate(lanes):
            u = x[p][:T] + x[p][T:]
            kbw_t = (jnp.concatenate([kt_s[rows, ln], bt_s[rows, ln]], axis=0) * w_end[p]).T
            upd.append(jnp.dot(kbw_t.astype(BF16), jnp.concatenate([v[p], u], axis=0).astype(BF16),
                               preferred_element_type=F32))
        for p, ln in enumerate(lanes):
            y_s[rows, ln] = y_stk[p][:T] + y_stk[p][T:]
            w_col = jnp.broadcast_to(w_end[p], (V7X_LANES, V7X_LANES)).T
            st_ref[p] = w_col * st[p] + jnp.where(head_blk, upd[p], 0.0)
        return carry

    lax.fori_loop(0, n_chunks, chunk_body, 0)

    inv_n = 1.0 / RW_HEAD_DIM
    for sg in range(LW // seg):
        ls = slice(sg * seg, (sg + 1) * seg)
        y = y_s[:, ls]
        yc = y - head_sum(y) * inv_n
        var = head_sum(yc * yc) * inv_n
        yn = yc * lax.rsqrt(var + RW_GN_EPS) * gw_ref[:, ls] + gb_ref[:, ls]
        gate = jnp.dot(pg_act, g2_ref[:, ls], preferred_element_type=F32)
        o_ref[:, ls] = ((yn + bn_s[:, ls]) * gate).astype(o_ref.dtype)


def _wkv(r, k, v, pw, pa, pg, w2, a2, g2, w0, a0, k_k, k_a, r_k, gn_w, gn_b, *, B, C, L, d):
    M, D = r.shape
    Lt = C + L
    rev = d == 1
    TB, LW = WKV_BLOCK, min(WKV_LANES, D)
    assert C % TB == 0 and L % TB == 0 and D % LW == 0
    nb, ncb = Lt // TB, C // TB

    def tblk(b, i):
        if not rev:
            return b * nb + i
        return b * nb + jnp.where(i < ncb, ncb - 1 - i, nb - 1 - i + ncb)

    blk = pl.BlockSpec((TB, LW), lambda b, gi, i: (tblk(b, i), gi))
    vec = pl.BlockSpec((1, LW), lambda b, gi, i: (0, gi))
    low = lambda w: pl.BlockSpec((TB, w.shape[0]), lambda b, gi, i: (tblk(b, i), d))
    up = lambda w: pl.BlockSpec((w.shape[0], LW), lambda b, gi, i: (0, gi))
    tbuf = pltpu.VMEM((TB, LW), F32)
    row = lambda z: z.reshape(1, D)
    return pl.pallas_call(
        functools.partial(_wkv_body, rev=rev), out_shape=jax.ShapeDtypeStruct((M, D), BF16),
        grid=(B, D // LW, nb),
        in_specs=[blk] * 3 + [low(w2), low(a2), low(g2), up(w2), up(a2), up(g2)] + [vec] * 7,
        out_specs=blk,
        scratch_shapes=[pltpu.VMEM((LW // V7X_LANES, V7X_LANES, V7X_LANES), F32)] + [tbuf] * 7,
        compiler_params=_cparams("parallel", "parallel", "arbitrary"), name="wkv7_bwd" if rev else "wkv7_fwd",
    )(r, k, v, pw, pa, pg, w2, a2, g2, row(w0), row(a0), row(k_k), row(k_a), row(r_k), row(gn_w), row(gn_b))


def _lerp(t, ts, mix):
    tf = t.astype(F32)
    return tf + (ts.astype(F32) - tf) * mix


def _rwkv7_time_mix(h, ts, p, *, B, C, L):
    M, D = h.shape
    mix = p["mix"]
    proj = lambda j, w, nm: _matmul([h, ts], w, prologue=_lerp, prologue_vecs=[mix[j].reshape(1, D)], name=nm)
    r = proj(0, p["w_r"].astype(BF16), "rw_r")
    k = proj(2, p["w_k"].astype(BF16), "rw_k")
    v = proj(3, p["w_v"].astype(BF16), "rw_v")
    lg = p["g1"].shape[2]
    lgp = -(-lg // V7X_LANES) * V7X_LANES
    w1c = jnp.concatenate([p["w1"][0], p["w1"][1]], axis=1).astype(BF16)
    a1c = jnp.concatenate([p["a1"][0], p["a1"][1]], axis=1).astype(BF16)
    g1p = jnp.pad(p["g1"], ((0, 0), (0, 0), (0, lgp - lg)))
    g1c = jnp.concatenate([g1p[0], g1p[1]], axis=1).astype(BF16)
    g2p = jnp.pad(p["g2"], ((0, 0), (0, lgp - lg), (0, 0))).astype(BF16)
    pw = proj(1, w1c, "rw_w1")
    pa = proj(4, a1c, "rw_a1")
    pg = proj(5, g1c, "rw_g1")

    o2 = [_wkv(r, k, v, pw, pa, pg, p["w2"][d].astype(BF16), p["a2"][d].astype(BF16), g2p[d], p["w0"][d],
               p["a0"][d], p["k_k"], p["k_a"], p["r_k"], p["gn_w"], p["gn_b"], B=B, C=C, L=L, d=d)
          for d in range(2)]
    return _matmul([o2[0], o2[1]], p["w_o"].astype(BF16),
                   prologue=lambda a, b: a.astype(F32) + b.astype(F32), name="rw_o")


def _router_params(router_w, router_b):
    D, E = router_w.shape
    rw = jnp.pad(router_w, ((0, 0), (0, V7X_LANES - E))).astype(BF16)
    rb = jnp.pad(router_b, (0, V7X_LANES - E)).reshape(1, V7X_LANES)
    return rw, rb


def kernel(x, c, ctx, c_ctx, l0_ada_w, l0_ada_b, l0_norm_mix, l0_norm_ffn, l0_na_w_qkv, l0_na_q_gain, l0_na_k_gain, l0_na_rpb, l0_na_w_o, l0_router_w, l0_router_b, l0_moe_w_gate, l0_moe_b_gate, l0_moe_w_up, l0_moe_b_up, l0_moe_w_down, l0_moe_b_down, l1_ada_w, l1_ada_b, l1_norm_mix, l1_norm_ffn, l1_rw_mix, l1_rw_w_r, l1_rw_w_k, l1_rw_w_v, l1_rw_k_k, l1_rw_k_a, l1_rw_r_k, l1_rw_w0, l1_rw_w1, l1_rw_w2, l1_rw_a0, l1_rw_a1, l1_rw_a2, l1_rw_g1, l1_rw_g2, l1_rw_gn_w, l1_rw_gn_b, l1_rw_w_o, l1_router_w, l1_router_b, l1_moe_w_gate, l1_moe_b_gate, l1_moe_w_up, l1_moe_b_up, l1_moe_w_down, l1_moe_b_down):
    B, L, D = x.shape
    C = ctx.shape[1]
    Lt = C + L
    M = B * Lt
    H = D // NA_HEAD_DIM
    E = l0_router_w.shape[1]
    assert C % ROW_TILE == 0 and L % ROW_TILE == 0 and L % GRID_W == 0
    tiles = dict(rows_per_batch=Lt, ctx_rows=C)

    mod0 = _ada_mod(c, c_ctx, l0_ada_w, l0_ada_b)
    xa, h = _rows(x.reshape(B * L, D), [], [], mod0, x_ctx=ctx.reshape(B * C, D), norm_w=l0_norm_mix, shift_row=0,
                  scale_row=1, name="l0_norm_mix", **tiles)
    scale = NA_HEAD_DIM ** -0.5
    gain = jnp.concatenate([jnp.tile(l0_na_q_gain * scale, H), jnp.tile(l0_na_k_gain, H), jnp.ones((D,), F32)])
    tm, tn = _mm_tiles(M, D, 3 * D, n_a=1, has_prologue=False, a_bytes=2, w_bytes=2, out_bytes=2)
    assert D % tn == 0 and tn % NA_HEAD_DIM == 0
    qkv = _matmul([h], l0_na_w_qkv.astype(BF16), out_dtypes=(BF16,),
                  epilogue=functools.partial(_qkv_epilogue, n_norm_tiles=2 * D // tn),
                  epi_vecs=[gain.reshape(1, 3 * D)], tm=tm, tn=tn, name="na_qkv")
    bias_tabs = _na_bias_tables(l0_na_rpb, L // GRID_W)
    att = _neighbourhood_attention(qkv, bias_tabs, B=B, C=C, L=L, D=D)
    y = _matmul([att], l0_na_w_o.astype(BF16), name="na_o")
    rw0, rb0 = _router_params(l0_router_w, l0_router_b)
    xa, h2, logits = _rows(xa, [y], [2], mod0, norm_w=l0_norm_ffn, shift_row=3, scale_row=4, h_dtype=F32,
                           router=(rw0, rb0), name="l0_norm_ffn", **tiles)
    ys = _moe_ffn(h2, logits[:, :E], l0_moe_w_gate.astype(BF16), l0_moe_b_gate, l0_moe_w_up.astype(BF16),
                  l0_moe_b_up, l0_moe_w_down.astype(BF16), l0_moe_b_down, tok_rows_per_batch=Lt, batch_stride=Lt,
                  row_offset=0)
    (xa,) = _rows(xa, [ys] * TOP_K, [5] * TOP_K, mod0, add_row_offsets=[k * M for k in range(TOP_K)],
                  tr=ROW_TILE // 2, name="l0_moe_residual", **tiles)

    mod1 = _ada_mod(c, c_ctx, l1_ada_w, l1_ada_b)
    h, ts = _norm_shift(xa, mod1, l1_norm_mix, **tiles)
    rwp = dict(mix=l1_rw_mix, w_r=l1_rw_w_r, w_k=l1_rw_w_k, w_v=l1_rw_w_v, k_k=l1_rw_k_k, k_a=l1_rw_k_a,
               r_k=l1_rw_r_k, w0=l1_rw_w0, w1=l1_rw_w1, w2=l1_rw_w2, a0=l1_rw_a0, a1=l1_rw_a1, a2=l1_rw_a2,
               g1=l1_rw_g1, g2=l1_rw_g2, gn_w=l1_rw_gn_w, gn_b=l1_rw_gn_b, w_o=l1_rw_w_o)
    y = _rwkv7_time_mix(h, ts, rwp, B=B, C=C, L=L)
    rw1, rb1 = _router_params(l1_router_w, l1_router_b)
    xa, h2, logits = _rows(xa, [y], [2], mod1, norm_w=l1_norm_ffn, shift_row=3, scale_row=4, h_dtype=F32,
                           router=(rw1, rb1), name="l1_norm_ffn", **tiles)
    lat_logits = logits.reshape(B, Lt, -1)[:, C:, :E].reshape(B * L, E)
    ys = _moe_ffn(h2, lat_logits, l1_moe_w_gate.astype(BF16), l1_moe_b_gate, l1_moe_w_up.astype(BF16),
                  l1_moe_b_up, l1_moe_w_down.astype(BF16), l1_moe_b_down, tok_rows_per_batch=L, batch_stride=Lt,
                  row_offset=C)
    (out,) = _rows(xa, [ys] * TOP_K, [5] * TOP_K, mod1, add_row_offsets=[k * B * L for k in range(TOP_K)],
                   rows_per_batch=L, ctx_rows=0, x_rows_per_batch=Lt, x_row_offset=C, tr=ROW_TILE // 2,
                   name="l1_moe_residual")
    return out.reshape(B, L, D)
```

```python
import functools

import numpy as np
import jax
import jax.numpy as jnp
from jax import lax
from jax.experimental import pallas as pl
from jax.experimental.pallas import tpu as pltpu

F32 = jnp.float32
BF16 = jnp.bfloat16

GRID_W = 64
N_MOD = 6
RMS_EPS = 1e-6
NEG_INF = -1e30
NA_HEAD_DIM = 128
NA_WIN_H = 8
NA_WIN_W = 16
RW_HEAD_DIM = 64
RW_GN_EPS = 64e-5
TOP_K = 4
SWIGLU_LIMIT = 7.0
SWIGLU_ALPHA = 1.702

V7X_LANES = 128
VMEM_LIMIT_BYTES = 56 * 1024 * 1024
MM_VMEM_BUDGET_BYTES = 44 * 1024 * 1024

ROW_TILE = 256
NA_QROWS = 4
NA_KROWS = 12
NA_UNROLL = 4
MOE_TILE = 256
WKV_T = 64
WKV_BLOCK = 256
WKV_LANES = 2048
WKV_SEG = 256


def _cparams(*sem, **kw):
    return pltpu.CompilerParams(dimension_semantics=sem, vmem_limit_bytes=VMEM_LIMIT_BYTES, **kw)


def _mm_body(*refs, n_a, n_pv, n_ev, n_em, n_out, prologue, group_starts, epilogue, chunk):
    a_refs = refs[:n_a]
    pv_refs = refs[n_a:n_a + n_pv]
    w_ref = refs[n_a + n_pv]
    base = n_a + n_pv + 1
    ev_refs = refs[base:base + n_ev]
    em_refs = refs[base + n_ev:base + n_ev + n_em]
    out_refs = refs[base + n_ev + n_em:base + n_ev + n_em + n_out]
    scratch = refs[base + n_ev + n_em + n_out:]
    if prologue is not None:
        a_scr = scratch[0]

        def fill(pvs):
            def body(c, carry):
                rows = pl.ds(pl.multiple_of(c * chunk, chunk), chunk)
                a_scr[rows, :] = prologue(*[r[rows, :] for r in a_refs], *pvs).astype(BF16)
                return carry

            lax.fori_loop(0, a_scr.shape[0] // chunk, body, 0)

        if group_starts is None:
            pl.when(pl.program_id(1) == 0)(lambda: fill([r[...] for r in pv_refs]))
        else:
            for g, start in enumerate(group_starts):
                pl.when(pl.program_id(1) == start)(lambda g=g: fill([pv_refs[g][...]]))
        a = a_scr[...]
    else:
        a = a_refs[0][...]
    acc = jnp.dot(a, w_ref[...].astype(BF16), preferred_element_type=F32)
    evs = [r[...] for r in ev_refs]
    ems = [r[...] for r in em_refs]
    if epilogue is None:
        out_refs[0][...] = acc.astype(out_refs[0].dtype)
    else:
        epilogue(acc, *evs, *ems, *out_refs)


def _mm_tiles(M, K, N, *, n_a, has_prologue, a_bytes, w_bytes, out_bytes):
    for tm, tn in ((768, 1024), (512, 1024), (768, 512), (512, 512), (256, 512), (256, 256)):
        if M % tm or N % tn:
            continue
        need = 2 * n_a * tm * K * a_bytes + 2 * K * tn * w_bytes + 2 * tm * tn * out_bytes
        need += tm * K * 2 if has_prologue else 0
        if need <= MM_VMEM_BUDGET_BYTES:
            return tm, tn
    return (256 if M % 256 == 0 else M), (256 if N % 256 == 0 else N)


def _matmul(a_list, w, *, out_dtypes=(F32,), prologue=None, prologue_vecs=(), group_starts=None, epilogue=None,
            epi_vecs=(), epi_mats=(), a_col=0, tm=None, tn=None, name="mm"):
    M = a_list[0].shape[0]
    K, N = w.shape
    if tm is None or tn is None:
        tm, tn = _mm_tiles(M, K, N, n_a=len(a_list), has_prologue=prologue is not None,
                           a_bytes=a_list[0].dtype.itemsize, w_bytes=w.dtype.itemsize,
                           out_bytes=sum(jnp.dtype(dt).itemsize for dt in out_dtypes) + 4 * len(epi_mats))
    assert M % tm == 0 and N % tn == 0
    chunk = 32 if tm % 32 == 0 else tm
    in_specs = [pl.BlockSpec((tm, K), lambda m, n: (m, a_col)) for _ in a_list]
    in_specs += [pl.BlockSpec((1, K), lambda m, n: (0, 0)) for _ in prologue_vecs]
    in_specs += [pl.BlockSpec((K, tn), lambda m, n: (0, n))]
    in_specs += [pl.BlockSpec((1, tn), lambda m, n: (0, n)) for _ in epi_vecs]
    in_specs += [pl.BlockSpec((tm, tn), lambda m, n: (m, n)) for _ in epi_mats]
    out_shape = [jax.ShapeDtypeStruct((M, N), dt) for dt in out_dtypes]
    out_specs = [pl.BlockSpec((tm, tn), lambda m, n: (m, n)) for _ in out_dtypes]
    scratch = [pltpu.VMEM((tm, K), BF16)] if prologue is not None else []
    body = functools.partial(_mm_body, n_a=len(a_list), n_pv=len(prologue_vecs), n_ev=len(epi_vecs),
                             n_em=len(epi_mats), n_out=len(out_dtypes), prologue=prologue,
                             group_starts=group_starts, epilogue=epilogue, chunk=chunk)
    outs = pl.pallas_call(
        body, out_shape=out_shape, grid=(M // tm, N // tn), in_specs=in_specs, out_specs=out_specs,
        scratch_shapes=scratch, compiler_params=_cparams("parallel", "arbitrary"), name=name,
    )(*a_list, *prologue_vecs, w, *epi_vecs, *epi_mats)
    return outs[0] if len(out_dtypes) == 1 else outs


def _rows_body(*refs, n_add, gate_rows, ctx_src_tiles, do_norm, shift_row, scale_row, has_router, write_x, chunk):
    x_ref = refs[0]
    n_src = 1 if ctx_src_tiles is None else 2
    refs = refs[n_src - 1:]
    ctx_ref = refs[0]
    add_refs = refs[1:1 + n_add]
    mod_ref = refs[1 + n_add]
    pos = 2 + n_add
    if do_norm:
        nw_ref = refs[pos]
        pos += 1
    if has_router:
        rw_ref, rb_ref = refs[pos], refs[pos + 1]
        pos += 2
    outs = refs[pos:]
    oi = 0
    if write_x:
        xo_ref = outs[oi]
        oi += 1
    if do_norm:
        h_ref = outs[oi]
        oi += 1
    if has_router:
        lg_ref = outs[oi]
    gates = [mod_ref[0, g:g + 1, :] for g in gate_rows]
    if do_norm:
        nw = nw_ref[...]
        sh = mod_ref[0, shift_row:shift_row + 1, :]
        sc = 1.0 + mod_ref[0, scale_row:scale_row + 1, :]

    def run(src_ref):
        def body(c, carry):
            rows = pl.ds(pl.multiple_of(c * chunk, chunk), chunk)
            x = src_ref[rows, :]
            for a_ref, g in zip(add_refs, gates):
                x = x + g * a_ref[rows, :].astype(F32)
            if write_x:
                xo_ref[rows, :] = x
            if do_norm:
                y = x * lax.rsqrt(jnp.mean(x * x, axis=-1, keepdims=True) + RMS_EPS)
                h_ref[rows, :] = ((y * nw) * sc + sh).astype(h_ref.dtype)
            return carry

        lax.fori_loop(0, src_ref.shape[0] // chunk, body, 0)

    if ctx_src_tiles is None:
        run(x_ref)
    else:
        tiles_per_batch, ctx_tiles = ctx_src_tiles
        is_ctx = lax.rem(pl.program_id(0), tiles_per_batch) < ctx_tiles
        pl.when(is_ctx)(functools.partial(run, ctx_ref))
        pl.when(jnp.logical_not(is_ctx))(functools.partial(run, x_ref))
    if has_router:
        lg_ref[...] = jnp.dot(h_ref[...].astype(BF16), rw_ref[...], preferred_element_type=F32) + rb_ref[...]


def _rows(x, adds, gate_rows, mod, *, rows_per_batch, ctx_rows, x_rows_per_batch=None, x_row_offset=0, x_ctx=None,
          add_row_offsets=None, norm_w=None, shift_row=0, scale_row=1, h_dtype=None, router=None, write_x=True,
          tr=ROW_TILE, name="rows"):
    D = x.shape[1]
    B = mod.shape[0] // 2
    M = B * rows_per_batch
    x_rows_per_batch = x_rows_per_batch or rows_per_batch
    assert rows_per_batch % tr == 0 and ctx_rows % tr == 0 and x_rows_per_batch % tr == 0 and x_row_offset % tr == 0
    tiles_per_batch, ctx_tiles = rows_per_batch // tr, ctx_rows // tr
    x_tiles_per_batch, x_tile_offset = x_rows_per_batch // tr, x_row_offset // tr
    add_row_offsets = list(add_row_offsets or [0] * len(adds))
    assert all(off % tr == 0 for off in add_row_offsets)
    do_norm = norm_w is not None

    def mod_idx(i):
        return ((i // tiles_per_batch) * 2 + ((i % tiles_per_batch) >= ctx_tiles).astype(jnp.int32), 0, 0)

    row_spec = pl.BlockSpec((tr, D), lambda i: (i, 0))
    x_spec = pl.BlockSpec((tr, D), lambda i: ((i // tiles_per_batch) * x_tiles_per_batch + x_tile_offset
                                               + i % tiles_per_batch, 0))
    add_specs = [pl.BlockSpec((tr, D), lambda i, t=off // tr: (t + i, 0)) for off in add_row_offsets]
    src = [x]
    if x_ctx is not None:
        lat_tiles = tiles_per_batch - ctx_tiles
        x_spec = pl.BlockSpec((tr, D), lambda i: ((i // tiles_per_batch) * lat_tiles
                                                   + jnp.maximum(i % tiles_per_batch - ctx_tiles, 0), 0))
        ctx_spec = pl.BlockSpec((tr, D), lambda i: ((i // tiles_per_batch) * ctx_tiles
                                                     + jnp.minimum(i % tiles_per_batch, ctx_tiles - 1), 0))
        src = [x, x_ctx]
    args = [*src, *adds, mod]
    in_specs = [x_spec] + ([ctx_spec] if x_ctx is not None else []) + add_specs + [pl.BlockSpec((1, 8, D), mod_idx)]
    if do_norm:
        args.append(norm_w.reshape(1, D))
        in_specs.append(pl.BlockSpec((1, D), lambda i: (0, 0)))
    if router is not None:
        rw, rb = router
        args += [rw, rb]
        in_specs += [pl.BlockSpec(rw.shape, lambda i: (0, 0)), pl.BlockSpec(rb.shape, lambda i: (0, 0))]
    out_shape, out_specs = [], []
    if write_x:
        out_shape.append(jax.ShapeDtypeStruct((M, D), F32))
        out_specs.append(row_spec)
    if do_norm:
        out_shape.append(jax.ShapeDtypeStruct((M, D), h_dtype or BF16))
        out_specs.append(row_spec)
    if router is not None:
        out_shape.append(jax.ShapeDtypeStruct((M, V7X_LANES), F32))
        out_specs.append(pl.BlockSpec((tr, V7X_LANES), lambda i: (i, 0)))
    body = functools.partial(_rows_body, n_add=len(adds), gate_rows=tuple(gate_rows),
                             ctx_src_tiles=(tiles_per_batch, ctx_tiles) if x_ctx is not None else None,
                             do_norm=do_norm, shift_row=shift_row, scale_row=scale_row,
                             has_router=router is not None, write_x=write_x, chunk=32)
    return pl.pallas_call(body, out_shape=out_shape, grid=(M // tr,), in_specs=in_specs, out_specs=out_specs,
                          compiler_params=_cparams("parallel"), name=name)(*args)


def _norm_shift_body(x_ref, xp_ref, xn_ref, mod_ref, nw_ref, h_ref, ts_ref, hs, *, tiles_per_batch, ctx_tiles, chunk):
    tr, D = x_ref.shape
    half = D // 2
    nw = nw_ref[...]
    sh = mod_ref[0, 0:1, :]
    sc = 1.0 + mod_ref[0, 1:2, :]

    def norm(x):
        return ((x * lax.rsqrt(jnp.mean(x * x, axis=-1, keepdims=True) + RMS_EPS)) * nw) * sc + sh

    j = lax.rem(pl.program_id(0), tiles_per_batch)
    first = jnp.logical_or(j == 0, j == ctx_tiles)
    last = jnp.logical_or(j == ctx_tiles - 1, j == tiles_per_batch - 1)
    hs[0:8, :] = jnp.where(first, 0.0, norm(xp_ref[...]))
    hs[tr + 8:tr + 16, :] = jnp.where(last, 0.0, norm(xn_ref[...]))

    def norm_rows(c, carry):
        rows = pl.ds(pl.multiple_of(c * chunk, chunk), chunk)
        h = norm(x_ref[rows, :])
        h_ref[rows, :] = h.astype(h_ref.dtype)
        hs[pl.ds(pl.multiple_of(c * chunk + 8, 8), chunk), :] = h
        return carry

    lax.fori_loop(0, tr // chunk, norm_rows, 0)

    def shift_rows(c, carry):
        win = pl.ds(pl.multiple_of(c * chunk, 8), chunk + 16)
        prev = pltpu.roll(hs[win, :half], 1, axis=0)[8:8 + chunk]
        nxt = pltpu.roll(hs[win, half:], chunk + 15, axis=0)[8:8 + chunk]
        rows = pl.ds(pl.multiple_of(c * chunk, chunk), chunk)
        ts_ref[rows, :] = jnp.concatenate([prev, nxt], axis=1).astype(ts_ref.dtype)
        return carry

    lax.fori_loop(0, tr // chunk, shift_rows, 0)


def _norm_shift(x, mod, norm_w, *, rows_per_batch, ctx_rows):
    M, D = x.shape
    tr = ROW_TILE
    tiles_per_batch, ctx_tiles = rows_per_batch // tr, ctx_rows // tr
    assert rows_per_batch % tr == 0 and ctx_rows % tr == 0 and ctx_tiles >= 1
    sub = tr // 8

    def mod_idx(i):
        return ((i // tiles_per_batch) * 2 + ((i % tiles_per_batch) >= ctx_tiles).astype(jnp.int32), 0, 0)

    row_spec = pl.BlockSpec((tr, D), lambda i: (i, 0))
    body = functools.partial(_norm_shift_body, tiles_per_batch=tiles_per_batch, ctx_tiles=ctx_tiles, chunk=32)
    return pl.pallas_call(
        body, out_shape=[jax.ShapeDtypeStruct((M, D), BF16)] * 2, grid=(M // tr,),
        in_specs=[row_spec,
                  pl.BlockSpec((8, D), lambda i: (jnp.maximum(i * sub - 1, 0), 0)),
                  pl.BlockSpec((8, D), lambda i: (jnp.minimum((i + 1) * sub, M // 8 - 1), 0)),
                  pl.BlockSpec((1, 8, D), mod_idx), pl.BlockSpec((1, D), lambda i: (0, 0))],
        out_specs=[row_spec, row_spec], scratch_shapes=[pltpu.VMEM((tr + 16, D), F32)],
        compiler_params=_cparams("parallel"), name="norm_shift")(x, x, x, mod, norm_w.reshape(1, D))


def _ada_mod(c, c_ctx, ada_w, ada_b):
    B, D = c.shape
    cond = jnp.concatenate([c, c_ctx[None, :], jnp.zeros((8 - B - 1, D), F32)], axis=0)

    def epi(acc, bias, out_ref):
        out_ref[...] = acc + bias

    out = _matmul([cond], ada_w, prologue=lambda a: a * jax.nn.sigmoid(a), epilogue=epi,
                  epi_vecs=[ada_b.reshape(1, -1)], tm=8, tn=512, name="ada_mod")
    lat = out[:B].reshape(B, N_MOD, D)
    ctx = jnp.broadcast_to(out[B].reshape(1, N_MOD, D), (B, N_MOD, D))
    mod = jnp.stack([ctx, lat], axis=1).reshape(2 * B, N_MOD, D)
    return jnp.pad(mod, ((0, 0), (0, 8 - N_MOD), (0, 0)))


def _qkv_epilogue(acc, gain, out_ref, *, n_norm_tiles):
    tn = acc.shape[1]

    @pl.when(pl.program_id(1) < n_norm_tiles)
    def _():
        for i in range(tn // NA_HEAD_DIM):
            sl = slice(i * NA_HEAD_DIM, (i + 1) * NA_HEAD_DIM)
            z = acc[:, sl]
            z = z * lax.rsqrt(jnp.mean(z * z, axis=-1, keepdims=True) + RMS_EPS)
            out_ref[:, sl] = (z * gain[:, sl]).astype(out_ref.dtype)

    @pl.when(pl.program_id(1) >= n_norm_tiles)
    def _():
        out_ref[...] = acc.astype(out_ref.dtype)


def _na_bias_tables(rpb, rows):
    nblk = rows // NA_QROWS
    kh = min(NA_WIN_H, rows)
    assert kh == NA_WIN_H and rows >= NA_KROWS + NA_QROWS and rows % NA_QROWS == 0
    col = np.arange(GRID_W)
    col_start = np.clip(col - NA_WIN_W // 2, 0, GRID_W - NA_WIN_W)
    col_ok = (col[None, :] >= col_start[:, None]) & (col[None, :] < col_start[:, None] + NA_WIN_W)
    dc = np.clip(col[None, :] - col[:, None] + NA_WIN_W - 1, 0, 2 * NA_WIN_W - 2)
    tabs = []
    for j in (0, 1, nblk - 1):
        s0 = int(np.clip(NA_QROWS * j - NA_WIN_H // 2, 0, rows - NA_KROWS))
        qr = NA_QROWS * j + np.arange(NA_QROWS)
        rs = np.clip(qr - kh // 2, 0, rows - kh)
        kr = s0 + np.arange(NA_KROWS)
        row_ok = (kr[None, :] >= rs[:, None]) & (kr[None, :] < rs[:, None] + kh)
        dr = np.clip(kr[None, :] - qr[:, None] + NA_WIN_H - 1, 0, 2 * NA_WIN_H - 2)
        ok = row_ok[:, None, :, None] & col_ok[None, :, None, :]
        ok_f = ok.reshape(NA_QROWS * GRID_W, NA_KROWS * GRID_W)
        by_row = rpb.astype(F32)[:, dr.reshape(-1), :]
        spread = np.zeros((2 * NA_WIN_W - 1, GRID_W * GRID_W), np.float32)
        spread[dc.reshape(-1), np.arange(GRID_W * GRID_W)] = 1.0
        full = jnp.einsum("hpc,cq->hpq", by_row, jnp.asarray(spread), precision=lax.Precision.HIGHEST)
        full = full.reshape(-1, NA_QROWS, NA_KROWS, GRID_W, GRID_W).transpose(0, 1, 3, 2, 4)
        tabs.append(jnp.where(ok_f[None], full.reshape(-1, NA_QROWS * GRID_W, NA_KROWS * GRID_W), NEG_INF))
    return jnp.stack(tabs, axis=0)


def _na_body(q_ref, k_ref, v_ref, bias_ref, o_ref, *, C, rows):
    nq = NA_QROWS * GRID_W
    nk = NA_KROWS * GRID_W
    nblk = rows // NA_QROWS
    dn = (((1,), (1,)), ((), ()))
    kc = k_ref[0:C, :]
    vc = v_ref[0:C, :]
    s = lax.dot_general(q_ref[0:C, :], kc, dn, preferred_element_type=F32)
    p = jnp.exp(s - jnp.max(s, axis=-1, keepdims=True))
    o = jnp.dot(p.astype(BF16), vc, preferred_element_type=F32) / jnp.sum(p, axis=-1, keepdims=True)
    o_ref[0:C, :] = o.astype(o_ref.dtype)

    def body(jj, carry):
        js = [jj * NA_UNROLL + t for t in range(NA_UNROLL)]
        s0 = [jnp.clip(NA_QROWS * j - NA_WIN_H // 2, 0, rows - NA_KROWS) for j in js]
        cls = [jnp.where(j == 0, 0, jnp.where(j == nblk - 1, 2, 1)) for j in js]
        qrows = [pl.ds(pl.multiple_of(C + j * nq, GRID_W), nq) for j in js]
        krows = [pl.ds(pl.multiple_of(C + s * GRID_W, GRID_W), nk) for s in s0]
        qb = [q_ref[r, :] for r in qrows]
        s_lat = [lax.dot_general(q, k_ref[r, :], dn, preferred_element_type=F32) + bias_ref[c, 0]
                 for q, r, c in zip(qb, krows, cls)]
        s_ctx = [lax.dot_general(q, kc, dn, preferred_element_type=F32) for q in qb]
        m = [jnp.maximum(jnp.max(a, axis=-1, keepdims=True), jnp.max(b, axis=-1, keepdims=True))
             for a, b in zip(s_lat, s_ctx)]
        p_lat = [jnp.exp(a - mm) for a, mm in zip(s_lat, m)]
        p_ctx = [jnp.exp(b - mm) for b, mm in zip(s_ctx, m)]
        den = [jnp.sum(a, axis=-1, keepdims=True) + jnp.sum(b, axis=-1, keepdims=True)
               for a, b in zip(p_lat, p_ctx)]
        o = [jnp.dot(b.astype(BF16), vc, preferred_element_type=F32)
             + jnp.dot(a.astype(BF16), v_ref[r, :], preferred_element_type=F32)
             for a, b, r in zip(p_lat, p_ctx, krows)]
        for r, oo, dd in zip(qrows, o, den):
            o_ref[r, :] = (oo / dd).astype(o_ref.dtype)
        return carry

    assert nblk % NA_UNROLL == 0
    lax.fori_loop(0, nblk // NA_UNROLL, body, 0)


def _neighbourhood_attention(qkv, bias_tabs, *, B, C, L, D):
    H = D // NA_HEAD_DIM
    Lt = C + L
    rows = L // GRID_W
    blk = (Lt, NA_HEAD_DIM)
    body = functools.partial(_na_body, C=C, rows=rows)
    return pl.pallas_call(
        body, out_shape=jax.ShapeDtypeStruct((B * Lt, D), BF16), grid=(B, H),
        in_specs=[pl.BlockSpec(blk, lambda b, h: (b, h)),
                  pl.BlockSpec(blk, lambda b, h: (b, H + h)),
                  pl.BlockSpec(blk, lambda b, h: (b, 2 * H + h)),
                  pl.BlockSpec((3, 1) + bias_tabs.shape[2:], lambda b, h: (0, h, 0, 0))],
        out_specs=pl.BlockSpec(blk, lambda b, h: (b, h)),
        compiler_params=_cparams("parallel", "parallel"), name="na_attention",
    )(qkv, qkv, qkv, bias_tabs)


def _moe_body(be_ref, na_ref, tok0_ref, tokn_ref, slot_ref, h_hbm, w_ref, wg_ref, bg_ref, wu_ref, bu_ref, wd_ref,
              bd_ref, ys_hbm, xbuf, ybuf, gsem, ssem):
    del be_ref
    tm = xbuf.shape[1]
    i = pl.program_id(0)
    n_act = na_ref[0]
    cur = lax.rem(i, 2)

    def start_gather(tok_ref, buf):
        for r in range(tm):
            pltpu.make_async_copy(h_hbm.at[pl.ds(tok_ref[0, 0, r], 1)], xbuf.at[buf, pl.ds(r, 1)],
                                  gsem.at[buf]).start()

    def wait_gather(buf):
        pltpu.make_async_copy(h_hbm.at[pl.ds(0, tm)], xbuf.at[buf], gsem.at[buf]).wait()

    def wait_scatter(buf):
        pltpu.make_async_copy(ybuf.at[buf], ys_hbm.at[pl.ds(0, tm)], ssem.at[buf]).wait()

    @pl.when(i == 0)
    def _():
        start_gather(tok0_ref, 0)

    @pl.when(i < n_act)
    def _():
        start_gather(tokn_ref, 1 - cur)
        wait_gather(cur)
        x = xbuf[cur].astype(BF16)
        g = jnp.minimum(jnp.dot(x, wg_ref[0], preferred_element_type=F32) + bg_ref[0], SWIGLU_LIMIT)
        u = jnp.clip(jnp.dot(x, wu_ref[0], preferred_element_type=F32) + bu_ref[0], -SWIGLU_LIMIT, SWIGLU_LIMIT)
        hdn = g * jax.nn.sigmoid(SWIGLU_ALPHA * g) * (u + 1.0)
        y = jnp.dot(hdn.astype(BF16), wd_ref[0], preferred_element_type=F32) + bd_ref[0]

        @pl.when(i >= 2)
        def _():
            wait_scatter(cur)

        ybuf[cur] = y * w_ref[...]

        for r in range(tm):
            pltpu.make_async_copy(ybuf.at[cur, pl.ds(r, 1)], ys_hbm.at[pl.ds(slot_ref[0, 0, r], 1)],
                                  ssem.at[cur]).start()

    @pl.when(i == n_act - 1)
    def _():
        wait_scatter(1 - cur)
        wait_scatter(cur)
        wait_gather(1 - cur)

    @pl.when(i == n_act)
    def _():
        ybuf[0] = jnp.zeros(ybuf.shape[1:], ybuf.dtype)

    @pl.when(i >= n_act)
    def _():
        fill = pltpu.make_async_copy(ybuf.at[0], ys_hbm.at[pl.ds(pl.multiple_of(i * tm, tm), tm)], ssem.at[0])
        fill.start()
        fill.wait()


def _moe_experts(h_all, row_src, row_slot, row_w, blk_exp, n_act, wg, bg, wu, bu, wd, bd):
    M, D = h_all.shape
    E, _, De = wg.shape
    tm = MOE_TILE
    n_rows = row_src.shape[0]
    n_blocks = n_rows // tm
    assert n_blocks >= 2
    smem_blk = lambda idx: pl.BlockSpec((1, 1, tm), idx, memory_space=pltpu.SMEM)
    grid_spec = pltpu.PrefetchScalarGridSpec(
        num_scalar_prefetch=2, grid=(n_blocks,),
        in_specs=[smem_blk(lambda i, be, na: (0, 0, 0)),
                  smem_blk(lambda i, be, na: (jnp.minimum(i + 1, n_blocks - 1), 0, 0)),
                  smem_blk(lambda i, be, na: (i, 0, 0)),
                  pl.BlockSpec(memory_space=pl.ANY),
                  pl.BlockSpec((tm, 1), lambda i, be, na: (i, 0)),
                  pl.BlockSpec((1, D, De), lambda i, be, na: (be[i], 0, 0)),
                  pl.BlockSpec((1, 1, De), lambda i, be, na: (be[i], 0, 0)),
                  pl.BlockSpec((1, D, De), lambda i, be, na: (be[i], 0, 0)),
                  pl.BlockSpec((1, 1, De), lambda i, be, na: (be[i], 0, 0)),
                  pl.BlockSpec((1, De, D), lambda i, be, na: (be[i], 0, 0)),
                  pl.BlockSpec((1, 1, D), lambda i, be, na: (be[i], 0, 0))],
        out_specs=pl.BlockSpec(memory_space=pl.ANY),
        scratch_shapes=[pltpu.VMEM((2, tm, D), F32), pltpu.VMEM((2, tm, D), F32),
                        pltpu.SemaphoreType.DMA((2,)), pltpu.SemaphoreType.DMA((2,))])
    src3 = row_src.reshape(n_blocks, 1, tm)
    return pl.pallas_call(
        _moe_body, out_shape=jax.ShapeDtypeStruct((n_rows, D), F32), grid_spec=grid_spec,
        compiler_params=_cparams("arbitrary", disable_bounds_checks=True), name="moe_experts",
    )(blk_exp, n_act, src3, src3, row_slot.reshape(n_blocks, 1, tm), h_all, row_w.reshape(n_rows, 1), wg,
      bg.reshape(E, 1, De), wu, bu.reshape(E, 1, De), wd, bd.reshape(E, 1, D))


def _moe_ffn(h_all, logits, wg, bg, wu, bu, wd, bd, *, tok_rows_per_batch, batch_stride, row_offset):
    N = logits.shape[0]
    E = wg.shape[0]
    tm = MOE_TILE
    top_val, top_idx = lax.top_k(logits, TOP_K)
    top_w = jax.nn.softmax(top_val, axis=-1)
    flat_e = top_idx.reshape(-1)
    onehot = (flat_e[:, None] == jnp.arange(E, dtype=flat_e.dtype)[None, :]).astype(jnp.int32)
    csum = jnp.cumsum(onehot, axis=0)
    counts = csum[-1]
    rank = jnp.take_along_axis(csum, flat_e[:, None], axis=1)[:, 0] - 1
    padded = (counts + tm - 1) // tm * tm
    pad_end = jnp.cumsum(padded)
    pad_start = pad_end - padded
    dest = pad_start[flat_e] + rank
    n_blocks = (N * TOP_K + tm - 1) // tm + E
    n_rows = n_blocks * tm
    asg = jnp.arange(N * TOP_K, dtype=jnp.int32)
    packed = jnp.stack([asg + 1, lax.bitcast_convert_type(top_w.reshape(-1), jnp.int32)], axis=1)
    rows = jnp.zeros((n_rows, 2), jnp.int32).at[dest].set(packed)
    row_asg = rows[:, 0] - 1
    is_pad = row_asg < 0
    tok = jnp.maximum(row_asg, 0) // TOP_K
    row_src = (tok // tok_rows_per_batch) * batch_stride + row_offset + tok % tok_rows_per_batch
    row_w = lax.bitcast_convert_type(rows[:, 1], F32)
    row_slot = jnp.where(is_pad, N * TOP_K - 1 + jnp.cumsum(is_pad.astype(jnp.int32)),
                         (row_asg % TOP_K) * N + tok)
    blk_start = jnp.arange(n_blocks, dtype=jnp.int32) * tm
    blk_exp = jnp.minimum(jnp.sum(blk_start[:, None] >= pad_end[None, :], axis=1), E - 1).astype(jnp.int32)
    n_act = (pad_end[-1:] // tm).astype(jnp.int32)
    return _moe_experts(h_all, row_src, row_slot, row_w, blk_exp, n_act, wg, bg, wu, bu, wd, bd)


def _block_id(idx, size):
    return lax.shift_right_logical(idx, int(np.log2(size)))


def _split2(x):
    hi = x.astype(BF16)
    lo = (x - hi.astype(F32)).astype(BF16)
    return hi, lo


def _dot01_lhs(m01, x):
    hi, lo = _split2(x)
    return jnp.dot(m01, hi, preferred_element_type=F32) + jnp.dot(m01, lo, preferred_element_type=F32)


def _wkv_body(r_ref, k_ref, v_ref, pw_ref, pa_ref, pg_ref, w2_ref, a2_ref, g2_ref, w0_ref, a0_ref, kk_ref, ka_ref,
              rk_ref, gw_ref, gb_ref, o_ref, st_ref, rt_s, kt_s, at_s, bt_s, cm_s, bn_s, y_s, *, rev):
    T = WKV_T
    TB, LW = r_ref.shape
    n_pairs = LW // V7X_LANES
    n_chunks = TB // T
    pw_act = jnp.tanh(pw_ref[...]).astype(BF16)
    pa_act = pa_ref[...].astype(BF16)
    pg_act = jax.nn.sigmoid(pg_ref[...]).astype(BF16)

    @pl.when(pl.program_id(2) == 0)
    def _():
        st_ref[...] = jnp.zeros_like(st_ref)

    def earlier(p, q, strict):
        lo, hi = (p, q) if rev else (q, p)
        return lo < hi if strict else lo <= hi

    seg = min(LW, WKV_SEG)
    ri = lax.broadcasted_iota(jnp.int32, (TB, TB), 0)
    ci = lax.broadcasted_iota(jnp.int32, (TB, TB), 1)
    same_chunk = _block_id(ri, T) == _block_id(ci, T)
    tri = jnp.where(same_chunk & earlier(ri, ci, False), 1.0, 0.0).astype(BF16)
    li = _block_id(lax.broadcasted_iota(jnp.int32, (seg, seg), 0), RW_HEAD_DIM)
    lj = _block_id(lax.broadcasted_iota(jnp.int32, (seg, seg), 1), RW_HEAD_DIM)
    head_ones = jnp.where(li == lj, 1.0, 0.0).astype(BF16)

    def head_sum(z):
        return jnp.dot(z.astype(BF16), head_ones, preferred_element_type=F32)

    for sg in range(LW // seg):
        ls = slice(sg * seg, (sg + 1) * seg)
        r = r_ref[:, ls]
        k = k_ref[:, ls]
        e = jax.nn.sigmoid(jnp.dot(pw_act, w2_ref[:, ls], preferred_element_type=F32) + w0_ref[:, ls]) \
            * float(np.exp(-0.5))
        a = jax.nn.sigmoid(jnp.dot(pa_act, a2_ref[:, ls], preferred_element_type=F32) + a0_ref[:, ls])
        kkv = k * kk_ref[:, ls]
        kk = kkv * lax.rsqrt(head_sum(kkv * kkv) + 1e-12)
        kd = k * (1.0 + (a - 1.0) * ka_ref[:, ls])
        be = kk * a
        cum = _dot01_lhs(tri, e)
        ecum = jnp.exp(cum)
        cm_s[:, ls] = cum
        rt_s[:, ls] = r * jnp.exp(-cum)
        kt_s[:, ls] = kd * ecum
        at_s[:, ls] = -kk * jnp.exp(e - cum)
        bt_s[:, ls] = be * ecum
        bn_s[:, ls] = head_sum(r * kd * rk_ref[:, ls]) * v_ref[:, ls]

    lane = lax.broadcasted_iota(jnp.int32, (1, V7X_LANES), 1)
    m_a = jnp.where(lane < RW_HEAD_DIM, 1.0, 0.0).astype(F32)
    m_b = 1.0 - m_a
    pr = lax.broadcasted_iota(jnp.int32, (2 * T, 2 * T), 0)
    pc = lax.broadcasted_iota(jnp.int32, (2 * T, 2 * T), 1)
    same_head = _block_id(pr, T) == _block_id(pc, T)
    incl = same_head & earlier(pr, pc, False)
    strict = same_head & earlier(pr, pc, True)
    hr = _block_id(lax.broadcasted_iota(jnp.int32, (V7X_LANES, V7X_LANES), 0), RW_HEAD_DIM)
    hc = _block_id(lax.broadcasted_iota(jnp.int32, (V7X_LANES, V7X_LANES), 1), RW_HEAD_DIM)
    head_blk = hr == hc
    dn_nt = (((1,), (1,)), ((), ()))
    n_steps = int(np.log2(T))

    def stack(x):
        return jnp.concatenate([x * m_a, x * m_b], axis=0)

    def chunk_body(it, carry):
        c = (n_chunks - 1 - it) if rev else it
        rows = pl.ds(pl.multiple_of(c * T, T), T)
        rows_last = pl.ds(pl.multiple_of(c * T + (0 if rev else T - 8), 8), 8)
        last = slice(0, 1) if rev else slice(7, 8)
        lanes = [slice(p * V7X_LANES, (p + 1) * V7X_LANES) for p in range(n_pairs)]
        pairs = range(n_pairs)
        st = [st_ref[p] for p in pairs]
        st_b = [z.astype(BF16) for z in st]
        v = [v_ref[rows, ln] for ln in lanes]
        r_ms = [stack(rt_s[rows, ln]).astype(BF16) for ln in lanes]
        a_ms = [stack(at_s[rows, ln]).astype(BF16) for ln in lanes]
        k_ms = [stack(kt_s[rows, ln]).astype(BF16) for ln in lanes]
        b_ms = [stack(bt_s[rows, ln]).astype(BF16) for ln in lanes]
        v_ms = [stack(z).astype(BF16) for z in v]
        mm = [lax.dot_general(jnp.concatenate([r_ms[p], a_ms[p]], axis=0),
                              jnp.concatenate([k_ms[p], b_ms[p]], axis=0), dn_nt, preferred_element_type=F32)
              for p in pairs]
        m_rk = [jnp.where(incl, z[:2 * T, :2 * T], 0.0).astype(BF16) for z in mm]
        m_rb = [jnp.where(incl, z[:2 * T, 2 * T:], 0.0).astype(BF16) for z in mm]
        m_ak = [jnp.where(strict, z[2 * T:, :2 * T], 0.0).astype(BF16) for z in mm]
        nmat = [jnp.where(strict, z[2 * T:, 2 * T:], 0.0) for z in mm]
        x = [jnp.dot(jnp.concatenate([a_ms[p], m_ak[p]], axis=1), jnp.concatenate([st_b[p], v_ms[p]], axis=0),
                     preferred_element_type=F32) for p in pairs]
        for s in range(n_steps):
            nb = [z.astype(BF16) for z in nmat]
            if s < n_steps - 1:
                nx = [jnp.dot(nb[p], jnp.concatenate([nb[p], x[p].astype(BF16)], axis=1),
                              preferred_element_type=F32) for p in pairs]
                nmat = [z[:, :2 * T] for z in nx]
                x = [x[p] + nx[p][:, 2 * T:] for p in pairs]
            else:
                x = [x[p] + jnp.dot(nb[p], x[p].astype(BF16), preferred_element_type=F32) for p in pairs]
        y_stk = [jnp.dot(jnp.concatenate([r_ms[p], m_rk[p], m_rb[p]], axis=1),
                         jnp.concatenate([st_b[p], v_ms[p], x[p].astype(BF16)], axis=0),
                         preferred_element_type=F32) for p in pairs]
        w_end = [jnp.exp(-cm_s[rows_last, ln][last]) for ln in lanes]
        upd = []
        for p, ln in enumerate(lanes):
            u = x[p][:T] + x[p][T:]
            kbw_t = (jnp.concatenate([kt_s[rows, ln], bt_s[rows, ln]], axis=0) * w_end[p]).T
            upd.append(jnp.dot(kbw_t.astype(BF16), jnp.concatenate([v[p], u], axis=0).astype(BF16),
                               preferred_element_type=F32))
        for p, ln in enumerate(lanes):
            y_s[rows, ln] = y_stk[p][:T] + y_stk[p][T:]
            w_col = jnp.broadcast_to(w_end[p], (V7X_LANES, V7X_LANES)).T
            st_ref[p] = w_col * st[p] + jnp.where(head_blk, upd[p], 0.0)
        return carry

    lax.fori_loop(0, n_chunks, chunk_body, 0)

    inv_n = 1.0 / RW_HEAD_DIM
    for sg in range(LW // seg):
        ls = slice(sg * seg, (sg + 1) * seg)
        y = y_s[:, ls]
        yc = y - head_sum(y) * inv_n
        var = head_sum(yc * yc) * inv_n
        yn = yc * lax.rsqrt(var + RW_GN_EPS) * gw_ref[:, ls] + gb_ref[:, ls]
        gate = jnp.dot(pg_act, g2_ref[:, ls], preferred_element_type=F32)
        o_ref[:, ls] = ((yn + bn_s[:, ls]) * gate).astype(o_ref.dtype)


def _wkv(r, k, v, pw, pa, pg, col0, w2, a2, g2, w0, a0, k_k, k_a, r_k, gn_w, gn_b, *, B, C, L, d):
    M, D = r.shape
    Lt = C + L
    rev = d == 1
    TB, LW = WKV_BLOCK, min(WKV_LANES, D)
    assert C % TB == 0 and L % TB == 0 and D % LW == 0
    nb, ncb = Lt // TB, C // TB

    def tblk(b, i):
        if not rev:
            return b * nb + i
        return b * nb + jnp.where(i < ncb, ncb - 1 - i, nb - 1 - i + ncb)

    blk = pl.BlockSpec((TB, LW), lambda b, gi, i: (tblk(b, i), gi))
    vec = pl.BlockSpec((1, LW), lambda b, gi, i: (0, gi))
    assert all(c0 % w.shape[0] == 0 for c0, w in zip(col0, (w2, a2, g2)))
    low = lambda w, c0: pl.BlockSpec((TB, w.shape[0]), lambda b, gi, i: (tblk(b, i), c0 // w.shape[0] + d))
    up = lambda w: pl.BlockSpec((w.shape[0], LW), lambda b, gi, i: (0, gi))
    tbuf = pltpu.VMEM((TB, LW), F32)
    row = lambda z: z.reshape(1, D)
    return pl.pallas_call(
        functools.partial(_wkv_body, rev=rev), out_shape=jax.ShapeDtypeStruct((M, D), BF16),
        grid=(B, D // LW, nb),
        in_specs=[blk] * 3 + [low(w2, col0[0]), low(a2, col0[1]), low(g2, col0[2]), up(w2), up(a2), up(g2)]
        + [vec] * 7,
        out_specs=blk,
        scratch_shapes=[pltpu.VMEM((LW // V7X_LANES, V7X_LANES, V7X_LANES), F32)] + [tbuf] * 7,
        compiler_params=_cparams("parallel", "parallel", "arbitrary"), name="wkv7_bwd" if rev else "wkv7_fwd",
    )(r, k, v, pw, pa, pg, w2, a2, g2, row(w0), row(a0), row(k_k), row(k_a), row(r_k), row(gn_w), row(gn_b))


def _lerp(t, ts, mix):
    tf = t.astype(F32)
    return tf + (ts.astype(F32) - tf) * mix


def _rwkv7_time_mix(h, ts, p, *, B, C, L):
    M, D = h.shape
    mix = p["mix"]
    proj = lambda j, w, nm: _matmul([h, ts], w, prologue=_lerp, prologue_vecs=[mix[j].reshape(1, D)], name=nm)
    r = proj(0, p["w_r"].astype(BF16), "rw_r")
    k = proj(2, p["w_k"].astype(BF16), "rw_k")
    v = proj(3, p["w_v"].astype(BF16), "rw_v")
    lg = p["g1"].shape[2]
    lgp = -(-lg // V7X_LANES) * V7X_LANES
    w1c = jnp.concatenate([p["w1"][0], p["w1"][1]], axis=1).astype(BF16)
    a1c = jnp.concatenate([p["a1"][0], p["a1"][1]], axis=1).astype(BF16)
    g1p = jnp.pad(p["g1"], ((0, 0), (0, 0), (0, lgp - lg)))
    g1c = jnp.concatenate([g1p[0], g1p[1]], axis=1).astype(BF16)
    g2p = jnp.pad(p["g2"], ((0, 0), (0, lgp - lg), (0, 0))).astype(BF16)
    tn_low = V7X_LANES * 2
    assert w1c.shape[1] % tn_low == 0 and a1c.shape[1] % tn_low == 0 and g1c.shape[1] % tn_low == 0
    starts = (0, w1c.shape[1] // tn_low, (w1c.shape[1] + a1c.shape[1]) // tn_low)
    low = _matmul([h, ts], jnp.concatenate([w1c, a1c, g1c], axis=1), prologue=_lerp,
                  prologue_vecs=[mix[j].reshape(1, D) for j in (1, 4, 5)], group_starts=starts,
                  tm=512 if M % 512 == 0 else 256, tn=tn_low, name="rw_low")
    pw = pa = pg = low
    col0 = (0, w1c.shape[1], w1c.shape[1] + a1c.shape[1])

    o2 = [_wkv(r, k, v, pw, pa, pg, col0, p["w2"][d].astype(BF16), p["a2"][d].astype(BF16), g2p[d], p["w0"][d],
               p["a0"][d], p["k_k"], p["k_a"], p["r_k"], p["gn_w"], p["gn_b"], B=B, C=C, L=L, d=d)
          for d in range(2)]
    return _matmul([o2[0], o2[1]], p["w_o"].astype(BF16),
                   prologue=lambda a, b: a.astype(F32) + b.astype(F32), name="rw_o")


def _router_params(router_w, router_b):
    D, E = router_w.shape
    rw = jnp.pad(router_w, ((0, 0), (0, V7X_LANES - E))).astype(BF16)
    rb = jnp.pad(router_b, (0, V7X_LANES - E)).reshape(1, V7X_LANES)
    return rw, rb


def kernel(x, c, ctx, c_ctx, l0_ada_w, l0_ada_b, l0_norm_mix, l0_norm_ffn, l0_na_w_qkv, l0_na_q_gain, l0_na_k_gain, l0_na_rpb, l0_na_w_o, l0_router_w, l0_router_b, l0_moe_w_gate, l0_moe_b_gate, l0_moe_w_up, l0_moe_b_up, l0_moe_w_down, l0_moe_b_down, l1_ada_w, l1_ada_b, l1_norm_mix, l1_norm_ffn, l1_rw_mix, l1_rw_w_r, l1_rw_w_k, l1_rw_w_v, l1_rw_k_k, l1_rw_k_a, l1_rw_r_k, l1_rw_w0, l1_rw_w1, l1_rw_w2, l1_rw_a0, l1_rw_a1, l1_rw_a2, l1_rw_g1, l1_rw_g2, l1_rw_gn_w, l1_rw_gn_b, l1_rw_w_o, l1_router_w, l1_router_b, l1_moe_w_gate, l1_moe_b_gate, l1_moe_w_up, l1_moe_b_up, l1_moe_w_down, l1_moe_b_down):
    B, L, D = x.shape
    C = ctx.shape[1]
    Lt = C + L
    M = B * Lt
    H = D // NA_HEAD_DIM
    E = l0_router_w.shape[1]
    assert C % ROW_TILE == 0 and L % ROW_TILE == 0 and L % GRID_W == 0
    tiles = dict(rows_per_batch=Lt, ctx_rows=C)

    mod0 = _ada_mod(c, c_ctx, l0_ada_w, l0_ada_b)
    xa, h = _rows(x.reshape(B * L, D), [], [], mod0, x_ctx=ctx.reshape(B * C, D), norm_w=l0_norm_mix, shift_row=0,
                  scale_row=1, name="l0_norm_mix", **tiles)
    scale = NA_HEAD_DIM ** -0.5
    gain = jnp.concatenate([jnp.tile(l0_na_q_gain * scale, H), jnp.tile(l0_na_k_gain, H), jnp.ones((D,), F32)])
    tm, tn = _mm_tiles(M, D, 3 * D, n_a=1, has_prologue=False, a_bytes=2, w_bytes=2, out_bytes=2)
    assert D % tn == 0 and tn % NA_HEAD_DIM == 0
    qkv = _matmul([h], l0_na_w_qkv.astype(BF16), out_dtypes=(BF16,),
                  epilogue=functools.partial(_qkv_epilogue, n_norm_tiles=2 * D // tn),
                  epi_vecs=[gain.reshape(1, 3 * D)], tm=tm, tn=tn, name="na_qkv")
    bias_tabs = _na_bias_tables(l0_na_rpb, L // GRID_W)
    att = _neighbourhood_attention(qkv, bias_tabs, B=B, C=C, L=L, D=D)
    y = _matmul([att], l0_na_w_o.astype(BF16), name="na_o")
    rw0, rb0 = _router_params(l0_router_w, l0_router_b)
    xa, h2, logits = _rows(xa, [y], [2], mod0, norm_w=l0_norm_ffn, shift_row=3, scale_row=4, h_dtype=F32,
                           router=(rw0, rb0), name="l0_norm_ffn", **tiles)
    ys = _moe_ffn(h2, logits[:, :E], l0_moe_w_gate.astype(BF16), l0_moe_b_gate, l0_moe_w_up.astype(BF16),
                  l0_moe_b_up, l0_moe_w_down.astype(BF16), l0_moe_b_down, tok_rows_per_batch=Lt, batch_stride=Lt,
                  row_offset=0)
    (xa,) = _rows(xa, [ys] * TOP_K, [5] * TOP_K, mod0, add_row_offsets=[k * M for k in range(TOP_K)],
                  tr=ROW_TILE // 2, name="l0_moe_residual", **tiles)

    mod1 = _ada_mod(c, c_ctx, l1_ada_w, l1_ada_b)
    h, ts = _norm_shift(xa, mod1, l1_norm_mix, **tiles)
    rwp = dict(mix=l1_rw_mix, w_r=l1_rw_w_r, w_k=l1_rw_w_k, w_v=l1_rw_w_v, k_k=l1_rw_k_k, k_a=l1_rw_k_a,
               r_k=l1_rw_r_k, w0=l1_rw_w0, w1=l1_rw_w1, w2=l1_rw_w2, a0=l1_rw_a0, a1=l1_rw_a1, a2=l1_rw_a2,
               g1=l1_rw_g1, g2=l1_rw_g2, gn_w=l1_rw_gn_w, gn_b=l1_rw_gn_b, w_o=l1_rw_w_o)
    y = _rwkv7_time_mix(h, ts, rwp, B=B, C=C, L=L)
    rw1, rb1 = _router_params(l1_router_w, l1_router_b)
    xa, h2, logits = _rows(xa, [y], [2], mod1, norm_w=l1_norm_ffn, shift_row=3, scale_row=4, h_dtype=F32,
                           router=(rw1, rb1), name="l1_norm_ffn", **tiles)
    lat_logits = logits.reshape(B, Lt, -1)[:, C:, :E].reshape(B * L, E)
    ys = _moe_ffn(h2, lat_logits, l1_moe_w_gate.astype(BF16), l1_moe_b_gate, l1_moe_w_up.astype(BF16),
                  l1_moe_b_up, l1_moe_w_down.astype(BF16), l1_moe_b_down, tok_rows_per_batch=L, batch_stride=Lt,
                  row_offset=C)
    (out,) = _rows(xa, [ys] * TOP_K, [5] * TOP_K, mod1, add_row_offsets=[k * B * L for k in range(TOP_K)],
                   rows_per_batch=L, ctx_rows=0, x_rows_per_batch=Lt, x_row_offset=C, tr=ROW_TILE // 2,
                   name="l1_moe_residual")
    return out.reshape(B, L, D)
```

```python
import functools

import numpy as np
import jax
import jax.numpy as jnp
from jax import lax
from jax.experimental import pallas as pl
from jax.experimental.pallas import tpu as pltpu

F32 = jnp.float32
BF16 = jnp.bfloat16

GRID_W = 64
N_MOD = 6
RMS_EPS = 1e-6
NEG_INF = -1e30
NA_HEAD_DIM = 128
NA_WIN_H = 8
NA_WIN_W = 16
RW_HEAD_DIM = 64
RW_GN_EPS = 64e-5
TOP_K = 4
SWIGLU_LIMIT = 7.0
SWIGLU_ALPHA = 1.702

V7X_LANES = 128
VMEM_LIMIT_BYTES = 56 * 1024 * 1024
MM_VMEM_BUDGET_BYTES = 44 * 1024 * 1024

ROW_TILE = 256
NA_QROWS = 4
NA_KROWS = 12
NA_UNROLL = 4
MOE_TILE = 256
WKV_T = 64
WKV_BLOCK = 256
WKV_LANES = 2048
WKV_SEG = 256


def _cparams(*sem, **kw):
    return pltpu.CompilerParams(dimension_semantics=sem, vmem_limit_bytes=VMEM_LIMIT_BYTES, **kw)


def _mm_body(*refs, n_a, n_pv, n_ev, n_em, n_out, prologue, epilogue, chunk):
    a_refs = refs[:n_a]
    pv_refs = refs[n_a:n_a + n_pv]
    w_ref = refs[n_a + n_pv]
    base = n_a + n_pv + 1
    ev_refs = refs[base:base + n_ev]
    em_refs = refs[base + n_ev:base + n_ev + n_em]
    out_refs = refs[base + n_ev + n_em:base + n_ev + n_em + n_out]
    scratch = refs[base + n_ev + n_em + n_out:]
    if prologue is not None:
        a_scr = scratch[0]

        @pl.when(pl.program_id(1) == 0)
        def _():
            pvs = [r[...] for r in pv_refs]

            def body(c, carry):
                rows = pl.ds(pl.multiple_of(c * chunk, chunk), chunk)
                a_scr[rows, :] = prologue(*[r[rows, :] for r in a_refs], *pvs).astype(BF16)
                return carry

            lax.fori_loop(0, a_scr.shape[0] // chunk, body, 0)

        a = a_scr[...]
    else:
        a = a_refs[0][...]
    acc = jnp.dot(a, w_ref[...].astype(BF16), preferred_element_type=F32)
    evs = [r[...] for r in ev_refs]
    ems = [r[...] for r in em_refs]
    if epilogue is None:
        out_refs[0][...] = acc.astype(out_refs[0].dtype)
    else:
        epilogue(acc, *evs, *ems, *out_refs)


def _mm_tiles(M, K, N, *, n_a, has_prologue, a_bytes, w_bytes, out_bytes):
    for tm, tn in ((768, 1024), (512, 1024), (768, 512), (512, 512), (256, 512), (256, 256)):
        if M % tm or N % tn:
            continue
        need = 2 * n_a * tm * K * a_bytes + 2 * K * tn * w_bytes + 2 * tm * tn * out_bytes
        need += tm * K * 2 if has_prologue else 0
        if need <= MM_VMEM_BUDGET_BYTES:
            return tm, tn
    return (256 if M % 256 == 0 else M), (256 if N % 256 == 0 else N)


def _matmul(a_list, w, *, out_dtypes=(F32,), prologue=None, prologue_vecs=(), epilogue=None,
            epi_vecs=(), epi_mats=(), a_col=0, tm=None, tn=None, name="mm"):
    M = a_list[0].shape[0]
    K, N = w.shape
    if tm is None or tn is None:
        tm, tn = _mm_tiles(M, K, N, n_a=len(a_list), has_prologue=prologue is not None,
                           a_bytes=a_list[0].dtype.itemsize, w_bytes=w.dtype.itemsize,
                           out_bytes=sum(jnp.dtype(dt).itemsize for dt in out_dtypes) + 4 * len(epi_mats))
    assert M % tm == 0 and N % tn == 0
    chunk = 32 if tm % 32 == 0 else tm
    in_specs = [pl.BlockSpec((tm, K), lambda m, n: (m, a_col)) for _ in a_list]
    in_specs += [pl.BlockSpec((1, K), lambda m, n: (0, 0)) for _ in prologue_vecs]
    in_specs += [pl.BlockSpec((K, tn), lambda m, n: (0, n))]
    in_specs += [pl.BlockSpec((1, tn), lambda m, n: (0, n)) for _ in epi_vecs]
    in_specs += [pl.BlockSpec((tm, tn), lambda m, n: (m, n)) for _ in epi_mats]
    out_shape = [jax.ShapeDtypeStruct((M, N), dt) for dt in out_dtypes]
    out_specs = [pl.BlockSpec((tm, tn), lambda m, n: (m, n)) for _ in out_dtypes]
    scratch = [pltpu.VMEM((tm, K), BF16)] if prologue is not None else []
    body = functools.partial(_mm_body, n_a=len(a_list), n_pv=len(prologue_vecs), n_ev=len(epi_vecs),
                             n_em=len(epi_mats), n_out=len(out_dtypes), prologue=prologue, epilogue=epilogue,
                             chunk=chunk)
    outs = pl.pallas_call(
        body, out_shape=out_shape, grid=(M // tm, N // tn), in_specs=in_specs, out_specs=out_specs,
        scratch_shapes=scratch, compiler_params=_cparams("parallel", "arbitrary"), name=name,
    )(*a_list, *prologue_vecs, w, *epi_vecs, *epi_mats)
    return outs[0] if len(out_dtypes) == 1 else outs


def _rows_body(*refs, n_add, gate_rows, ctx_src_tiles, do_norm, shift_row, scale_row, has_router, write_x, chunk):
    x_ref = refs[0]
    n_src = 1 if ctx_src_tiles is None else 2
    refs = refs[n_src - 1:]
    ctx_ref = refs[0]
    add_refs = refs[1:1 + n_add]
    mod_ref = refs[1 + n_add]
    pos = 2 + n_add
    if do_norm:
        nw_ref = refs[pos]
        pos += 1
    if has_router:
        rw_ref, rb_ref = refs[pos], refs[pos + 1]
        pos += 2
    outs = refs[pos:]
    oi = 0
    if write_x:
        xo_ref = outs[oi]
        oi += 1
    if do_norm:
        h_ref = outs[oi]
        oi += 1
    if has_router:
        lg_ref = outs[oi]
    gates = [mod_ref[0, g:g + 1, :] for g in gate_rows]
    if do_norm:
        nw = nw_ref[...]
        sh = mod_ref[0, shift_row:shift_row + 1, :]
        sc = 1.0 + mod_ref[0, scale_row:scale_row + 1, :]

    def run(src_ref):
        def body(c, carry):
            rows = pl.ds(pl.multiple_of(c * chunk, chunk), chunk)
            x = src_ref[rows, :]
            for a_ref, g in zip(add_refs, gates):
                x = x + g * a_ref[rows, :].astype(F32)
            if write_x:
                xo_ref[rows, :] = x
            if do_norm:
                y = x * lax.rsqrt(jnp.mean(x * x, axis=-1, keepdims=True) + RMS_EPS)
                h_ref[rows, :] = ((y * nw) * sc + sh).astype(h_ref.dtype)
            return carry

        lax.fori_loop(0, src_ref.shape[0] // chunk, body, 0)

    if ctx_src_tiles is None:
        run(x_ref)
    else:
        tiles_per_batch, ctx_tiles = ctx_src_tiles
        is_ctx = lax.rem(pl.program_id(0), tiles_per_batch) < ctx_tiles
        pl.when(is_ctx)(functools.partial(run, ctx_ref))
        pl.when(jnp.logical_not(is_ctx))(functools.partial(run, x_ref))
    if has_router:
        lg_ref[...] = jnp.dot(h_ref[...].astype(BF16), rw_ref[...], preferred_element_type=F32) + rb_ref[...]


def _rows(x, adds, gate_rows, mod, *, rows_per_batch, ctx_rows, x_rows_per_batch=None, x_row_offset=0, x_ctx=None,
          add_row_offsets=None, norm_w=None, shift_row=0, scale_row=1, h_dtype=None, router=None, write_x=True,
          tr=ROW_TILE, name="rows"):
    D = x.shape[1]
    B = mod.shape[0] // 2
    M = B * rows_per_batch
    x_rows_per_batch = x_rows_per_batch or rows_per_batch
    assert rows_per_batch % tr == 0 and ctx_rows % tr == 0 and x_rows_per_batch % tr == 0 and x_row_offset % tr == 0
    tiles_per_batch, ctx_tiles = rows_per_batch // tr, ctx_rows // tr
    x_tiles_per_batch, x_tile_offset = x_rows_per_batch // tr, x_row_offset // tr
    add_row_offsets = list(add_row_offsets or [0] * len(adds))
    assert all(off % tr == 0 for off in add_row_offsets)
    do_norm = norm_w is not None

    def mod_idx(i):
        return ((i // tiles_per_batch) * 2 + ((i % tiles_per_batch) >= ctx_tiles).astype(jnp.int32), 0, 0)

    row_spec = pl.BlockSpec((tr, D), lambda i: (i, 0))
    x_spec = pl.BlockSpec((tr, D), lambda i: ((i // tiles_per_batch) * x_tiles_per_batch + x_tile_offset
                                               + i % tiles_per_batch, 0))
    add_specs = [pl.BlockSpec((tr, D), lambda i, t=off // tr: (t + i, 0)) for off in add_row_offsets]
    src = [x]
    if x_ctx is not None:
        lat_tiles = tiles_per_batch - ctx_tiles
        x_spec = pl.BlockSpec((tr, D), lambda i: ((i // tiles_per_batch) * lat_tiles
                                                   + jnp.maximum(i % tiles_per_batch - ctx_tiles, 0), 0))
        ctx_spec = pl.BlockSpec((tr, D), lambda i: ((i // tiles_per_batch) * ctx_tiles
                                                     + jnp.minimum(i % tiles_per_batch, ctx_tiles - 1), 0))
        src = [x, x_ctx]
    args = [*src, *adds, mod]
    in_specs = [x_spec] + ([ctx_spec] if x_ctx is not None else []) + add_specs + [pl.BlockSpec((1, 8, D), mod_idx)]
    if do_norm:
        args.append(norm_w.reshape(1, D))
        in_specs.append(pl.BlockSpec((1, D), lambda i: (0, 0)))
    if router is not None:
        rw, rb = router
        args += [rw, rb]
        in_specs += [pl.BlockSpec(rw.shape, lambda i: (0, 0)), pl.BlockSpec(rb.shape, lambda i: (0, 0))]
    out_shape, out_specs = [], []
    if write_x:
        out_shape.append(jax.ShapeDtypeStruct((M, D), F32))
        out_specs.append(row_spec)
    if do_norm:
        out_shape.append(jax.ShapeDtypeStruct((M, D), h_dtype or BF16))
        out_specs.append(row_spec)
    if router is not None:
        out_shape.append(jax.ShapeDtypeStruct((M, V7X_LANES), F32))
        out_specs.append(pl.BlockSpec((tr, V7X_LANES), lambda i: (i, 0)))
    body = functools.partial(_rows_body, n_add=len(adds), gate_rows=tuple(gate_rows),
                             ctx_src_tiles=(tiles_per_batch, ctx_tiles) if x_ctx is not None else None,
                             do_norm=do_norm, shift_row=shift_row, scale_row=scale_row,
                             has_router=router is not None, write_x=write_x, chunk=32)
    return pl.pallas_call(body, out_shape=out_shape, grid=(M // tr,), in_specs=in_specs, out_specs=out_specs,
                          compiler_params=_cparams("parallel"), name=name)(*args)


def _norm_shift_body(x_ref, xp_ref, xn_ref, mod_ref, nw_ref, h_ref, ts_ref, hs, *, tiles_per_batch, ctx_tiles, chunk):
    tr, D = x_ref.shape
    half = D // 2
    nw = nw_ref[...]
    sh = mod_ref[0, 0:1, :]
    sc = 1.0 + mod_ref[0, 1:2, :]

    def norm(x):
        return ((x * lax.rsqrt(jnp.mean(x * x, axis=-1, keepdims=True) + RMS_EPS)) * nw) * sc + sh

    j = lax.rem(pl.program_id(0), tiles_per_batch)
    first = jnp.logical_or(j == 0, j == ctx_tiles)
    last = jnp.logical_or(j == ctx_tiles - 1, j == tiles_per_batch - 1)
    hs[0:8, :] = jnp.where(first, 0.0, norm(xp_ref[...]))
    hs[tr + 8:tr + 16, :] = jnp.where(last, 0.0, norm(xn_ref[...]))

    def norm_rows(c, carry):
        rows = pl.ds(pl.multiple_of(c * chunk, chunk), chunk)
        h = norm(x_ref[rows, :])
        h_ref[rows, :] = h.astype(h_ref.dtype)
        hs[pl.ds(pl.multiple_of(c * chunk + 8, 8), chunk), :] = h
        return carry

    lax.fori_loop(0, tr // chunk, norm_rows, 0)

    def shift_rows(c, carry):
        win = pl.ds(pl.multiple_of(c * chunk, 8), chunk + 16)
        prev = pltpu.roll(hs[win, :half], 1, axis=0)[8:8 + chunk]
        nxt = pltpu.roll(hs[win, half:], chunk + 15, axis=0)[8:8 + chunk]
        rows = pl.ds(pl.multiple_of(c * chunk, chunk), chunk)
        ts_ref[rows, :] = jnp.concatenate([prev, nxt], axis=1).astype(ts_ref.dtype)
        return carry

    lax.fori_loop(0, tr // chunk, shift_rows, 0)


def _norm_shift(x, mod, norm_w, *, rows_per_batch, ctx_rows):
    M, D = x.shape
    tr = ROW_TILE
    tiles_per_batch, ctx_tiles = rows_per_batch // tr, ctx_rows // tr
    assert rows_per_batch % tr == 0 and ctx_rows % tr == 0 and ctx_tiles >= 1
    sub = tr // 8

    def mod_idx(i):
        return ((i // tiles_per_batch) * 2 + ((i % tiles_per_batch) >= ctx_tiles).astype(jnp.int32), 0, 0)

    row_spec = pl.BlockSpec((tr, D), lambda i: (i, 0))
    body = functools.partial(_norm_shift_body, tiles_per_batch=tiles_per_batch, ctx_tiles=ctx_tiles, chunk=32)
    return pl.pallas_call(
        body, out_shape=[jax.ShapeDtypeStruct((M, D), BF16)] * 2, grid=(M // tr,),
        in_specs=[row_spec,
                  pl.BlockSpec((8, D), lambda i: (jnp.maximum(i * sub - 1, 0), 0)),
                  pl.BlockSpec((8, D), lambda i: (jnp.minimum((i + 1) * sub, M // 8 - 1), 0)),
                  pl.BlockSpec((1, 8, D), mod_idx), pl.BlockSpec((1, D), lambda i: (0, 0))],
        out_specs=[row_spec, row_spec], scratch_shapes=[pltpu.VMEM((tr + 16, D), F32)],
        compiler_params=_cparams("parallel"), name="norm_shift")(x, x, x, mod, norm_w.reshape(1, D))


def _ada_mod(c, c_ctx, ada_w, ada_b):
    B, D = c.shape
    cond = jnp.concatenate([c, c_ctx[None, :], jnp.zeros((8 - B - 1, D), F32)], axis=0)

    def epi(acc, bias, out_ref):
        out_ref[...] = acc + bias

    out = _matmul([cond], ada_w, prologue=lambda a: a * jax.nn.sigmoid(a), epilogue=epi,
                  epi_vecs=[ada_b.reshape(1, -1)], tm=8, tn=512, name="ada_mod")
    lat = out[:B].reshape(B, N_MOD, D)
    ctx = jnp.broadcast_to(out[B].reshape(1, N_MOD, D), (B, N_MOD, D))
    mod = jnp.stack([ctx, lat], axis=1).reshape(2 * B, N_MOD, D)
    return jnp.pad(mod, ((0, 0), (0, 8 - N_MOD), (0, 0)))


def _qkv_epilogue(acc, gain, out_ref, *, n_norm_tiles):
    tn = acc.shape[1]

    @pl.when(pl.program_id(1) < n_norm_tiles)
    def _():
        for i in range(tn // NA_HEAD_DIM):
            sl = slice(i * NA_HEAD_DIM, (i + 1) * NA_HEAD_DIM)
            z = acc[:, sl]
            z = z * lax.rsqrt(jnp.mean(z * z, axis=-1, keepdims=True) + RMS_EPS)
            out_ref[:, sl] = (z * gain[:, sl]).astype(out_ref.dtype)

    @pl.when(pl.program_id(1) >= n_norm_tiles)
    def _():
        out_ref[...] = acc.astype(out_ref.dtype)


def _na_bias_tables(rpb, rows):
    nblk = rows // NA_QROWS
    kh = min(NA_WIN_H, rows)
    assert kh == NA_WIN_H and rows >= NA_KROWS + NA_QROWS and rows % NA_QROWS == 0
    col = np.arange(GRID_W)
    col_start = np.clip(col - NA_WIN_W // 2, 0, GRID_W - NA_WIN_W)
    col_ok = (col[None, :] >= col_start[:, None]) & (col[None, :] < col_start[:, None] + NA_WIN_W)
    dc = np.clip(col[None, :] - col[:, None] + NA_WIN_W - 1, 0, 2 * NA_WIN_W - 2)
    tabs = []
    for j in (0, 1, nblk - 1):
        s0 = int(np.clip(NA_QROWS * j - NA_WIN_H // 2, 0, rows - NA_KROWS))
        qr = NA_QROWS * j + np.arange(NA_QROWS)
        rs = np.clip(qr - kh // 2, 0, rows - kh)
        kr = s0 + np.arange(NA_KROWS)
        row_ok = (kr[None, :] >= rs[:, None]) & (kr[None, :] < rs[:, None] + kh)
        dr = np.clip(kr[None, :] - qr[:, None] + NA_WIN_H - 1, 0, 2 * NA_WIN_H - 2)
        ok = row_ok[:, None, :, None] & col_ok[None, :, None, :]
        ok_f = ok.reshape(NA_QROWS * GRID_W, NA_KROWS * GRID_W)
        by_row = rpb.astype(F32)[:, dr.reshape(-1), :]
        spread = np.zeros((2 * NA_WIN_W - 1, GRID_W * GRID_W), np.float32)
        spread[dc.reshape(-1), np.arange(GRID_W * GRID_W)] = 1.0
        full = jnp.einsum("hpc,cq->hpq", by_row, jnp.asarray(spread), precision=lax.Precision.HIGHEST)
        full = full.reshape(-1, NA_QROWS, NA_KROWS, GRID_W, GRID_W).transpose(0, 1, 3, 2, 4)
        tabs.append(jnp.where(ok_f[None], full.reshape(-1, NA_QROWS * GRID_W, NA_KROWS * GRID_W), NEG_INF))
    return jnp.stack(tabs, axis=0)


def _na_body(q_ref, k_ref, v_ref, bias_ref, o_ref, *, C, rows):
    nq = NA_QROWS * GRID_W
    nk = NA_KROWS * GRID_W
    nblk = rows // NA_QROWS
    dn = (((1,), (1,)), ((), ()))
    kc = k_ref[0:C, :]
    vc = v_ref[0:C, :]
    s = lax.dot_general(q_ref[0:C, :], kc, dn, preferred_element_type=F32)
    p = jnp.exp(s - jnp.max(s, axis=-1, keepdims=True))
    o = jnp.dot(p.astype(BF16), vc, preferred_element_type=F32) / jnp.sum(p, axis=-1, keepdims=True)
    o_ref[0:C, :] = o.astype(o_ref.dtype)

    def body(jj, carry):
        js = [jj * NA_UNROLL + t for t in range(NA_UNROLL)]
        s0 = [jnp.clip(NA_QROWS * j - NA_WIN_H // 2, 0, rows - NA_KROWS) for j in js]
        cls = [jnp.where(j == 0, 0, jnp.where(j == nblk - 1, 2, 1)) for j in js]
        qrows = [pl.ds(pl.multiple_of(C + j * nq, GRID_W), nq) for j in js]
        krows = [pl.ds(pl.multiple_of(C + s * GRID_W, GRID_W), nk) for s in s0]
        qb = [q_ref[r, :] for r in qrows]
        s_lat = [lax.dot_general(q, k_ref[r, :], dn, preferred_element_type=F32) + bias_ref[c, 0]
                 for q, r, c in zip(qb, krows, cls)]
        s_ctx = [lax.dot_general(q, kc, dn, preferred_element_type=F32) for q in qb]
        m = [jnp.maximum(jnp.max(a, axis=-1, keepdims=True), jnp.max(b, axis=-1, keepdims=True))
             for a, b in zip(s_lat, s_ctx)]
        p_lat = [jnp.exp(a - mm) for a, mm in zip(s_lat, m)]
        p_ctx = [jnp.exp(b - mm) for b, mm in zip(s_ctx, m)]
        den = [jnp.sum(a, axis=-1, keepdims=True) + jnp.sum(b, axis=-1, keepdims=True)
               for a, b in zip(p_lat, p_ctx)]
        o = [jnp.dot(b.astype(BF16), vc, preferred_element_type=F32)
             + jnp.dot(a.astype(BF16), v_ref[r, :], preferred_element_type=F32)
             for a, b, r in zip(p_lat, p_ctx, krows)]
        for r, oo, dd in zip(qrows, o, den):
            o_ref[r, :] = (oo / dd).astype(o_ref.dtype)
        return carry

    assert nblk % NA_UNROLL == 0
    lax.fori_loop(0, nblk // NA_UNROLL, body, 0)


def _neighbourhood_attention(qkv, bias_tabs, *, B, C, L, D):
    H = D // NA_HEAD_DIM
    Lt = C + L
    rows = L // GRID_W
    blk = (Lt, NA_HEAD_DIM)
    body = functools.partial(_na_body, C=C, rows=rows)
    return pl.pallas_call(
        body, out_shape=jax.ShapeDtypeStruct((B * Lt, D), BF16), grid=(B, H),
        in_specs=[pl.BlockSpec(blk, lambda b, h: (b, h)),
                  pl.BlockSpec(blk, lambda b, h: (b, H + h)),
                  pl.BlockSpec(blk, lambda b, h: (b, 2 * H + h)),
                  pl.BlockSpec((3, 1) + bias_tabs.shape[2:], lambda b, h: (0, h, 0, 0))],
        out_specs=pl.BlockSpec(blk, lambda b, h: (b, h)),
        compiler_params=_cparams("parallel", "parallel"), name="na_attention",
    )(qkv, qkv, qkv, bias_tabs)


def _moe_body(be_ref, na_ref, tok0_ref, tokn_ref, slot_ref, h_hbm, w_ref, wg_ref, bg_ref, wu_ref, bu_ref, wd_ref,
              bd_ref, ys_hbm, xbuf, ybuf, gsem, ssem):
    del be_ref
    tm = xbuf.shape[1]
    i = pl.program_id(0)
    n_act = na_ref[0]
    cur = lax.rem(i, 2)

    def start_gather(tok_ref, buf):
        for r in range(tm):
            pltpu.make_async_copy(h_hbm.at[pl.ds(tok_ref[0, 0, r], 1)], xbuf.at[buf, pl.ds(r, 1)],
                                  gsem.at[buf]).start()

    def wait_gather(buf):
        pltpu.make_async_copy(h_hbm.at[pl.ds(0, tm)], xbuf.at[buf], gsem.at[buf]).wait()

    def wait_scatter(buf):
        pltpu.make_async_copy(ybuf.at[buf], ys_hbm.at[pl.ds(0, tm)], ssem.at[buf]).wait()

    @pl.when(i == 0)
    def _():
        start_gather(tok0_ref, 0)

    @pl.when(i < n_act)
    def _():
        start_gather(tokn_ref, 1 - cur)
        wait_gather(cur)
        x = xbuf[cur].astype(BF16)
        g = jnp.minimum(jnp.dot(x, wg_ref[0], preferred_element_type=F32) + bg_ref[0], SWIGLU_LIMIT)
        u = jnp.clip(jnp.dot(x, wu_ref[0], preferred_element_type=F32) + bu_ref[0], -SWIGLU_LIMIT, SWIGLU_LIMIT)
        hdn = g * jax.nn.sigmoid(SWIGLU_ALPHA * g) * (u + 1.0)
        y = jnp.dot(hdn.astype(BF16), wd_ref[0], preferred_element_type=F32) + bd_ref[0]

        @pl.when(i >= 2)
        def _():
            wait_scatter(cur)

        ybuf[cur] = y * w_ref[...]

        for r in range(tm):
            pltpu.make_async_copy(ybuf.at[cur, pl.ds(r, 1)], ys_hbm.at[pl.ds(slot_ref[0, 0, r], 1)],
                                  ssem.at[cur]).start(priority=r % 2)

    @pl.when(i == n_act - 1)
    def _():
        wait_scatter(1 - cur)
        wait_scatter(cur)
        wait_gather(1 - cur)

    @pl.when(i == n_act)
    def _():
        ybuf[0] = jnp.zeros(ybuf.shape[1:], ybuf.dtype)

    @pl.when(i >= n_act)
    def _():
        fill = pltpu.make_async_copy(ybuf.at[0], ys_hbm.at[pl.ds(pl.multiple_of(i * tm, tm), tm)], ssem.at[0])
        fill.start()
        fill.wait()


def _moe_experts(h_all, row_src, row_slot, row_w, blk_exp, n_act, wg, bg, wu, bu, wd, bd):
    M, D = h_all.shape
    E, _, De = wg.shape
    tm = MOE_TILE
    n_rows = row_src.shape[0]
    n_blocks = n_rows // tm
    assert n_blocks >= 2
    smem_blk = lambda idx: pl.BlockSpec((1, 1, tm), idx, memory_space=pltpu.SMEM)
    grid_spec = pltpu.PrefetchScalarGridSpec(
        num_scalar_prefetch=2, grid=(n_blocks,),
        in_specs=[smem_blk(lambda i, be, na: (0, 0, 0)),
                  smem_blk(lambda i, be, na: (jnp.minimum(i + 1, n_blocks - 1), 0, 0)),
                  smem_blk(lambda i, be, na: (i, 0, 0)),
                  pl.BlockSpec(memory_space=pl.ANY),
                  pl.BlockSpec((tm, 1), lambda i, be, na: (i, 0)),
                  pl.BlockSpec((1, D, De), lambda i, be, na: (be[i], 0, 0)),
                  pl.BlockSpec((1, 1, De), lambda i, be, na: (be[i], 0, 0)),
                  pl.BlockSpec((1, D, De), lambda i, be, na: (be[i], 0, 0)),
                  pl.BlockSpec((1, 1, De), lambda i, be, na: (be[i], 0, 0)),
                  pl.BlockSpec((1, De, D), lambda i, be, na: (be[i], 0, 0)),
                  pl.BlockSpec((1, 1, D), lambda i, be, na: (be[i], 0, 0))],
        out_specs=pl.BlockSpec(memory_space=pl.ANY),
        scratch_shapes=[pltpu.VMEM((2, tm, D), F32), pltpu.VMEM((2, tm, D), F32),
                        pltpu.SemaphoreType.DMA((2,)), pltpu.SemaphoreType.DMA((2,))])
    src3 = row_src.reshape(n_blocks, 1, tm)
    return pl.pallas_call(
        _moe_body, out_shape=jax.ShapeDtypeStruct((n_rows, D), F32), grid_spec=grid_spec,
        compiler_params=_cparams("arbitrary", disable_bounds_checks=True), name="moe_experts",
    )(blk_exp, n_act, src3, src3, row_slot.reshape(n_blocks, 1, tm), h_all, row_w.reshape(n_rows, 1), wg,
      bg.reshape(E, 1, De), wu, bu.reshape(E, 1, De), wd, bd.reshape(E, 1, D))


def _moe_ffn(h_all, logits, wg, bg, wu, bu, wd, bd, *, tok_rows_per_batch, batch_stride, row_offset):
    N = logits.shape[0]
    E = wg.shape[0]
    tm = MOE_TILE
    top_val, top_idx = lax.top_k(logits, TOP_K)
    top_w = jax.nn.softmax(top_val, axis=-1)
    flat_e = top_idx.reshape(-1)
    onehot = (flat_e[:, None] == jnp.arange(E, dtype=flat_e.dtype)[None, :]).astype(jnp.int32)
    csum = jnp.cumsum(onehot, axis=0)
    counts = csum[-1]
    rank = jnp.take_along_axis(csum, flat_e[:, None], axis=1)[:, 0] - 1
    padded = (counts + tm - 1) // tm * tm
    pad_end = jnp.cumsum(padded)
    pad_start = pad_end - padded
    dest = pad_start[flat_e] + rank
    n_blocks = (N * TOP_K + tm - 1) // tm + E
    n_rows = n_blocks * tm
    asg = jnp.arange(N * TOP_K, dtype=jnp.int32)
    packed = jnp.stack([asg + 1, lax.bitcast_convert_type(top_w.reshape(-1), jnp.int32)], axis=1)
    rows = jnp.zeros((n_rows, 2), jnp.int32).at[dest].set(packed)
    row_asg = rows[:, 0] - 1
    is_pad = row_asg < 0
    tok = jnp.maximum(row_asg, 0) // TOP_K
    row_src = (tok // tok_rows_per_batch) * batch_stride + row_offset + tok % tok_rows_per_batch
    row_w = lax.bitcast_convert_type(rows[:, 1], F32)
    row_slot = jnp.where(is_pad, N * TOP_K - 1 + jnp.cumsum(is_pad.astype(jnp.int32)),
                         (row_asg % TOP_K) * N + tok)
    blk_start = jnp.arange(n_blocks, dtype=jnp.int32) * tm
    blk_exp = jnp.minimum(jnp.sum(blk_start[:, None] >= pad_end[None, :], axis=1), E - 1).astype(jnp.int32)
    n_act = (pad_end[-1:] // tm).astype(jnp.int32)
    return _moe_experts(h_all, row_src, row_slot, row_w, blk_exp, n_act, wg, bg, wu, bu, wd, bd)


def _block_id(idx, size):
    return lax.shift_right_logical(idx, int(np.log2(size)))


def _split2(x):
    hi = x.astype(BF16)
    lo = (x - hi.astype(F32)).astype(BF16)
    return hi, lo


def _dot01_lhs(m01, x):
    hi, lo = _split2(x)
    return jnp.dot(m01, hi, preferred_element_type=F32) + jnp.dot(m01, lo, preferred_element_type=F32)


def _wkv_body(r_ref, k_ref, v_ref, pw_ref, pa_ref, pg_ref, w2_ref, a2_ref, g2_ref, w0_ref, a0_ref, kk_ref, ka_ref,
              rk_ref, gw_ref, gb_ref, o_ref, st_ref, rt_s, kt_s, at_s, bt_s, cm_s, bn_s, y_s, *, rev):
    T = WKV_T
    TB, LW = r_ref.shape
    n_pairs = LW // V7X_LANES
    n_chunks = TB // T
    pw_act = jnp.tanh(pw_ref[...]).astype(BF16)
    pa_act = pa_ref[...].astype(BF16)
    pg_act = jax.nn.sigmoid(pg_ref[...]).astype(BF16)

    @pl.when(pl.program_id(2) == 0)
    def _():
        st_ref[...] = jnp.zeros_like(st_ref)

    def earlier(p, q, strict):
        lo, hi = (p, q) if rev else (q, p)
        return lo < hi if strict else lo <= hi

    seg = min(LW, WKV_SEG)
    ri = lax.broadcasted_iota(jnp.int32, (TB, TB), 0)
    ci = lax.broadcasted_iota(jnp.int32, (TB, TB), 1)
    same_chunk = _block_id(ri, T) == _block_id(ci, T)
    tri = jnp.where(same_chunk & earlier(ri, ci, False), 1.0, 0.0).astype(BF16)
    li = _block_id(lax.broadcasted_iota(jnp.int32, (seg, seg), 0), RW_HEAD_DIM)
    lj = _block_id(lax.broadcasted_iota(jnp.int32, (seg, seg), 1), RW_HEAD_DIM)
    head_ones = jnp.where(li == lj, 1.0, 0.0).astype(BF16)

    def head_sum(z):
        return jnp.dot(z.astype(BF16), head_ones, preferred_element_type=F32)

    for sg in range(LW // seg):
        ls = slice(sg * seg, (sg + 1) * seg)
        r = r_ref[:, ls]
        k = k_ref[:, ls]
        e = jax.nn.sigmoid(jnp.dot(pw_act, w2_ref[:, ls], preferred_element_type=F32) + w0_ref[:, ls]) \
            * float(np.exp(-0.5))
        a = jax.nn.sigmoid(jnp.dot(pa_act, a2_ref[:, ls], preferred_element_type=F32) + a0_ref[:, ls])
        kkv = k * kk_ref[:, ls]
        kk = kkv * lax.rsqrt(head_sum(kkv * kkv) + 1e-12)
        kd = k * (1.0 + (a - 1.0) * ka_ref[:, ls])
        be = kk * a
        cum = _dot01_lhs(tri, e)
        ecum = jnp.exp(cum)
        cm_s[:, ls] = cum
        rt_s[:, ls] = r * jnp.exp(-cum)
        kt_s[:, ls] = kd * ecum
        at_s[:, ls] = -kk * jnp.exp(e - cum)
        bt_s[:, ls] = be * ecum
        bn_s[:, ls] = head_sum(r * kd * rk_ref[:, ls]) * v_ref[:, ls]

    lane = lax.broadcasted_iota(jnp.int32, (1, V7X_LANES), 1)
    m_a = jnp.where(lane < RW_HEAD_DIM, 1.0, 0.0).astype(F32)
    m_b = 1.0 - m_a
    pr = lax.broadcasted_iota(jnp.int32, (2 * T, 2 * T), 0)
    pc = lax.broadcasted_iota(jnp.int32, (2 * T, 2 * T), 1)
    same_head = _block_id(pr, T) == _block_id(pc, T)
    incl = same_head & earlier(pr, pc, False)
    strict = same_head & earlier(pr, pc, True)
    hr = _block_id(lax.broadcasted_iota(jnp.int32, (V7X_LANES, V7X_LANES), 0), RW_HEAD_DIM)
    hc = _block_id(lax.broadcasted_iota(jnp.int32, (V7X_LANES, V7X_LANES), 1), RW_HEAD_DIM)
    head_blk = hr == hc
    dn_nt = (((1,), (1,)), ((), ()))
    n_steps = int(np.log2(T))

    def stack(x):
        return jnp.concatenate([x * m_a, x * m_b], axis=0)

    def chunk_body(it, carry):
        c = (n_chunks - 1 - it) if rev else it
        rows = pl.ds(pl.multiple_of(c * T, T), T)
        rows_last = pl.ds(pl.multiple_of(c * T + (0 if rev else T - 8), 8), 8)
        last = slice(0, 1) if rev else slice(7, 8)
        lanes = [slice(p * V7X_LANES, (p + 1) * V7X_LANES) for p in range(n_pairs)]
        pairs = range(n_pairs)
        st = [st_ref[p] for p in pairs]
        st_b = [z.astype(BF16) for z in st]
        v = [v_ref[rows, ln] for ln in lanes]
        r_ms = [stack(rt_s[rows, ln]).astype(BF16) for ln in lanes]
        a_ms = [stack(at_s[rows, ln]).astype(BF16) for ln in lanes]
        k_ms = [stack(kt_s[rows, ln]).astype(BF16) for ln in lanes]
        b_ms = [stack(bt_s[rows, ln]).astype(BF16) for ln in lanes]
        v_ms = [stack(z).astype(BF16) for z in v]
        mm = [lax.dot_general(jnp.concatenate([r_ms[p], a_ms[p]], axis=0),
                              jnp.concatenate([k_ms[p], b_ms[p]], axis=0), dn_nt, preferred_element_type=F32)
              for p in pairs]
        m_rk = [jnp.where(incl, z[:2 * T, :2 * T], 0.0).astype(BF16) for z in mm]
        m_rb = [jnp.where(incl, z[:2 * T, 2 * T:], 0.0).astype(BF16) for z in mm]
        m_ak = [jnp.where(strict, z[2 * T:, :2 * T], 0.0).astype(BF16) for z in mm]
        nmat = [jnp.where(strict, z[2 * T:, 2 * T:], 0.0) for z in mm]
        x = [jnp.dot(jnp.concatenate([a_ms[p], m_ak[p]], axis=1), jnp.concatenate([st_b[p], v_ms[p]], axis=0),
                     preferred_element_type=F32) for p in pairs]
        for s in range(n_steps):
            nb = [z.astype(BF16) for z in nmat]
            if s < n_steps - 1:
                nx = [jnp.dot(nb[p], jnp.concatenate([nb[p], x[p].astype(BF16)], axis=1),
                              preferred_element_type=F32) for p in pairs]
                nmat = [z[:, :2 * T] for z in nx]
                x = [x[p] + nx[p][:, 2 * T:] for p in pairs]
            else:
                x = [x[p] + jnp.dot(nb[p], x[p].astype(BF16), preferred_element_type=F32) for p in pairs]
        y_stk = [jnp.dot(jnp.concatenate([r_ms[p], m_rk[p], m_rb[p]], axis=1),
                         jnp.concatenate([st_b[p], v_ms[p], x[p].astype(BF16)], axis=0),
                         preferred_element_type=F32) for p in pairs]
        w_end = [jnp.exp(-cm_s[rows_last, ln][last]) for ln in lanes]
        upd = []
        for p, ln in enumerate(lanes):
            u = x[p][:T] + x[p][T:]
            kbw_t = (jnp.concatenate([kt_s[rows, ln], bt_s[rows, ln]], axis=0) * w_end[p]).T
            upd.append(jnp.dot(kbw_t.astype(BF16), jnp.concatenate([v[p], u], axis=0).astype(BF16),
                               preferred_element_type=F32))
        for p, ln in enumerate(lanes):
            y_s[rows, ln] = y_stk[p][:T] + y_stk[p][T:]
            w_col = jnp.broadcast_to(w_end[p], (V7X_LANES, V7X_LANES)).T
            st_ref[p] = w_col * st[p] + jnp.where(head_blk, upd[p], 0.0)
        return carry

    lax.fori_loop(0, n_chunks, chunk_body, 0)

    inv_n = 1.0 / RW_HEAD_DIM
    for sg in range(LW // seg):
        ls = slice(sg * seg, (sg + 1) * seg)
        y = y_s[:, ls]
        yc = y - head_sum(y) * inv_n
        var = head_sum(yc * yc) * inv_n
        yn = yc * lax.rsqrt(var + RW_GN_EPS) * gw_ref[:, ls] + gb_ref[:, ls]
        gate = jnp.dot(pg_act, g2_ref[:, ls], preferred_element_type=F32)
        o_ref[:, ls] = ((yn + bn_s[:, ls]) * gate).astype(o_ref.dtype)


def _wkv(r, k, v, pw, pa, pg, w2, a2, g2, w0, a0, k_k, k_a, r_k, gn_w, gn_b, *, B, C, L, d):
    M, D = r.shape
    Lt = C + L
    rev = d == 1
    TB, LW = WKV_BLOCK, min(WKV_LANES, D)
    assert C % TB == 0 and L % TB == 0 and D % LW == 0
    nb, ncb = Lt // TB, C // TB

    def tblk(b, i):
        if not rev:
            return b * nb + i
        return b * nb + jnp.where(i < ncb, ncb - 1 - i, nb - 1 - i + ncb)

    blk = pl.BlockSpec((TB, LW), lambda b, gi, i: (tblk(b, i), gi))
    vec = pl.BlockSpec((1, LW), lambda b, gi, i: (0, gi))
    low = lambda w: pl.BlockSpec((TB, w.shape[0]), lambda b, gi, i: (tblk(b, i), d))
    up = lambda w: pl.BlockSpec((w.shape[0], LW), lambda b, gi, i: (0, gi))
    tbuf = pltpu.VMEM((TB, LW), F32)
    row = lambda z: z.reshape(1, D)
    return pl.pallas_call(
        functools.partial(_wkv_body, rev=rev), out_shape=jax.ShapeDtypeStruct((M, D), BF16),
        grid=(B, D // LW, nb),
        in_specs=[blk] * 3 + [low(w2), low(a2), low(g2), up(w2), up(a2), up(g2)] + [vec] * 7,
        out_specs=blk,
        scratch_shapes=[pltpu.VMEM((LW // V7X_LANES, V7X_LANES, V7X_LANES), F32)] + [tbuf] * 7,
        compiler_params=_cparams("parallel", "parallel", "arbitrary"), name="wkv7_bwd" if rev else "wkv7_fwd",
    )(r, k, v, pw, pa, pg, w2, a2, g2, row(w0), row(a0), row(k_k), row(k_a), row(r_k), row(gn_w), row(gn_b))


def _lerp(t, ts, mix):
    tf = t.astype(F32)
    return tf + (ts.astype(F32) - tf) * mix


def _rwkv7_time_mix(h, ts, p, *, B, C, L):
    M, D = h.shape
    mix = p["mix"]
    proj = lambda j, w, nm: _matmul([h, ts], w, prologue=_lerp, prologue_vecs=[mix[j].reshape(1, D)], name=nm)
    r = proj(0, p["w_r"].astype(BF16), "rw_r")
    k = proj(2, p["w_k"].astype(BF16), "rw_k")
    v = proj(3, p["w_v"].astype(BF16), "rw_v")
    lg = p["g1"].shape[2]
    lgp = -(-lg // V7X_LANES) * V7X_LANES
    w1c = jnp.concatenate([p["w1"][0], p["w1"][1]], axis=1).astype(BF16)
    a1c = jnp.concatenate([p["a1"][0], p["a1"][1]], axis=1).astype(BF16)
    g1p = jnp.pad(p["g1"], ((0, 0), (0, 0), (0, lgp - lg)))
    g1c = jnp.concatenate([g1p[0], g1p[1]], axis=1).astype(BF16)
    g2p = jnp.pad(p["g2"], ((0, 0), (0, lgp - lg), (0, 0))).astype(BF16)
    pw = proj(1, w1c, "rw_w1")
    pa = proj(4, a1c, "rw_a1")
    pg = proj(5, g1c, "rw_g1")

    o2 = [_wkv(r, k, v, pw, pa, pg, p["w2"][d].astype(BF16), p["a2"][d].astype(BF16), g2p[d], p["w0"][d],
               p["a0"][d], p["k_k"], p["k_a"], p["r_k"], p["gn_w"], p["gn_b"], B=B, C=C, L=L, d=d)
          for d in range(2)]
    return _matmul([o2[0], o2[1]], p["w_o"].astype(BF16),
                   prologue=lambda a, b: a.astype(F32) + b.astype(F32), name="rw_o")


def _router_params(router_w, router_b):
    D, E = router_w.shape
    rw = jnp.pad(router_w, ((0, 0), (0, V7X_LANES - E))).astype(BF16)
    rb = jnp.pad(router_b, (0, V7X_LANES - E)).reshape(1, V7X_LANES)
    return rw, rb


def kernel(x, c, ctx, c_ctx, l0_ada_w, l0_ada_b, l0_norm_mix, l0_norm_ffn, l0_na_w_qkv, l0_na_q_gain, l0_na_k_gain, l0_na_rpb, l0_na_w_o, l0_router_w, l0_router_b, l0_moe_w_gate, l0_moe_b_gate, l0_moe_w_up, l0_moe_b_up, l0_moe_w_down, l0_moe_b_down, l1_ada_w, l1_ada_b, l1_norm_mix, l1_norm_ffn, l1_rw_mix, l1_rw_w_r, l1_rw_w_k, l1_rw_w_v, l1_rw_k_k, l1_rw_k_a, l1_rw_r_k, l1_rw_w0, l1_rw_w1, l1_rw_w2, l1_rw_a0, l1_rw_a1, l1_rw_a2, l1_rw_g1, l1_rw_g2, l1_rw_gn_w, l1_rw_gn_b, l1_rw_w_o, l1_router_w, l1_router_b, l1_moe_w_gate, l1_moe_b_gate, l1_moe_w_up, l1_moe_b_up, l1_moe_w_down, l1_moe_b_down):
    B, L, D = x.shape
    C = ctx.shape[1]
    Lt = C + L
    M = B * Lt
    H = D // NA_HEAD_DIM
    E = l0_router_w.shape[1]
    assert C % ROW_TILE == 0 and L % ROW_TILE == 0 and L % GRID_W == 0
    tiles = dict(rows_per_batch=Lt, ctx_rows=C)

    mod0 = _ada_mod(c, c_ctx, l0_ada_w, l0_ada_b)
    xa, h = _rows(x.reshape(B * L, D), [], [], mod0, x_ctx=ctx.reshape(B * C, D), norm_w=l0_norm_mix, shift_row=0,
                  scale_row=1, name="l0_norm_mix", **tiles)
    scale = NA_HEAD_DIM ** -0.5
    gain = jnp.concatenate([jnp.tile(l0_na_q_gain * scale, H), jnp.tile(l0_na_k_gain, H), jnp.ones((D,), F32)])
    tm, tn = _mm_tiles(M, D, 3 * D, n_a=1, has_prologue=False, a_bytes=2, w_bytes=2, out_bytes=2)
    assert D % tn == 0 and tn % NA_HEAD_DIM == 0
    qkv = _matmul([h], l0_na_w_qkv.astype(BF16), out_dtypes=(BF16,),
                  epilogue=functools.partial(_qkv_epilogue, n_norm_tiles=2 * D // tn),
                  epi_vecs=[gain.reshape(1, 3 * D)], tm=tm, tn=tn, name="na_qkv")
    bias_tabs = _na_bias_tables(l0_na_rpb, L // GRID_W)
    att = _neighbourhood_attention(qkv, bias_tabs, B=B, C=C, L=L, D=D)
    y = _matmul([att], l0_na_w_o.astype(BF16), name="na_o")
    rw0, rb0 = _router_params(l0_router_w, l0_router_b)
    xa, h2, logits = _rows(xa, [y], [2], mod0, norm_w=l0_norm_ffn, shift_row=3, scale_row=4, h_dtype=F32,
                           router=(rw0, rb0), name="l0_norm_ffn", **tiles)
    ys = _moe_ffn(h2, logits[:, :E], l0_moe_w_gate.astype(BF16), l0_moe_b_gate, l0_moe_w_up.astype(BF16),
                  l0_moe_b_up, l0_moe_w_down.astype(BF16), l0_moe_b_down, tok_rows_per_batch=Lt, batch_stride=Lt,
                  row_offset=0)
    (xa,) = _rows(xa, [ys] * TOP_K, [5] * TOP_K, mod0, add_row_offsets=[k * M for k in range(TOP_K)],
                  tr=ROW_TILE // 2, name="l0_moe_residual", **tiles)

    mod1 = _ada_mod(c, c_ctx, l1_ada_w, l1_ada_b)
    h, ts = _norm_shift(xa, mod1, l1_norm_mix, **tiles)
    rwp = dict(mix=l1_rw_mix, w_r=l1_rw_w_r, w_k=l1_rw_w_k, w_v=l1_rw_w_v, k_k=l1_rw_k_k, k_a=l1_rw_k_a,
               r_k=l1_rw_r_k, w0=l1_rw_w0, w1=l1_rw_w1, w2=l1_rw_w2, a0=l1_rw_a0, a1=l1_rw_a1, a2=l1_rw_a2,
               g1=l1_rw_g1, g2=l1_rw_g2, gn_w=l1_rw_gn_w, gn_b=l1_rw_gn_b, w_o=l1_rw_w_o)
    y = _rwkv7_time_mix(h, ts, rwp, B=B, C=C, L=L)
    rw1, rb1 = _router_params(l1_router_w, l1_router_b)
    xa, h2, logits = _rows(xa, [y], [2], mod1, norm_w=l1_norm_ffn, shift_row=3, scale_row=4, h_dtype=F32,
                           router=(rw1, rb1), name="l1_norm_ffn", **tiles)
    lat_logits = logits.reshape(B, Lt, -1)[:, C:, :E].reshape(B * L, E)
    ys = _moe_ffn(h2, lat_logits, l1_moe_w_gate.astype(BF16), l1_moe_b_gate, l1_moe_w_up.astype(BF16),
                  l1_moe_b_up, l1_moe_w_down.astype(BF16), l1_moe_b_down, tok_rows_per_batch=L, batch_stride=Lt,
                  row_offset=C)
    (out,) = _rows(xa, [ys] * TOP_K, [5] * TOP_K, mod1, add_row_offsets=[k * B * L for k in range(TOP_K)],
                   rows_per_batch=L, ctx_rows=0, x_rows_per_batch=Lt, x_row_offset=C, tr=ROW_TILE // 2,
                   name="l1_moe_residual")
    return out.reshape(B, L, D)
```
